```python
import jax
import jax.numpy as jnp
from jax import lax
import numpy as np

D_MODEL = 4096
BATCH = 4
SEQ = 2048
DEPTH = 2

GRID_W = 64
CTX_LEN = 256
NORM_EPS = 1e-6
ATTN_HEAD_DIM = 128
ATTN_HEADS = D_MODEL // 2 // ATTN_HEAD_DIM
ATTN_KV_HEADS = ATTN_HEADS // 4
ATTN_WINDOW = 128
ATTN_BLOCK = 128
ROPE_THETA = 10000.0
MLSTM_HEADS = 4
MLSTM_V_DIM = D_MODEL // 2 // MLSTM_HEADS
MLSTM_QK_DIM = MLSTM_V_DIM // 2
MLSTM_CHUNK = 128
GLA_QK_DIM = 256
GLA_V_DIM = 512
GLA_HEADS = D_MODEL // GLA_V_DIM
GLA_GATE_RANK = 16
GLA_GATE_TEMP = 16.0
GLA_CHUNK = 64
D_FF = 11008
CONV_WIDTH = 3

AB_SPLITS = (ATTN_HEADS * ATTN_HEAD_DIM, ATTN_KV_HEADS * ATTN_HEAD_DIM, ATTN_KV_HEADS * ATTN_HEAD_DIM,
             MLSTM_HEADS * MLSTM_QK_DIM, MLSTM_HEADS * MLSTM_QK_DIM, MLSTM_HEADS * MLSTM_V_DIM,
             MLSTM_HEADS * MLSTM_V_DIM, 4 * MLSTM_HEADS)
AB_IN = sum(AB_SPLITS)
AB_OUT = ATTN_HEADS * ATTN_HEAD_DIM + MLSTM_HEADS * MLSTM_V_DIM
GLA_SPLITS = (GLA_HEADS * GLA_QK_DIM, GLA_HEADS * GLA_QK_DIM, GLA_HEADS * GLA_V_DIM, GLA_HEADS * GLA_V_DIM,
              GLA_GATE_RANK, GLA_GATE_RANK)
GLA_IN = sum(GLA_SPLITS)

kernel_name = 'hybrid_swa_mlstm_gla_convffn_prefix_dit'


def rmsnorm(x, w):
    xf = x.astype(jnp.float32)
    y = xf * lax.rsqrt(jnp.mean(xf * xf, axis=-1, keepdims=True) + NORM_EPS)
    return (y * w.astype(jnp.float32)).astype(x.dtype)


def modulate(h, shift, scale):
    return (h * (1 + scale) + shift).astype(h.dtype)


def split_cols(a, sizes):
    idx, acc = [], 0
    for s in sizes[:-1]:
        acc += s
        idx.append(acc)
    return jnp.split(a, idx, axis=-1)


def to_heads(a, n_heads):
    b, l, _ = a.shape
    return a.reshape(b, l, n_heads, -1).transpose(0, 2, 1, 3)


def axial_rope_tables(seq_len, head_dim):
    rows = seq_len // GRID_W
    row = jnp.repeat(jnp.arange(rows), GRID_W).astype(jnp.float32)
    col = jnp.tile(jnp.arange(GRID_W), rows).astype(jnp.float32)
    n_freq = head_dim // 4
    inv = ROPE_THETA ** (-jnp.arange(n_freq, dtype=jnp.float32) / n_freq)
    ang_r = row[:, None] * inv
    ang_c = col[:, None] * inv
    return (jnp.cos(ang_r), jnp.sin(ang_r), jnp.cos(ang_c), jnp.sin(ang_c))


def apply_axial_rope(x, tables):
    cr, sr, cc, sc = (t[None, :, None, :] for t in tables)
    xr, xcol = jnp.split(x.astype(jnp.float32), 2, axis=-1)

    def rot(z, cos, sin):
        z1, z2 = jnp.split(z, 2, axis=-1)
        return jnp.concatenate([z1 * cos - z2 * sin, z2 * cos + z1 * sin], axis=-1)

    return jnp.concatenate([rot(xr, cr, sr), rot(xcol, cc, sc)], axis=-1).astype(x.dtype)


def softmax_with_sink(logits, sink):
    m = jnp.maximum(logits.max(axis=-1, keepdims=True), sink)
    e = jnp.exp(logits - m)
    return e / (e.sum(axis=-1, keepdims=True) + jnp.exp(sink - m))


def window_attention(q, k, v, k_ctx, v_ctx, sink):
    b, l, h, dh = q.shape
    hkv = k.shape[2]
    g = h // hkv
    blk = ATTN_BLOCK
    nb = l // blk
    qb = q.reshape(b, nb, blk, hkv, g, dh)

    def band(a):
        ap = jnp.pad(a, ((0, 0), (blk, blk), (0, 0), (0, 0))).reshape(b, nb + 2, blk, hkv, dh)
        return jnp.concatenate([ap[:, :-2], ap[:, 1:-1], ap[:, 2:]], axis=2)

    kb, vb = band(k), band(v)
    qpos = jnp.arange(l).reshape(nb, blk)
    kpos = (jnp.arange(nb)[:, None] - 1) * blk + jnp.arange(3 * blk)[None, :]
    mask = ((jnp.abs(qpos[:, :, None] - kpos[:, None, :]) <= ATTN_WINDOW)
            & (kpos[:, None, :] >= 0) & (kpos[:, None, :] < l))
    scale = dh ** -0.5
    s_lat = jnp.einsum('bnqhgd,bnkhd->bnhgqk', qb, kb).astype(jnp.float32) * scale
    s_lat = jnp.where(mask[None, :, None, None], s_lat, -jnp.inf)
    s_ctx = jnp.einsum('bnqhgd,bchd->bnhgqc', qb, k_ctx).astype(jnp.float32) * scale
    sink_b = sink.astype(jnp.float32).reshape(hkv, g)[:, :, None, None]
    p = softmax_with_sink(jnp.concatenate([s_lat, s_ctx], axis=-1), sink_b)
    p_lat = p[..., :3 * blk].astype(v.dtype)
    p_ctx = p[..., 3 * blk:].astype(v.dtype)
    o = (jnp.einsum('bnhgqk,bnkhd->bnqhgd', p_lat, vb)
         + jnp.einsum('bnhgqc,bchd->bnqhgd', p_ctx, v_ctx))
    return o.reshape(b, l, h * dh)


def context_attention(q, k, v, sink):
    b, lc, h, dh = q.shape
    hkv = k.shape[2]
    g = h // hkv
    qg = q.reshape(b, lc, hkv, g, dh)
    s = jnp.einsum('bqhgd,bkhd->bhgqk', qg, k).astype(jnp.float32) * dh ** -0.5
    p = softmax_with_sink(s, sink.astype(jnp.float32).reshape(hkv, g)[:, :, None, None]).astype(v.dtype)
    return jnp.einsum('bhgqk,bkhd->bqhgd', p, v).reshape(b, lc, h * dh)


def mlstm_scan(q, k, v, ig, lf, state, chunk=MLSTM_CHUNK):
    q, k, v, ig, lf = (a.astype(jnp.float32) for a in (q, k, v, ig, lf))
    b, h, l, dk = q.shape
    dv = v.shape[-1]
    nc = l // chunk

    def to_chunks(a):
        return jnp.moveaxis(a.reshape(b, h, nc, chunk, *a.shape[3:]), 2, 0)

    causal = jnp.tril(jnp.ones((chunk, chunk), dtype=bool))

    def step(carry, inp):
        c_st, n_st, m_st = carry
        qb, kb, vb, ib, fb = inp
        bc = jnp.cumsum(fb, axis=-1)
        log_d = jnp.where(causal, bc[..., :, None] - bc[..., None, :] + ib[..., None, :], -jnp.inf)
        inter = bc + m_st[..., None]
        m_t = jnp.maximum(inter, log_d.max(axis=-1))
        s = jnp.einsum('bhtd,bhsd->bhts', qb, kb) * jnp.exp(log_d - m_t[..., None])
        w_inter = jnp.exp(inter - m_t)
        num = (jnp.einsum('bhts,bhsv->bhtv', s, vb)
               + w_inter[..., None] * jnp.einsum('bhtd,bhvd->bhtv', qb, c_st))
        den = s.sum(axis=-1) + w_inter * jnp.einsum('bhtd,bhd->bht', qb, n_st)
        h_t = num / jnp.maximum(jnp.abs(den), jnp.exp(-m_t))[..., None]
        b_end = bc[..., -1]
        m_new = m_t[..., -1]
        wk = jnp.exp(b_end[..., None] - bc + ib - m_new[..., None])
        decay = jnp.exp(b_end + m_st - m_new)
        c_new = decay[..., None, None] * c_st + jnp.einsum('bhsv,bhsd->bhvd', vb * wk[..., None], kb)
        n_new = decay[..., None] * n_st + jnp.einsum('bhs,bhsd->bhd', wk, kb)
        return (c_new, n_new, m_new), h_t

    state, hs = lax.scan(step, state, tuple(map(to_chunks, (q, k, v, ig, lf))))
    return jnp.moveaxis(hs, 0, 2).reshape(b, h, l, dv), state


def gla_scan(q, k, v, lg, state, chunk=GLA_CHUNK):
    q, k, v, lg = (a.astype(jnp.float32) for a in (q, k, v, lg))
    b, h, l, dk = q.shape
    dv = v.shape[-1]
    nc = l // chunk

    def to_chunks(a):
        return jnp.moveaxis(a.reshape(b, h, nc, chunk, a.shape[-1]), 2, 0)

    causal = jnp.tril(jnp.ones((chunk, chunk), dtype=bool))

    def step(s_st, inp):
        qb, kb, vb, gb = inp
        gcum = jnp.cumsum(gb, axis=2)
        q_dec = qb * jnp.exp(gcum)
        k_inv = kb * jnp.exp(-gcum)
        att = jnp.where(causal, jnp.einsum('bhtd,bhsd->bhts', q_dec, k_inv), 0.0)
        o = jnp.einsum('bhts,bhsv->bhtv', att, vb) + jnp.einsum('bhtd,bhdv->bhtv', q_dec, s_st)
        g_end = gcum[:, :, -1:, :]
        k_end = kb * jnp.exp(g_end - gcum)
        s_new = jnp.exp(g_end[:, :, 0, :])[..., None] * s_st + jnp.einsum('bhsd,bhsv->bhdv', k_end, vb)
        return s_new, o

    state, os_ = lax.scan(step, state, tuple(map(to_chunks, (q, k, v, lg))))
    return jnp.moveaxis(os_, 0, 2).reshape(b, h, l, dv), state


def two_segment_scan(scan_fn, ctx_in, lat_in, state0, reverse):
    flip = (lambda a: jnp.flip(a, axis=2)) if reverse else (lambda a: a)
    h_ctx, st = scan_fn(*map(flip, ctx_in), state0)
    h_lat, _ = scan_fn(*map(flip, lat_in), st)
    return flip(h_ctx), flip(h_lat)


def head_output(h, gain, gate):
    b, nh, l, dv = h.shape
    hn = h * lax.rsqrt(jnp.mean(h * h, axis=-1, keepdims=True) + NORM_EPS)
    hn = hn.transpose(0, 2, 1, 3).reshape(b, l, nh * dv)
    return hn * gain.astype(jnp.float32) * gate


def mlstm_gates(g, gate_b):
    b, l, _ = g.shape
    g = (g.astype(jnp.float32) + gate_b.astype(jnp.float32)).reshape(b, l, 4, MLSTM_HEADS).transpose(2, 0, 3, 1)
    return g[0], jax.nn.log_sigmoid(g[1]), g[2], jax.nn.log_sigmoid(g[3])


def mixer_attn_mlstm(h_lat, h_ctx, w_in, gate_b, sink, head_norm, w_out, rope, with_ctx):
    b, l, _ = h_lat.shape
    lc = h_ctx.shape[1]
    lat = split_cols(h_lat @ w_in, AB_SPLITS)
    cx = split_cols(h_ctx @ w_in, AB_SPLITS)

    def attn_qkv(parts, n):
        return (parts[0].reshape(b, n, ATTN_HEADS, ATTN_HEAD_DIM),
                parts[1].reshape(b, n, ATTN_KV_HEADS, ATTN_HEAD_DIM),
                parts[2].reshape(b, n, ATTN_KV_HEADS, ATTN_HEAD_DIM))

    qa_l, ka_l, va_l = attn_qkv(lat, l)
    qa_l = apply_axial_rope(qa_l, rope)
    ka_l = apply_axial_rope(ka_l, rope)
    qa_c, ka_c, va_c = attn_qkv(cx, lc)
    a_lat = window_attention(qa_l, ka_l, va_l, ka_c, va_c, sink)

    def mlstm_qkv(parts):
        return (to_heads(parts[3], MLSTM_HEADS),
                to_heads(parts[4], MLSTM_HEADS) * MLSTM_QK_DIM ** -0.5,
                to_heads(parts[5], MLSTM_HEADS))

    qkv_c, qkv_l = mlstm_qkv(cx), mlstm_qkv(lat)
    if_c, lf_c, ib_c, lb_c = mlstm_gates(cx[7], gate_b)
    if_l, lf_l, ib_l, lb_l = mlstm_gates(lat[7], gate_b)
    zero = jnp.zeros((b, MLSTM_HEADS), jnp.float32)
    state0 = (jnp.zeros((b, MLSTM_HEADS, MLSTM_V_DIM, MLSTM_QK_DIM), jnp.float32),
              jnp.zeros((b, MLSTM_HEADS, MLSTM_QK_DIM), jnp.float32), zero)
    hc_f, hl_f = two_segment_scan(mlstm_scan, (*qkv_c, if_c, lf_c), (*qkv_l, if_l, lf_l), state0, False)
    hc_b, hl_b = two_segment_scan(mlstm_scan, (*qkv_c, ib_c, lb_c), (*qkv_l, ib_l, lb_l), state0, True)
    m_lat = head_output(hl_f + hl_b, head_norm, jax.nn.sigmoid(lat[6].astype(jnp.float32)))
    y_lat = jnp.concatenate([a_lat, m_lat.astype(a_lat.dtype)], axis=-1) @ w_out
    if not with_ctx:
        return y_lat, None
    a_ctx = context_attention(qa_c, ka_c, va_c, sink)
    m_ctx = head_output(hc_f + hc_b, head_norm, jax.nn.sigmoid(cx[6].astype(jnp.float32)))
    y_ctx = jnp.concatenate([a_ctx, m_ctx.astype(a_ctx.dtype)], axis=-1) @ w_out
    return y_lat, y_ctx


def mixer_gla(h_lat, h_ctx, w_in, gate_w2, gate_b, head_norm, w_out, with_ctx):
    b = h_lat.shape[0]
    lat = split_cols(h_lat @ w_in, GLA_SPLITS)
    cx = split_cols(h_ctx @ w_in, GLA_SPLITS)

    def qkv_gates(parts):
        q = to_heads(parts[0], GLA_HEADS).astype(jnp.float32) * GLA_QK_DIM ** -0.5
        k = to_heads(parts[1], GLA_HEADS)
        v = to_heads(parts[2], GLA_HEADS)
        lg = [to_heads(jax.nn.log_sigmoid(parts[4 + d].astype(jnp.float32) @ gate_w2[d] + gate_b[d]) / GLA_GATE_TEMP,
                       GLA_HEADS) for d in range(2)]
        return (q, k, v), lg

    qkv_c, lg_c = qkv_gates(cx)
    qkv_l, lg_l = qkv_gates(lat)
    state0 = jnp.zeros((b, GLA_HEADS, GLA_QK_DIM, GLA_V_DIM), jnp.float32)
    oc_f, ol_f = two_segment_scan(gla_scan, (*qkv_c, lg_c[0]), (*qkv_l, lg_l[0]), state0, False)
    oc_b, ol_b = two_segment_scan(gla_scan, (*qkv_c, lg_c[1]), (*qkv_l, lg_l[1]), state0, True)
    y_lat = head_output(ol_f + ol_b, head_norm, jax.nn.silu(lat[3].astype(jnp.float32))).astype(h_lat.dtype) @ w_out
    if not with_ctx:
        return y_lat, None
    y_ctx = head_output(oc_f + oc_b, head_norm, jax.nn.silu(cx[3].astype(jnp.float32))).astype(h_ctx.dtype) @ w_out
    return y_lat, y_ctx


def conv_ffn(h, w_up, conv_w, conv_b, w_down):
    u = h @ w_up
    l = u.shape[1]
    pad = CONV_WIDTH // 2
    up = jnp.pad(u, ((0, 0), (pad, pad), (0, 0)))
    u = conv_b + sum(up[:, j:j + l] * conv_w[j] for j in range(CONV_WIDTH))
    gate, val = jnp.split(u, 2, axis=-1)
    return (jax.nn.silu(gate) * val) @ w_down


def setup_inputs(seed: int = 0) -> dict:
    key = jax.random.key(seed)
    ks = jax.random.split(key, 24)
    n_even = (DEPTH + 1) // 2
    n_odd = DEPTH // 2
    d = D_MODEL

    def nrm(k, shape, scale):
        return jax.random.normal(k, shape, jnp.float32) * scale

    return {
        'x': nrm(ks[0], (BATCH, SEQ, d), 1.0),
        'c': nrm(ks[1], (BATCH, d), 1.0),
        'ctx': nrm(ks[2], (BATCH, CTX_LEN, d), 1.0),
        'c_ctx': nrm(ks[3], (d,), 1.0),
        'ada_w': nrm(ks[4], (DEPTH, d, 6 * d), d ** -0.5),
        'ada_b': nrm(ks[5], (DEPTH, 6 * d), 0.01),
        'norm_w': 1.0 + nrm(ks[6], (DEPTH, 4, d), 0.02),
        'ab_w_in': nrm(ks[7], (n_even, d, AB_IN), d ** -0.5),
        'ab_gate_b': jnp.repeat(jnp.array([0.0, 3.0, 0.0, 3.0], jnp.float32), MLSTM_HEADS)
                     + nrm(ks[8], (n_even, 4 * MLSTM_HEADS), 0.1),
        'ab_sink': nrm(ks[9], (n_even, ATTN_HEADS), 0.5),
        'ab_head_norm': 1.0 + nrm(ks[10], (n_even, MLSTM_HEADS * MLSTM_V_DIM), 0.02),
        'ab_w_out': nrm(ks[11], (n_even, AB_OUT, d), AB_OUT ** -0.5),
        'gla_w_in': nrm(ks[12], (n_odd, d, GLA_IN), d ** -0.5),
        'gla_gate_w2': nrm(ks[13], (n_odd, 2, GLA_GATE_RANK, GLA_HEADS * GLA_QK_DIM), GLA_GATE_RANK ** -0.5),
        'gla_gate_b': nrm(ks[14], (n_odd, 2, GLA_HEADS * GLA_QK_DIM), 0.1),
        'gla_head_norm': 1.0 + nrm(ks[15], (n_odd, GLA_HEADS * GLA_V_DIM), 0.02),
        'gla_w_out': nrm(ks[16], (n_odd, GLA_HEADS * GLA_V_DIM, d), (GLA_HEADS * GLA_V_DIM) ** -0.5),
        'ffn_w_up': nrm(ks[17], (DEPTH, d, 2 * D_FF), d ** -0.5),
        'ffn_conv_w': nrm(ks[18], (DEPTH, CONV_WIDTH, 2 * D_FF), CONV_WIDTH ** -0.5),
        'ffn_conv_b': nrm(ks[19], (DEPTH, 2 * D_FF), 0.01),
        'ffn_w_down': nrm(ks[20], (DEPTH, D_FF, d), D_FF ** -0.5),
    }


def reference(x, c, ctx, c_ctx, ada_w, ada_b, norm_w, ab_w_in, ab_gate_b, ab_sink, ab_head_norm, ab_w_out,
              gla_w_in, gla_gate_w2, gla_gate_b, gla_head_norm, gla_w_out,
              ffn_w_up, ffn_conv_w, ffn_conv_b, ffn_w_down):
    rope = axial_rope_tables(x.shape[1], ATTN_HEAD_DIM)
    silu_c = jax.nn.silu(c)
    silu_cc = jax.nn.silu(c_ctx)
    for i in range(DEPTH):
        last = i == DEPTH - 1
        j = i // 2
        mod_l = jnp.split((silu_c @ ada_w[i] + ada_b[i])[:, None, :], 6, axis=-1)
        mod_c = jnp.split(silu_cc @ ada_w[i] + ada_b[i], 6, axis=-1)
        nw = norm_w[i]
        h_l = modulate(rmsnorm(x, nw[0]), mod_l[0], mod_l[1])
        h_c = modulate(rmsnorm(ctx, nw[0]), mod_c[0], mod_c[1])
        if i % 2 == 0:
            y_l, y_c = mixer_attn_mlstm(h_l, h_c, ab_w_in[j], ab_gate_b[j], ab_sink[j], ab_head_norm[j],
                                        ab_w_out[j], rope, not last)
        else:
            y_l, y_c = mixer_gla(h_l, h_c, gla_w_in[j], gla_gate_w2[j], gla_gate_b[j], gla_head_norm[j],
                                 gla_w_out[j], not last)
        x = x + mod_l[2] * rmsnorm(y_l, nw[1])
        f_l = conv_ffn(modulate(rmsnorm(x, nw[2]), mod_l[3], mod_l[4]),
                       ffn_w_up[i], ffn_conv_w[i], ffn_conv_b[i], ffn_w_down[i])
        x = x + mod_l[5] * rmsnorm(f_l, nw[3])
        if not last:
            ctx = ctx + mod_c[2] * rmsnorm(y_c, nw[1])
            f_c = conv_ffn(modulate(rmsnorm(ctx, nw[2]), mod_c[3], mod_c[4]),
                           ffn_w_up[i], ffn_conv_w[i], ffn_conv_b[i], ffn_w_down[i])
            ctx = ctx + mod_c[5] * rmsnorm(f_c, nw[3])
    return x
```

```python
import functools
import math

import jax
import jax.numpy as jnp
from jax import lax
from jax.experimental import pallas as pl
from jax.experimental.pallas import tpu as pltpu

F32 = jnp.float32
BF16 = jnp.bfloat16

NORM_EPS = 1e-6
ROPE_THETA = 10000.0
GRID_W = 64
ATTN_HEAD_DIM = 128
ATTN_GROUP = 4
ATTN_BLOCK = 128
MLSTM_HEADS = 4
MLSTM_QK = 256
MLSTM_V = 512
MLSTM_CHUNK = 128
GLA_HEADS = 8
GLA_QK = 256
GLA_V = 512
GLA_CHUNK = 64
GLA_GATE_TEMP = 16.0
GLA_GATE_RANK = 16
CONV_HALO = 16
LANE = 128
MIB = 1024 * 1024
VMEM_CAP = 60 * MIB


def _cparams(n_axes, vmem_bytes):
    return pltpu.CompilerParams(
        dimension_semantics=("arbitrary",) * n_axes,
        vmem_limit_bytes=int(min(VMEM_CAP, vmem_bytes)),
    )


def _sigmoid(x):
    return 1.0 / (1.0 + jnp.exp(-x))


def _log_sigmoid(x):
    return jnp.minimum(x, 0.0) - jnp.log(1.0 + jnp.exp(-jnp.abs(x)))


def _split3(x):
    hi = x.astype(BF16)
    r1 = x - hi.astype(F32)
    mid = r1.astype(BF16)
    lo = (r1 - mid.astype(F32)).astype(BF16)
    return hi, mid, lo


def _ada_body(s_ref, w_ref, b_ref, o_ref):
    s = s_ref[...]
    s = (s * _sigmoid(s)).astype(BF16)
    o_ref[...] = jnp.dot(s, w_ref[...].astype(BF16), preferred_element_type=F32) + b_ref[...]


def ada_mods(cvec, ada_w, ada_b, tn=512):
    depth, d, n = ada_w.shape
    rows = cvec.shape[0]
    return pl.pallas_call(
        _ada_body,
        out_shape=jax.ShapeDtypeStruct((depth, rows, n), F32),
        grid=(depth, n // tn),
        in_specs=[
            pl.BlockSpec((rows, d), lambda i, j: (0, 0)),
            pl.BlockSpec((None, d, tn), lambda i, j: (i, 0, j)),
            pl.BlockSpec((None, 1, tn), lambda i, j: (i, 0, j)),
        ],
        out_specs=pl.BlockSpec((None, rows, tn), lambda i, j: (i, 0, j)),
        compiler_params=_cparams(2, 2 * d * tn * 4 + d * tn * 2 + 8 * MIB),
        name="ada_mods",
    )(cvec, ada_w, ada_b.reshape(depth, 1, n))


def _rms(x, w):
    return x * lax.rsqrt(jnp.mean(x * x, axis=-1, keepdims=True) + NORM_EPS) * w


def _row_body(*refs, has_y, has_h):
    it = iter(refs)
    x = next(it)[...]
    if has_y:
        y = next(it)[...].astype(F32)
        gate = next(it)[...]
        nwy = next(it)[...]
    if has_h:
        nwh = next(it)[...]
        shift = next(it)[...]
        scale = next(it)[...]
    if has_y:
        x = x + gate * _rms(y, nwy)
        next(it)[...] = x
    if has_h:
        next(it)[...] = (_rms(x, nwh) * (1.0 + scale) + shift).astype(BF16)


def row_update(x, group_of_tile, n_rows, *, x_row_off=0, y=None, gate=None, nwy=None,
               nwh=None, shift=None, scale=None, tr=256):
    d = x.shape[1]
    has_y, has_h = y is not None, nwh is not None
    off = x_row_off // tr
    row = pl.BlockSpec((tr, d), lambda t: (t, 0))
    vec = pl.BlockSpec((1, d), lambda t: (0, 0))
    tab = pl.BlockSpec((None, 1, d), lambda t: (group_of_tile(t), 0, 0))
    args, specs = [x], [pl.BlockSpec((tr, d), lambda t: (t + off, 0))]
    if has_y:
        args += [y, gate, nwy.reshape(1, d)]
        specs += [row, tab, vec]
    if has_h:
        args += [nwh.reshape(1, d), shift, scale]
        specs += [vec, tab, tab]
    out_shape, out_specs = [], []
    if has_y:
        out_shape.append(jax.ShapeDtypeStruct((n_rows, d), F32))
        out_specs.append(row)
    if has_h:
        out_shape.append(jax.ShapeDtypeStruct((n_rows, d), BF16))
        out_specs.append(row)
    outs = pl.pallas_call(
        functools.partial(_row_body, has_y=has_y, has_h=has_h),
        out_shape=out_shape,
        grid=(n_rows // tr,),
        in_specs=specs,
        out_specs=out_specs,
        compiler_params=_cparams(1, 2 * tr * d * 18 + 8 * MIB),
        name="row_update",
    )(*args)
    return outs


def _mm_body(*refs, n_valid, has_acc):
    if has_acc:
        a_ref, w_ref, acc_ref, o_ref, wbf_ref = refs
    else:
        a_ref, w_ref, o_ref, wbf_ref = refs

    @pl.when(pl.program_id(1) == 0)
    def _():
        w = w_ref[...]
        if n_valid is not None:
            col = lax.broadcasted_iota(jnp.int32, w.shape, 1)
            w = jnp.where(col < n_valid, w, 0.0)
        wbf_ref[...] = w.astype(BF16)

    r = jnp.dot(a_ref[...], wbf_ref[...], preferred_element_type=F32)
    if has_acc:
        r = r + acc_ref[...]
    o_ref[...] = r.astype(o_ref.dtype)


def matmul(a, w, *, layer, m_blocks, n_blocks, tm, tn, tk=None, k_block=0, a_row_off=0, n_off=0,
           n_valid=None, acc=None, out_dtype=F32):
    k_total = a.shape[1]
    tk = k_total if tk is None else tk
    roff = a_row_off // tm
    in_specs = [
        pl.BlockSpec((tm, tk), lambda n, m: (m + roff, k_block)),
        pl.BlockSpec((None, tk, tn), lambda n, m: (layer, k_block, n + n_off)),
    ]
    args = [a, w]
    out_spec = pl.BlockSpec((tm, tn), lambda n, m: (m, n))
    aliases = {}
    if acc is not None:
        in_specs.append(out_spec)
        args.append(acc)
        aliases = {2: 0}
    osz = jnp.dtype(out_dtype).itemsize
    vmem = (2 * tm * tk * 2 + 2 * tk * tn * 4 + tk * tn * 2 + 2 * tm * tn * osz
            + (2 * tm * tn * 4 if acc is not None else 0) + 2 * tm * tn * 4 + 4 * MIB)
    return pl.pallas_call(
        functools.partial(_mm_body, n_valid=n_valid, has_acc=acc is not None),
        out_shape=jax.ShapeDtypeStruct((m_blocks * tm, n_blocks * tn), out_dtype),
        grid=(n_blocks, m_blocks),
        in_specs=in_specs,
        out_specs=out_spec,
        scratch_shapes=[pltpu.VMEM((tk, tn), BF16)],
        input_output_aliases=aliases,
        compiler_params=_cparams(2, vmem),
        name="matmul",
    )(*args)


def _ffn_up_body(a_ref, wg_ref, wv_ref, cwg_ref, cwv_ref, cbg_ref, cbv_ref, o_ref, wg_bf, wv_bf,
                 *, tm, n_ctx_rows, ctx_len, seq_len):
    m = pl.program_id(1)

    @pl.when(m == 0)
    def _():
        wg_bf[...] = wg_ref[...].astype(BF16)
        wv_bf[...] = wv_ref[...].astype(BF16)

    r = m * tm + lax.broadcasted_iota(jnp.int32, (tm, 1), 0)
    is_ctx = r < n_ctx_rows
    pos = jnp.where(is_ctx, r & (ctx_len - 1), (r - n_ctx_rows) & (seq_len - 1))
    prev_ok = pos != 0
    next_ok = pos != jnp.where(is_ctx, ctx_len - 1, seq_len - 1)

    a = a_ref[...]
    rows = tm + 2 * CONV_HALO

    def conv(w_bf, cw_ref, cb_ref):
        u = jnp.dot(a, w_bf[...], preferred_element_type=F32)
        u_prev = pltpu.roll(u, 1, 0)[CONV_HALO:CONV_HALO + tm]
        u_next = pltpu.roll(u, rows - 1, 0)[CONV_HALO:CONV_HALO + tm]
        u_mid = u[CONV_HALO:CONV_HALO + tm]
        cw = cw_ref[...]
        return (cb_ref[...] + jnp.where(prev_ok, u_prev, 0.0) * cw[0:1] + u_mid * cw[1:2]
                + jnp.where(next_ok, u_next, 0.0) * cw[2:3])

    gate = conv(wg_bf, cwg_ref, cbg_ref)
    val = conv(wv_bf, cwv_ref, cbv_ref)
    o_ref[...] = (gate * _sigmoid(gate) * val).astype(o_ref.dtype)


def ffn_up(a_ext, w_up, conv_w, conv_b, *, layer, n_ctx_rows, ctx_len, seq_len, tn=256):
    tiles, rows, k = a_ext.shape
    tm = rows - 2 * CONV_HALO
    d_ff = w_up.shape[2] // 2
    nb = d_ff // tn
    depth = w_up.shape[0]
    cb = conv_b.reshape(depth, 1, 2 * d_ff)
    vmem = 2 * rows * k * 2 + 2 * 2 * k * tn * 4 + 2 * k * tn * 2 + 2 * tm * tn * 2 + 10 * rows * tn * 4 + 4 * MIB
    return pl.pallas_call(
        functools.partial(_ffn_up_body, tm=tm, n_ctx_rows=n_ctx_rows, ctx_len=ctx_len, seq_len=seq_len),
        out_shape=jax.ShapeDtypeStruct((tiles * tm, d_ff), BF16),
        grid=(nb, tiles),
        in_specs=[
            pl.BlockSpec((None, rows, k), lambda n, m: (m, 0, 0)),
            pl.BlockSpec((None, k, tn), lambda n, m: (layer, 0, n)),
            pl.BlockSpec((None, k, tn), lambda n, m: (layer, 0, n + nb)),
            pl.BlockSpec((None, 3, tn), lambda n, m: (layer, 0, n)),
            pl.BlockSpec((None, 3, tn), lambda n, m: (layer, 0, n + nb)),
            pl.BlockSpec((None, 1, tn), lambda n, m: (layer, 0, n)),
            pl.BlockSpec((None, 1, tn), lambda n, m: (layer, 0, n + nb)),
        ],
        out_specs=pl.BlockSpec((tm, tn), lambda n, m: (m, n)),
        scratch_shapes=[pltpu.VMEM((k, tn), BF16), pltpu.VMEM((k, tn), BF16)],
        compiler_params=_cparams(2, vmem),
        name="ffn_up",
    )(a_ext, w_up, w_up, conv_w, conv_w, cb, cb)


def with_halo(h, tm):
    m = h.shape[0]
    hp = jnp.pad(h, ((CONV_HALO, CONV_HALO), (0, 0)))
    return jnp.stack([hp[i * tm:i * tm + tm + 2 * CONV_HALO] for i in range(m // tm)])


def _rope(x, cos, sin_signed):
    lane = lax.broadcasted_iota(jnp.int32, x.shape, 1)
    q = ATTN_HEAD_DIM // 4
    partner = jnp.where((lane & (2 * q - 1)) < q, pltpu.roll(x, ATTN_HEAD_DIM - q, 1), pltpu.roll(x, q, 1))
    return x * cos + partner * sin_signed


def _sink_attention(q4, kcat, vcat, sink_col, mask):
    s = lax.dot_general(q4, kcat, (((1,), (1,)), ((), ())), preferred_element_type=F32)
    s = s * (ATTN_HEAD_DIM ** -0.5)
    if mask is not None:
        s = jnp.where(mask, s, -jnp.inf)
    m = jnp.maximum(jnp.max(s, axis=-1, keepdims=True), sink_col)
    e = jnp.exp(s - m)
    p = e / (jnp.sum(e, axis=-1, keepdims=True) + jnp.exp(sink_col - m))
    return jnp.dot(p.astype(BF16), vcat, preferred_element_type=F32)


def _sink_column(sink_ref, j, rows):
    return jnp.concatenate(
        [jnp.full((rows, 1), sink_ref[j * ATTN_GROUP + g], F32) for g in range(ATTN_GROUP)], axis=0)


def _win_attn_body(sink_ref, q_ref, kp_ref, kc_ref, kn_ref, vp_ref, vc_ref, vn_ref, kx_ref, vx_ref,
                   cos_ref, sin_ref, o_ref, *, nb):
    j, n = pl.program_id(1), pl.program_id(2)
    blk, dh = ATTN_BLOCK, ATTN_HEAD_DIM

    def table(ref, i):
        return ref[pl.ds(pl.multiple_of(i * blk, blk), blk), :]

    n_prev, n_next = jnp.maximum(n - 1, 0), jnp.minimum(n + 1, nb - 1)
    cos_q, sin_q = table(cos_ref, n), table(sin_ref, n)
    q = q_ref[...].astype(F32)
    q4 = jnp.concatenate(
        [_rope(q[:, g * dh:(g + 1) * dh], cos_q, sin_q) for g in range(ATTN_GROUP)], axis=0).astype(BF16)
    kcat = jnp.concatenate([
        _rope(kp_ref[...].astype(F32), table(cos_ref, n_prev), table(sin_ref, n_prev)).astype(BF16),
        _rope(kc_ref[...].astype(F32), cos_q, sin_q).astype(BF16),
        _rope(kn_ref[...].astype(F32), table(cos_ref, n_next), table(sin_ref, n_next)).astype(BF16),
        kx_ref[...]], axis=0)
    vcat = jnp.concatenate([vp_ref[...], vc_ref[...], vn_ref[...], vx_ref[...]], axis=0)
    n_keys = kcat.shape[0]
    qi = lax.broadcasted_iota(jnp.int32, (ATTN_GROUP * blk, n_keys), 0) & (blk - 1)
    kj = lax.broadcasted_iota(jnp.int32, (ATTN_GROUP * blk, n_keys), 1)
    lo = jnp.where(n > 0, 0, blk)
    hi = jnp.where(n < nb - 1, 3 * blk, 2 * blk)
    band = (kj >= qi) & (kj <= qi + 2 * blk) & (kj >= lo) & (kj < hi)
    mask = band | (kj >= 3 * blk)
    o = _sink_attention(q4, kcat, vcat, _sink_column(sink_ref, j, blk), mask)
    for g in range(ATTN_GROUP):
        o_ref[:, g * dh:(g + 1) * dh] = o[g * blk:(g + 1) * blk].astype(o_ref.dtype)


def window_attention(p, sink, cos, sin_signed, *, batch, seq, ctx_len, q_col, k_col, v_col):
    blk, dh = ATTN_BLOCK, ATTN_HEAD_DIM
    nb = seq // blk
    n_kv = (k_col - q_col) // (ATTN_GROUP * dh)
    lat0 = batch * ctx_len // blk
    qw = ATTN_GROUP * dh
    kc0, vc0 = k_col // dh, v_col // dh

    def kv_spec(col0, shift):
        return pl.BlockSpec(
            (blk, dh), lambda b, j, n: (lat0 + b * nb + jnp.clip(n + shift, 0, nb - 1), col0 + j))

    in_specs = [
        pl.BlockSpec(memory_space=pltpu.SMEM),
        pl.BlockSpec((blk, qw), lambda b, j, n: (lat0 + b * nb + n, q_col // qw + j)),
        kv_spec(kc0, -1), kv_spec(kc0, 0), kv_spec(kc0, 1),
        kv_spec(vc0, -1), kv_spec(vc0, 0), kv_spec(vc0, 1),
        pl.BlockSpec((ctx_len, dh), lambda b, j, n: (b, kc0 + j)),
        pl.BlockSpec((ctx_len, dh), lambda b, j, n: (b, vc0 + j)),
        pl.BlockSpec((seq, dh), lambda b, j, n: (0, 0)),
        pl.BlockSpec((seq, dh), lambda b, j, n: (0, 0)),
    ]
    return pl.pallas_call(
        functools.partial(_win_attn_body, nb=nb),
        out_shape=jax.ShapeDtypeStruct((batch * seq, n_kv * qw), BF16),
        grid=(batch, n_kv, nb),
        in_specs=in_specs,
        out_specs=pl.BlockSpec((blk, qw), lambda b, j, n: (b * nb + n, j)),
        compiler_params=_cparams(3, 32 * MIB),
        name="window_attention",
    )(sink, p, p, p, p, p, p, p, p, p, cos, sin_signed)


def _ctx_attn_body(sink_ref, q_ref, k_ref, v_ref, o_ref):
    j = pl.program_id(1)
    dh = ATTN_HEAD_DIM
    rows = q_ref.shape[0]
    q = q_ref[...]
    q4 = jnp.concatenate([q[:, g * dh:(g + 1) * dh] for g in range(ATTN_GROUP)], axis=0)
    o = _sink_attention(q4, k_ref[...], v_ref[...], _sink_column(sink_ref, j, rows), None)
    for g in range(ATTN_GROUP):
        o_ref[:, g * dh:(g + 1) * dh] = o[g * rows:(g + 1) * rows].astype(o_ref.dtype)


def context_attention(p, sink, *, batch, ctx_len, q_col, k_col, v_col):
    dh = ATTN_HEAD_DIM
    qw = ATTN_GROUP * dh
    n_kv = (k_col - q_col) // qw
    kc0, vc0 = k_col // dh, v_col // dh
    return pl.pallas_call(
        _ctx_attn_body,
        out_shape=jax.ShapeDtypeStruct((batch * ctx_len, n_kv * qw), BF16),
        grid=(batch, n_kv),
        in_specs=[
            pl.BlockSpec(memory_space=pltpu.SMEM),
            pl.BlockSpec((ctx_len, qw), lambda b, j: (b, q_col // qw + j)),
            pl.BlockSpec((ctx_len, dh), lambda b, j: (b, kc0 + j)),
            pl.BlockSpec((ctx_len, dh), lambda b, j: (b, vc0 + j)),
        ],
        out_specs=pl.BlockSpec((ctx_len, qw), lambda b, j: (b, j)),
        compiler_params=_cparams(2, 32 * MIB),
        name="context_attention",
    )(sink, p, p, p)


def _direction_mask(size, fwd):
    t = lax.broadcasted_iota(jnp.int32, (size, size), 0)
    s = lax.broadcasted_iota(jnp.int32, (size, size), 1)
    diff = t - s
    return jnp.where(fwd, diff, -diff) >= 0


def _segment_chunk(c, d, n_ctx_chunks, n_lat_chunks):
    is_ctx = c < n_ctx_chunks
    fwd_i = jnp.where(is_ctx, c, c - n_ctx_chunks)
    last = jnp.where(is_ctx, n_ctx_chunks - 1, n_lat_chunks - 1)
    return is_ctx, jnp.where(d == 0, fwd_i, last - fwd_i)


def _mlstm_body(q_ref, k_ref, v0_ref, v1_ref, g_ref, gt_ref, gb_row_ref, gb_col_ref, o_ref, cn_ref, m_ref):
    d, c = pl.program_id(1), pl.program_id(2)
    ch, dk, dv, nh = MLSTM_CHUNK, MLSTM_QK, MLSTM_V, MLSTM_HEADS

    @pl.when(c == 0)
    def _():
        cn_ref[...] = jnp.zeros_like(cn_ref)
        m_ref[...] = jnp.zeros_like(m_ref)

    fwd = d == 0
    valid = _direction_mask(ch, fwd)
    tri = jnp.where(valid, 1.0, 0.0).astype(BF16)
    last_t = jnp.where(fwd, ch - 1, 0)
    t_col = lax.broadcasted_iota(jnp.int32, (ch, 1), 0)
    is_last = t_col == last_t

    g_col = g_ref[...] + gb_row_ref[...]
    g_row = gt_ref[...] + gb_col_ref[...]
    lf_col, lf_row = _log_sigmoid(g_col), _log_sigmoid(g_row)
    bc_col_all = sum(jnp.dot(tri, part, preferred_element_type=F32) for part in _split3(lf_col))
    bc_row_all = sum(lax.dot_general(part, tri, (((1,), (1,)), ((), ())), preferred_element_type=F32)
                     for part in _split3(lf_row))
    ones_col = jnp.where(lax.broadcasted_iota(jnp.int32, (ch, LANE), 1) == 0, 1.0, 0.0)

    for h in range(nh):
        ci_f, ci_b = h, 2 * nh + h
        cf_f, cf_b = nh + h, 3 * nh + h
        ig_col = jnp.where(fwd, g_col[:, ci_f:ci_f + 1], g_col[:, ci_b:ci_b + 1])
        ig_row = jnp.where(fwd, g_row[ci_f:ci_f + 1, :], g_row[ci_b:ci_b + 1, :])
        bc_col = jnp.where(fwd, bc_col_all[:, cf_f:cf_f + 1], bc_col_all[:, cf_b:cf_b + 1])
        bc_row = jnp.where(fwd, bc_row_all[cf_f:cf_f + 1, :], bc_row_all[cf_b:cf_b + 1, :])
        m_st = m_ref[h:h + 1, 0:1]
        q = q_ref[:, h * dk:(h + 1) * dk]
        k = k_ref[:, h * dk:(h + 1) * dk] * (dk ** -0.5)
        v_ref = v0_ref if h < nh // 2 else v1_ref
        hv = h % (nh // 2)
        v = v_ref[:, hv * dv:(hv + 1) * dv]

        log_d = jnp.where(valid, bc_col - bc_row + ig_row, -jnp.inf)
        inter = bc_col + m_st
        m_t = jnp.maximum(inter, jnp.max(log_d, axis=-1, keepdims=True))
        s = lax.dot_general(q, k, (((1,), (1,)), ((), ())), preferred_element_type=F32) * jnp.exp(log_d - m_t)
        w_inter = jnp.exp(inter - m_t)
        state = cn_ref[h]
        q_state = jnp.dot(q, state.astype(BF16), preferred_element_type=F32)
        num = jnp.dot(s.astype(BF16), v, preferred_element_type=F32) + w_inter * q_state[:, :dv]
        den = jnp.sum(s, axis=-1, keepdims=True) + w_inter * q_state[:, dv:dv + 1]
        o_ref[:, h * dv:(h + 1) * dv] = num * (1.0 / jnp.maximum(jnp.abs(den), jnp.exp(-m_t)))

        b_end = jnp.sum(jnp.where(is_last, bc_col, 0.0), axis=0, keepdims=True)
        m_new = jnp.sum(jnp.where(is_last, m_t, 0.0), axis=0, keepdims=True)
        wk = jnp.exp(b_end - bc_col + ig_col - m_new)
        decay = jnp.exp(b_end + m_st - m_new)
        v_ext = jnp.concatenate([v.astype(F32), ones_col], axis=1) * wk
        cn_ref[h] = decay * state + lax.dot_general(
            k, v_ext.astype(BF16), (((0,), (0,)), ((), ())), preferred_element_type=F32)
        m_ref[h:h + 1, :] = jnp.broadcast_to(m_new, (1, LANE))


def mlstm(p, gates, gate_b, *, batch, seq, ctx_len, q_col, k_col, v_col):
    ch, nh, dk, dv = MLSTM_CHUNK, MLSTM_HEADS, MLSTM_QK, MLSTM_V
    ncc, nlc = ctx_len // ch, seq // ch
    rows = p.shape[0]

    def rowblk(b, d, c):
        is_ctx, i = _segment_chunk(c, d, ncc, nlc)
        return jnp.where(is_ctx, b * ncc + i, batch * ncc + b * nlc + i)

    gb = jnp.zeros((LANE,), F32).at[:4 * nh].set(gate_b.astype(F32))
    qw, vw = nh * dk, nh * dv // 2
    return pl.pallas_call(
        _mlstm_body,
        out_shape=jax.ShapeDtypeStruct((2, rows, nh * dv), F32),
        grid=(batch, 2, ncc + nlc),
        in_specs=[
            pl.BlockSpec((ch, qw), lambda b, d, c: (rowblk(b, d, c), q_col // qw)),
            pl.BlockSpec((ch, qw), lambda b, d, c: (rowblk(b, d, c), k_col // qw)),
            pl.BlockSpec((ch, vw), lambda b, d, c: (rowblk(b, d, c), v_col // vw)),
            pl.BlockSpec((ch, vw), lambda b, d, c: (rowblk(b, d, c), v_col // vw + 1)),
            pl.BlockSpec((ch, LANE), lambda b, d, c: (rowblk(b, d, c), 0)),
            pl.BlockSpec((LANE, ch), lambda b, d, c: (0, rowblk(b, d, c))),
            pl.BlockSpec((1, LANE), lambda b, d, c: (0, 0)),
            pl.BlockSpec((LANE, 1), lambda b, d, c: (0, 0)),
        ],
        out_specs=pl.BlockSpec((None, ch, nh * dv), lambda b, d, c: (d, rowblk(b, d, c), 0)),
        scratch_shapes=[pltpu.VMEM((nh, dk, dv + LANE), F32), pltpu.VMEM((8, LANE), F32)],
        compiler_params=_cparams(3, 32 * MIB),
        name="mlstm",
    )(p, p, p, p, gates, gates.T, gb.reshape(1, LANE), gb.reshape(LANE, 1))


def _gla_body(q_ref, k_ref, v_ref, lr_ref, w2_ref, gb_ref, o_ref, st_ref):
    d, c = pl.program_id(1), pl.program_id(2)
    ch, dk, dv, nh = GLA_CHUNK, GLA_QK, GLA_V, GLA_HEADS

    @pl.when(c == 0)
    def _():
        st_ref[...] = jnp.zeros_like(st_ref)

    fwd = d == 0
    valid = _direction_mask(ch, fwd)
    tri = jnp.where(valid, 1.0, 0.0).astype(BF16)

    pre = jnp.dot(lr_ref[...].astype(BF16), w2_ref[...].astype(BF16), preferred_element_type=F32)
    lg = _log_sigmoid(pre + gb_ref[...]) / GLA_GATE_TEMP
    gcum = sum(jnp.dot(tri, part, preferred_element_type=F32) for part in _split3(lg))
    g_end = jnp.where(fwd, gcum[ch - 1:ch, :], gcum[0:1, :])
    e_pos, e_neg = jnp.exp(gcum), jnp.exp(-gcum)
    e_end = jnp.exp(g_end - gcum)
    decay = jnp.exp(g_end)

    for h in range(nh):
        sl = slice(h * dk, (h + 1) * dk)
        q = q_ref[:, sl].astype(F32) * (dk ** -0.5)
        k = k_ref[:, sl].astype(F32)
        v = v_ref[:, h * dv:(h + 1) * dv]
        q_dec = (q * e_pos[:, sl]).astype(BF16)
        k_inv = (k * e_neg[:, sl]).astype(BF16)
        k_end = (k * e_end[:, sl]).astype(BF16)
        att = lax.dot_general(q_dec, k_inv, (((1,), (1,)), ((), ())), preferred_element_type=F32)
        att = jnp.where(valid, att, 0.0).astype(BF16)
        st = st_ref[h]
        o = (jnp.dot(att, v, preferred_element_type=F32)
             + lax.dot_general(q_dec, st.astype(BF16), (((1,), (1,)), ((), ())), preferred_element_type=F32))
        o_ref[:, h * dv:(h + 1) * dv] = o
        st_ref[h] = decay[:, sl] * st + lax.dot_general(
            v, k_end, (((0,), (0,)), ((), ())), preferred_element_type=F32)


def gla(p, lowrank, gate_w2, gate_b, *, batch, seq, ctx_len, q_col, k_col, v_col):
    ch, nh, dk, dv = GLA_CHUNK, GLA_HEADS, GLA_QK, GLA_V
    ncc, nlc = ctx_len // ch, seq // ch
    rank = GLA_GATE_RANK

    def rowblk(b, d, c):
        is_ctx, i = _segment_chunk(c, d, ncc, nlc)
        return jnp.where(is_ctx, b * ncc + i, batch * ncc + b * nlc + i)

    def outblk(b, d, c):
        _, i = _segment_chunk(jnp.maximum(c, ncc), d, ncc, nlc)
        return b * nlc + i

    w2 = jnp.zeros((2, LANE, nh * dk), F32)
    w2 = w2.at[0, :rank].set(gate_w2[0]).at[1, rank:2 * rank].set(gate_w2[1])
    return pl.pallas_call(
        _gla_body,
        out_shape=jax.ShapeDtypeStruct((2, batch * seq, nh * dv), F32),
        grid=(batch, 2, ncc + nlc),
        in_specs=[
            pl.BlockSpec((ch, nh * dk), lambda b, d, c: (rowblk(b, d, c), q_col // (nh * dk))),
            pl.BlockSpec((ch, nh * dk), lambda b, d, c: (rowblk(b, d, c), k_col // (nh * dk))),
            pl.BlockSpec((ch, nh * dv), lambda b, d, c: (rowblk(b, d, c), v_col // (nh * dv))),
            pl.BlockSpec((ch, LANE), lambda b, d, c: (rowblk(b, d, c), 0)),
            pl.BlockSpec((None, LANE, nh * dk), lambda b, d, c: (d, 0, 0)),
            pl.BlockSpec((None, 1, nh * dk), lambda b, d, c: (d, 0, 0)),
        ],
        out_specs=pl.BlockSpec((None, ch, nh * dv), lambda b, d, c: (d, outblk(b, d, c), 0)),
        scratch_shapes=[pltpu.VMEM((nh, dv, dk), F32)],
        compiler_params=_cparams(3, 40 * MIB),
        name="gla",
    )(p, p, p, lowrank, w2, gate_b.reshape(2, 1, nh * dk))


def _head_out_body(hf_ref, hb_ref, g_ref, gain_ref, o_ref, *, silu_gate):
    h = hf_ref[...] + hb_ref[...]
    hn = h * lax.rsqrt(jnp.mean(h * h, axis=-1, keepdims=True) + NORM_EPS)
    g = g_ref[...].astype(F32)
    act = g * _sigmoid(g) if silu_gate else _sigmoid(g)
    o_ref[...] = (hn * gain_ref[...] * act).astype(o_ref.dtype)


def head_output(h2, p, gain, *, n_rows, p_row_off, gate_col, n_heads, dv, silu_gate, tr=512):
    roff = p_row_off // tr
    return pl.pallas_call(
        functools.partial(_head_out_body, silu_gate=silu_gate),
        out_shape=jax.ShapeDtypeStruct((n_rows, n_heads * dv), BF16),
        grid=(n_rows // tr, n_heads),
        in_specs=[
            pl.BlockSpec((None, tr, dv), lambda t, h: (0, t, h)),
            pl.BlockSpec((None, tr, dv), lambda t, h: (1, t, h)),
            pl.BlockSpec((tr, dv), lambda t, h: (t + roff, gate_col // dv + h)),
            pl.BlockSpec((1, dv), lambda t, h: (0, h)),
        ],
        out_specs=pl.BlockSpec((tr, dv), lambda t, h: (t, h)),
        compiler_params=_cparams(2, 32 * MIB),
        name="head_output",
    )(h2, h2, p, gain.reshape(1, n_heads * dv).astype(F32))


def _rope_tables(seq):
    rows = seq // GRID_W
    row = jnp.repeat(jnp.arange(rows), GRID_W).astype(F32)
    col = jnp.tile(jnp.arange(GRID_W), rows).astype(F32)
    n_freq = ATTN_HEAD_DIM // 4
    inv = ROPE_THETA ** (-jnp.arange(n_freq, dtype=F32) / n_freq)
    ang_r, ang_c = row[:, None] * inv, col[:, None] * inv
    cos = jnp.concatenate([jnp.cos(ang_r), jnp.cos(ang_r), jnp.cos(ang_c), jnp.cos(ang_c)], axis=-1)
    sin = jnp.concatenate([-jnp.sin(ang_r), jnp.sin(ang_r), -jnp.sin(ang_c), jnp.sin(ang_c)], axis=-1)
    return cos, sin


def _conv_ffn(h, w_up, conv_w, conv_b, w_down, *, layer, n_ctx_rows, ctx_len, seq_len, tm):
    g = ffn_up(with_halo(h, tm), w_up, conv_w, conv_b, layer=layer, n_ctx_rows=n_ctx_rows,
               ctx_len=ctx_len, seq_len=seq_len)
    m, d_ff = g.shape
    d = w_down.shape[2]
    half = d_ff // 2
    kw = dict(layer=layer, m_blocks=m // 512, n_blocks=d // 512, tm=512, tn=512, tk=half)
    f = matmul(g, w_down, k_block=0, **kw)
    return matmul(g, w_down, k_block=1, acc=f, **kw)


def kernel(x, c, ctx, c_ctx, ada_w, ada_b, norm_w, ab_w_in, ab_gate_b, ab_sink, ab_head_norm, ab_w_out,
           gla_w_in, gla_gate_w2, gla_gate_b, gla_head_norm, gla_w_out,
           ffn_w_up, ffn_conv_w, ffn_conv_b, ffn_w_down):
    batch, seq, d = x.shape
    ctx_len = ctx.shape[1]
    n_ctx, n_lat = batch * ctx_len, batch * seq
    n_tok = n_ctx + n_lat
    depth = ada_w.shape[0]
    assert depth == 2 and ab_w_in.shape[0] == 1 and gla_w_in.shape[0] == 1
    tr = 256
    tm = math.gcd(math.gcd(n_ctx, seq), 1024)
    assert tm % 512 == 0 and ctx_len % tr == 0 and batch + 1 <= 8
    assert ctx_len & (ctx_len - 1) == 0 and seq & (seq - 1) == 0

    cvec = jnp.zeros((8, d), F32).at[:batch].set(c).at[batch].set(c_ctx)
    mods = ada_mods(cvec, ada_w, ada_b).reshape(depth, 8, 6, 1, d)

    def mod(i, j):
        return mods[i, :, j]

    ctx_tiles = n_ctx // tr
    lat_tiles_per_seq = seq // tr

    def group_all(t):
        return jnp.where(t < ctx_tiles, batch, (t - ctx_tiles) // lat_tiles_per_seq)

    def group_lat(t):
        return t // lat_tiles_per_seq

    tok = jnp.concatenate([ctx.reshape(n_ctx, d), x.reshape(n_lat, d)], axis=0)

    nw = norm_w[0]
    (h,) = row_update(tok, group_all, n_tok, nwh=nw[0], shift=mod(0, 0), scale=mod(0, 1), tr=tr)
    aq = ATTN_GROUP * ATTN_HEAD_DIM * (ab_sink.shape[1] // ATTN_GROUP)
    akv = ATTN_HEAD_DIM * (ab_sink.shape[1] // ATTN_GROUP)
    mqk = MLSTM_HEADS * MLSTM_QK
    mv = MLSTM_HEADS * MLSTM_V
    cols = [0, aq, aq + akv, aq + 2 * akv, aq + 2 * akv + mqk, aq + 2 * akv + 2 * mqk,
            aq + 2 * akv + 2 * mqk + mv, aq + 2 * akv + 2 * mqk + 2 * mv]
    wide = cols[-1]
    p = matmul(h, ab_w_in, layer=0, m_blocks=n_tok // tm, n_blocks=wide // 512, tm=tm, tn=512,
               out_dtype=BF16)
    gates = matmul(h, ab_w_in, layer=0, m_blocks=n_tok // tm, n_blocks=1, tm=tm, tn=LANE,
                   n_off=wide // LANE, n_valid=ab_w_in.shape[2] - wide)
    cos, sin = _rope_tables(seq)
    sink = ab_sink[0].astype(F32)
    a_lat = window_attention(p, sink, cos, sin, batch=batch, seq=seq, ctx_len=ctx_len,
                             q_col=cols[0], k_col=cols[1], v_col=cols[2])
    a_ctx = context_attention(p, sink, batch=batch, ctx_len=ctx_len, q_col=cols[0], k_col=cols[1], v_col=cols[2])
    hm = mlstm(p, gates, ab_gate_b[0], batch=batch, seq=seq, ctx_len=ctx_len,
               q_col=cols[3], k_col=cols[4], v_col=cols[5])
    m_all = head_output(hm, p, ab_head_norm[0], n_rows=n_tok, p_row_off=0, gate_col=cols[6],
                        n_heads=MLSTM_HEADS, dv=MLSTM_V, silu_gate=False)
    mixed = jnp.concatenate([jnp.concatenate([a_ctx, a_lat], axis=0), m_all], axis=1)
    y = matmul(mixed, ab_w_out, layer=0, m_blocks=n_tok // tm, n_blocks=d // 512, tm=tm, tn=512)
    tok, h = row_update(tok, group_all, n_tok, y=y, gate=mod(0, 2), nwy=nw[1],
                        nwh=nw[2], shift=mod(0, 3), scale=mod(0, 4), tr=tr)
    f = _conv_ffn(h, ffn_w_up, ffn_conv_w, ffn_conv_b, ffn_w_down, layer=0, n_ctx_rows=n_ctx,
                  ctx_len=ctx_len, seq_len=seq, tm=tm)
    nw1 = norm_w[1]
    tok, h = row_update(tok, group_all, n_tok, y=f, gate=mod(0, 5), nwy=nw[3],
                        nwh=nw1[0], shift=mod(1, 0), scale=mod(1, 1), tr=tr)

    gq = GLA_HEADS * GLA_QK
    gv = GLA_HEADS * GLA_V
    wide = 2 * gq + 2 * gv
    p = matmul(h, gla_w_in, layer=0, m_blocks=n_tok // tm, n_blocks=wide // 512, tm=tm, tn=512,
               out_dtype=BF16)
    lowrank = matmul(h, gla_w_in, layer=0, m_blocks=n_tok // tm, n_blocks=1, tm=tm, tn=LANE,
                     n_off=wide // LANE, n_valid=gla_w_in.shape[2] - wide)
    o2 = gla(p, lowrank, gla_gate_w2[0], gla_gate_b[0], batch=batch, seq=seq, ctx_len=ctx_len,
             q_col=0, k_col=gq, v_col=2 * gq)
    o_all = head_output(o2, p, gla_head_norm[0], n_rows=n_lat, p_row_off=n_ctx, gate_col=2 * gq + gv,
                        n_heads=GLA_HEADS, dv=GLA_V, silu_gate=True)
    y = matmul(o_all, gla_w_out, layer=0, m_blocks=n_lat // tm, n_blocks=d // 512, tm=tm, tn=512)
    xl, h = row_update(tok, group_lat, n_lat, x_row_off=n_ctx, y=y, gate=mod(1, 2), nwy=nw1[1],
                       nwh=nw1[2], shift=mod(1, 3), scale=mod(1, 4), tr=tr)
    f = _conv_ffn(h, ffn_w_up, ffn_conv_w, ffn_conv_b, ffn_w_down, layer=1, n_ctx_rows=0,
                  ctx_len=ctx_len, seq_len=seq, tm=tm)
    (out,) = row_update(xl, group_lat, n_lat, y=f, gate=mod(1, 5), nwy=nw1[3], tr=tr)
    return out.reshape(batch, seq, d)
```

```python
import functools
import math

import jax
import jax.numpy as jnp
from jax import lax
from jax.experimental import pallas as pl
from jax.experimental.pallas import tpu as pltpu

F32 = jnp.float32
BF16 = jnp.bfloat16

NORM_EPS = 1e-6
ROPE_THETA = 10000.0
GRID_W = 64
ATTN_HEAD_DIM = 128
ATTN_GROUP = 4
ATTN_BLOCK = 128
MLSTM_HEADS = 4
MLSTM_QK = 256
MLSTM_V = 512
MLSTM_CHUNK = 128
GLA_HEADS = 8
GLA_QK = 256
GLA_V = 512
GLA_CHUNK = 64
GLA_GATE_TEMP = 16.0
GLA_GATE_RANK = 16
LANE = 128
MXU_DEPTH = 256
MIB = 1024 * 1024
VMEM_CAP = 60 * MIB


def _cparams(n_axes, vmem_bytes, flags=None):
    return pltpu.CompilerParams(
        dimension_semantics=("arbitrary",) * n_axes,
        vmem_limit_bytes=int(min(VMEM_CAP, vmem_bytes)),
        flags=flags,
    )


def _sigmoid(x):
    return 1.0 / (1.0 + jnp.exp(-x))


def _log_sigmoid(x):
    return jnp.minimum(x, 0.0) - jnp.log(1.0 + jnp.exp(-jnp.abs(x)))


def _split3(x):
    hi = x.astype(BF16)
    r1 = x - hi.astype(F32)
    mid = r1.astype(BF16)
    lo = (r1 - mid.astype(F32)).astype(BF16)
    return hi, mid, lo


def _ada_body(s_ref, w_ref, b_ref, o_ref):
    s = s_ref[...]
    s = (s * _sigmoid(s)).astype(BF16)
    o_ref[...] = jnp.dot(s, w_ref[...].astype(BF16), preferred_element_type=F32) + b_ref[...]


def ada_mods(cvec, ada_w, ada_b, tn=512):
    depth, d, n = ada_w.shape
    rows = cvec.shape[0]
    return pl.pallas_call(
        _ada_body,
        out_shape=jax.ShapeDtypeStruct((depth, rows, n), F32),
        grid=(depth, n // tn),
        in_specs=[
            pl.BlockSpec((rows, d), lambda i, j: (0, 0)),
            pl.BlockSpec((None, d, tn), lambda i, j: (i, 0, j)),
            pl.BlockSpec((None, 1, tn), lambda i, j: (i, 0, j)),
        ],
        out_specs=pl.BlockSpec((None, rows, tn), lambda i, j: (i, 0, j)),
        compiler_params=_cparams(2, 2 * d * tn * 4 + d * tn * 2 + 8 * MIB),
        name="ada_mods",
    )(cvec, ada_w, ada_b.reshape(depth, 1, n))


def _rms(x, w):
    return x * lax.rsqrt(jnp.mean(x * x, axis=-1, keepdims=True) + NORM_EPS) * w


def _row_body(*refs, has_y, has_h, split):
    it = iter(refs)
    x = next(it)[...]
    if split is not None:
        x = jnp.where(pl.program_id(0) < split, x, next(it)[...])
    if has_y:
        y = next(it)[...].astype(F32)
        gate = next(it)[...]
        nwy = next(it)[...]
    if has_h:
        nwh = next(it)[...]
        shift = next(it)[...]
        scale = next(it)[...]
    if has_y:
        x = x + gate * _rms(y, nwy)
        next(it)[...] = x
    if has_h:
        next(it)[...] = (_rms(x, nwh) * (1.0 + scale) + shift).astype(BF16)


def row_update(x, group_of_tile, n_rows, *, x_tail=None, x_row_off=0, y=None, gate=None, nwy=None,
               nwh=None, shift=None, scale=None, tr=256):
    d = x.shape[1]
    has_y, has_h = y is not None, nwh is not None
    off = x_row_off // tr
    row = pl.BlockSpec((tr, d), lambda t: (t, 0))
    vec = pl.BlockSpec((1, d), lambda t: (0, 0))
    tab = pl.BlockSpec((None, 1, d), lambda t: (group_of_tile(t), 0, 0))
    split = None
    if x_tail is None:
        args, specs = [x], [pl.BlockSpec((tr, d), lambda t: (t + off, 0))]
    else:
        split = x.shape[0] // tr
        args = [x, x_tail]
        specs = [pl.BlockSpec((tr, d), lambda t: (jnp.minimum(t, split - 1), 0)),
                 pl.BlockSpec((tr, d), lambda t: (jnp.maximum(t - split, 0), 0))]
    if has_y:
        args += [y, gate, nwy.reshape(1, d)]
        specs += [row, tab, vec]
    if has_h:
        args += [nwh.reshape(1, d), shift, scale]
        specs += [vec, tab, tab]
    out_shape, out_specs = [], []
    if has_y:
        out_shape.append(jax.ShapeDtypeStruct((n_rows, d), F32))
        out_specs.append(row)
    if has_h:
        out_shape.append(jax.ShapeDtypeStruct((n_rows, d), BF16))
        out_specs.append(row)
    return pl.pallas_call(
        functools.partial(_row_body, has_y=has_y, has_h=has_h, split=split),
        out_shape=out_shape,
        grid=(n_rows // tr,),
        in_specs=specs,
        out_specs=out_specs,
        compiler_params=_cparams(1, 2 * tr * d * 22 + 8 * MIB),
        name="row_update",
    )(*args)


def _mm_body(*refs, n_a, n_valid, has_acc):
    a_refs = refs[:n_a]
    w_ref = refs[n_a]
    acc_ref = refs[n_a + 1] if has_acc else None
    o_ref, wbf_ref = refs[-2], refs[-1]

    @pl.when(pl.program_id(1) == 0)
    def _():
        w = w_ref[...]
        if n_valid is not None:
            col = lax.broadcasted_iota(jnp.int32, w.shape, 1)
            w = jnp.where(col < n_valid, w, 0.0)
        wbf_ref[...] = w.astype(BF16)

    r, k0 = None, 0
    for a_ref in a_refs:
        k1 = k0 + a_ref.shape[1]
        part = jnp.dot(a_ref[...], wbf_ref[k0:k1, :], preferred_element_type=F32)
        r = part if r is None else r + part
        k0 = k1
    if has_acc:
        r = r + acc_ref[...]
    o_ref[...] = r.astype(o_ref.dtype)


def matmul(a, w, *, layer, m_blocks, n_blocks, tm, tn, tk=None, k_block=0, a_row_off=0, n_off=0,
           n_valid=None, acc=None, out_dtype=F32):
    a_list = list(a) if isinstance(a, (list, tuple)) else [a]
    k_total = sum(x.shape[1] for x in a_list)
    tk = k_total if tk is None else tk
    assert len(a_list) == 1 or tk == k_total
    roff = a_row_off // tm
    in_specs = [pl.BlockSpec((tm, tk if len(a_list) == 1 else x.shape[1]), lambda n, m: (m + roff, k_block))
                for x in a_list]
    in_specs.append(pl.BlockSpec((None, tk, tn), lambda n, m: (layer, k_block, n + n_off)))
    args = a_list + [w]
    out_spec = pl.BlockSpec((tm, tn), lambda n, m: (m, n))
    aliases = {}
    if acc is not None:
        in_specs.append(out_spec)
        aliases = {len(args): 0}
        args.append(acc)
    osz = jnp.dtype(out_dtype).itemsize
    vmem = (2 * tm * tk * 2 + 2 * tk * tn * 4 + tk * tn * 2 + 2 * tm * tn * osz
            + (2 * tm * tn * 4 if acc is not None else 0) + 2 * tm * tn * 4 + 4 * MIB)
    return pl.pallas_call(
        functools.partial(_mm_body, n_a=len(a_list), n_valid=n_valid, has_acc=acc is not None),
        out_shape=jax.ShapeDtypeStruct((m_blocks * tm, n_blocks * tn), out_dtype),
        grid=(n_blocks, m_blocks),
        in_specs=in_specs,
        out_specs=out_spec,
        scratch_shapes=[pltpu.VMEM((tk, tn), BF16)],
        input_output_aliases=aliases,
        compiler_params=_cparams(2, vmem),
        name="matmul",
    )(*args)


FFN_LAG = 2
FFN_RING = FFN_LAG + 1


def _ffn_up_body(a_ref, wg_ref, wv_ref, cwg_ref, cwv_ref, cbg_ref, cbv_ref, o_ref, wg_bf, wv_bf, ug_ref, uv_ref,
                 *, tm, tiles, total, n_ctx_rows, ctx_len, seq_len):
    s = pl.program_id(0)
    m = jnp.minimum(s, total - 1) % tiles
    m_out = jnp.maximum(s - FFN_LAG, 0) % tiles

    @pl.when(s == 0)
    def _():
        ug_ref[...] = jnp.zeros_like(ug_ref)
        uv_ref[...] = jnp.zeros_like(uv_ref)

    @pl.when(m == 0)
    def _():
        wg_bf[...] = wg_ref[...].astype(BF16)
        wv_bf[...] = wv_ref[...].astype(BF16)

    slot_new = s % FFN_RING
    slot_mid = (s + 1) % FFN_RING
    slot_next = (s + 2) % FFN_RING
    k_tiles = a_ref.shape[1] // MXU_DEPTH
    chunk = tm // k_tiles
    unit = min(chunk, ctx_len)
    row8 = lax.broadcasted_iota(jnp.int32, (8, o_ref.shape[1]), 0)

    def starts_sequence(g):
        return (g <= n_ctx_rows) | (((g - n_ctx_rows) & (seq_len - 1)) == 0)

    def conv(u_ref, cw_ref, cb_ref, r0):
        r1 = r0 + unit
        above = u_ref[slot_new, tm - 1:tm, :] if r0 == 0 else u_ref[slot_mid, r0 - 1:r0, :]
        below = u_ref[slot_next, 0:1, :] if r1 == tm else u_ref[slot_mid, r1:r1 + 1, :]
        if r0 % ctx_len == 0:
            above = jnp.where(starts_sequence(m_out * tm + r0), 0.0, above)
        if r1 % ctx_len == 0:
            below = jnp.where(starts_sequence(m_out * tm + r1), 0.0, below)
        u = u_ref[slot_mid, r0:r1, :]
        down, up = pltpu.roll(u, 1, 0), pltpu.roll(u, unit - 1, 0)
        u_prev = jnp.concatenate([jnp.where(row8 == 0, above, down[0:8]), down[8:]], axis=0)
        u_next = jnp.concatenate([up[:unit - 8], jnp.where(row8 == 7, below, up[unit - 8:])], axis=0)
        cw = cw_ref[...]
        return cb_ref[...] + u_prev * cw[0:1] + u * cw[1:2] + u_next * cw[2:3]

    sixteen = jnp.uint32(16)
    zeros = []
    for j in range(k_tiles):
        seen = None
        for r0 in range(j * chunk, (j + 1) * chunk, unit):
            gate = conv(ug_ref, cwg_ref, cbg_ref, r0)
            val = conv(uv_ref, cwv_ref, cbv_ref, r0)
            out = gate * _sigmoid(gate) * val
            o_ref[r0:r0 + unit, :] = out.astype(o_ref.dtype)
            bits = lax.bitcast_convert_type(out, jnp.uint32)
            for r in range(0, unit, 8):
                for c in range(0, bits.shape[1], LANE):
                    blk = bits[r:r + 8, c:c + LANE]
                    seen = blk if seen is None else seen | blk
        zeros.append(pltpu.bitcast(lax.shift_right_logical(lax.shift_right_logical(seen, sixteen), sixteen), BF16))

    def paced(w):
        parts = [w[0:MXU_DEPTH]]
        zr = zeros[0].shape[0]
        for k in range(1, k_tiles):
            blk = w[k * MXU_DEPTH:(k + 1) * MXU_DEPTH]
            head = jnp.concatenate([blk[0:zr, :LANE] + zeros[k - 1], blk[0:zr, LANE:]], axis=1)
            parts.append(jnp.concatenate([head, blk[zr:]], axis=0))
        return jnp.concatenate(parts, axis=0)

    a = a_ref[...]
    ug_ref[slot_new] = jnp.dot(a, paced(wg_bf[...]), preferred_element_type=F32)
    uv_ref[slot_new] = jnp.dot(a, paced(wv_bf[...]), preferred_element_type=F32)


def ffn_up(h, w_up, conv_w, conv_b, *, layer, tm, n_ctx_rows, ctx_len, seq_len, tn=256):
    m_rows, k = h.shape
    tiles = m_rows // tm
    d_ff = w_up.shape[2] // 2
    nb = d_ff // tn
    total = nb * tiles
    assert tm % ctx_len == 0 and seq_len % ctx_len == 0 and n_ctx_rows % ctx_len == 0 and tn % LANE == 0
    chunk = tm // (k // MXU_DEPTH)
    assert k % MXU_DEPTH == 0 and chunk % 8 == 0 and (chunk % ctx_len == 0 or ctx_len % chunk == 0)
    depth = w_up.shape[0]
    cb = conv_b.reshape(depth, 1, 2 * d_ff)

    def cur(s):
        sc = jnp.minimum(s, total - 1)
        return sc // tiles, sc % tiles

    def out(s):
        so = jnp.maximum(s - FFN_LAG, 0)
        return so // tiles, so % tiles

    vmem = (2 * tm * k * 2 + 2 * 2 * k * tn * 4 + 2 * k * tn * 2 + 2 * FFN_RING * tm * tn * 4
            + 2 * tm * tn * 2 + 12 * tm * tn * 4 + 4 * MIB)
    return pl.pallas_call(
        functools.partial(_ffn_up_body, tm=tm, tiles=tiles, total=total, n_ctx_rows=n_ctx_rows,
                          ctx_len=ctx_len, seq_len=seq_len),
        out_shape=jax.ShapeDtypeStruct((m_rows, d_ff), BF16),
        grid=(total + FFN_LAG,),
        in_specs=[
            pl.BlockSpec((tm, k), lambda s: (cur(s)[1], 0)),
            pl.BlockSpec((None, k, tn), lambda s: (layer, 0, cur(s)[0])),
            pl.BlockSpec((None, k, tn), lambda s: (layer, 0, cur(s)[0] + nb)),
            pl.BlockSpec((None, 3, tn), lambda s: (layer, 0, out(s)[0])),
            pl.BlockSpec((None, 3, tn), lambda s: (layer, 0, out(s)[0] + nb)),
            pl.BlockSpec((None, 1, tn), lambda s: (layer, 0, out(s)[0])),
            pl.BlockSpec((None, 1, tn), lambda s: (layer, 0, out(s)[0] + nb)),
        ],
        out_specs=pl.BlockSpec((tm, tn), lambda s: (out(s)[1], out(s)[0])),
        scratch_shapes=[pltpu.VMEM((k, tn), BF16), pltpu.VMEM((k, tn), BF16),
                        pltpu.VMEM((FFN_RING, tm, tn), F32), pltpu.VMEM((FFN_RING, tm, tn), F32)],
        compiler_params=_cparams(1, vmem),
        name="ffn_up",
    )(h, w_up, w_up, conv_w, conv_w, cb, cb)


def _rope(x, cos, sin_signed):
    lane = lax.broadcasted_iota(jnp.int32, x.shape, 1)
    q = ATTN_HEAD_DIM // 4
    partner = jnp.where((lane & (2 * q - 1)) < q, pltpu.roll(x, ATTN_HEAD_DIM - q, 1), pltpu.roll(x, q, 1))
    return x * cos + partner * sin_signed


def _sink_attention(q4, kcat, vcat, sink_col, mask):
    s = lax.dot_general(q4, kcat, (((1,), (1,)), ((), ())), preferred_element_type=F32)
    s = s * (ATTN_HEAD_DIM ** -0.5)
    if mask is not None:
        s = jnp.where(mask, s, -jnp.inf)
    m = jnp.maximum(jnp.max(s, axis=-1, keepdims=True), sink_col)
    e = jnp.exp(s - m)
    p = e / (jnp.sum(e, axis=-1, keepdims=True) + jnp.exp(sink_col - m))
    return jnp.dot(p.astype(BF16), vcat, preferred_element_type=F32)


def _sink_column(sink_ref, j, rows):
    return jnp.concatenate(
        [jnp.full((rows, 1), sink_ref[j * ATTN_GROUP + g], F32) for g in range(ATTN_GROUP)], axis=0)


def _win_attn_body(sink_ref, q_ref, kp_ref, kc_ref, kn_ref, vp_ref, vc_ref, vn_ref, kx_ref, vx_ref,
                   cos_ref, sin_ref, prev_ref, o_ref, *, nb, n_kv):
    del prev_ref
    n = pl.program_id(1)
    blk, dh = ATTN_BLOCK, ATTN_HEAD_DIM
    qw = ATTN_GROUP * dh

    def table(ref, i):
        return ref[pl.ds(pl.multiple_of(i * blk, blk), blk), :]

    n_prev, n_next = jnp.maximum(n - 1, 0), jnp.minimum(n + 1, nb - 1)
    cos_q, sin_q = table(cos_ref, n), table(sin_ref, n)
    cos_p, sin_p = table(cos_ref, n_prev), table(sin_ref, n_prev)
    cos_n, sin_n = table(cos_ref, n_next), table(sin_ref, n_next)
    n_keys = 3 * blk + kx_ref.shape[0]
    qi = lax.broadcasted_iota(jnp.int32, (ATTN_GROUP * blk, n_keys), 0) & (blk - 1)
    kj = lax.broadcasted_iota(jnp.int32, (ATTN_GROUP * blk, n_keys), 1)
    lo = jnp.where(n > 0, 0, blk)
    hi = jnp.where(n < nb - 1, 3 * blk, 2 * blk)
    band = (kj >= qi) & (kj <= qi + 2 * blk) & (kj >= lo) & (kj < hi)
    mask = band | (kj >= 3 * blk)

    for j in range(n_kv):
        ksl = slice(j * dh, (j + 1) * dh)
        q = q_ref[:, j * qw:(j + 1) * qw].astype(F32)
        q4 = jnp.concatenate(
            [_rope(q[:, g * dh:(g + 1) * dh], cos_q, sin_q) for g in range(ATTN_GROUP)], axis=0).astype(BF16)
        kcat = jnp.concatenate([
            _rope(kp_ref[:, ksl].astype(F32), cos_p, sin_p).astype(BF16),
            _rope(kc_ref[:, ksl].astype(F32), cos_q, sin_q).astype(BF16),
            _rope(kn_ref[:, ksl].astype(F32), cos_n, sin_n).astype(BF16),
            kx_ref[:, ksl]], axis=0)
        vcat = jnp.concatenate([vp_ref[:, ksl], vc_ref[:, ksl], vn_ref[:, ksl], vx_ref[:, ksl]], axis=0)
        o = _sink_attention(q4, kcat, vcat, _sink_column(sink_ref, j, blk), mask)
        for g in range(ATTN_GROUP):
            c0 = j * qw + g * dh
            o_ref[:, c0:c0 + dh] = o[g * blk:(g + 1) * blk].astype(o_ref.dtype)


def window_attention(p, sink, cos, sin_signed, prev, *, batch, seq, ctx_len, q_col, k_col, v_col):
    blk, dh = ATTN_BLOCK, ATTN_HEAD_DIM
    nb = seq // blk
    n_kv = (k_col - q_col) // (ATTN_GROUP * dh)
    lat0 = batch * ctx_len // blk
    qw = n_kv * ATTN_GROUP * dh
    kw = n_kv * dh

    def kv_spec(col, shift):
        return pl.BlockSpec((blk, kw), lambda b, n: (lat0 + b * nb + jnp.clip(n + shift, 0, nb - 1), col // kw))

    in_specs = [
        pl.BlockSpec(memory_space=pltpu.SMEM),
        pl.BlockSpec((blk, qw), lambda b, n: (lat0 + b * nb + n, q_col // qw)),
        kv_spec(k_col, -1), kv_spec(k_col, 0), kv_spec(k_col, 1),
        kv_spec(v_col, -1), kv_spec(v_col, 0), kv_spec(v_col, 1),
        pl.BlockSpec((ctx_len, kw), lambda b, n: (b, k_col // kw)),
        pl.BlockSpec((ctx_len, kw), lambda b, n: (b, v_col // kw)),
        pl.BlockSpec((seq, dh), lambda b, n: (0, 0)),
        pl.BlockSpec((seq, dh), lambda b, n: (0, 0)),
        pl.BlockSpec(memory_space=pl.ANY),
    ]
    return pl.pallas_call(
        functools.partial(_win_attn_body, nb=nb, n_kv=n_kv),
        out_shape=jax.ShapeDtypeStruct(prev.shape, prev.dtype),
        grid=(batch, nb),
        in_specs=in_specs,
        out_specs=pl.BlockSpec((blk, qw), lambda b, n: (lat0 + b * nb + n, 0)),
        input_output_aliases={12: 0},
        compiler_params=_cparams(2, 32 * MIB),
        name="window_attention",
    )(sink, p, p, p, p, p, p, p, p, p, cos, sin_signed, prev)


def _ctx_attn_body(sink_ref, q_ref, k_ref, v_ref, o_ref, *, n_kv):
    dh = ATTN_HEAD_DIM
    qw = ATTN_GROUP * dh
    rows = q_ref.shape[0]
    for j in range(n_kv):
        ksl = slice(j * dh, (j + 1) * dh)
        q4 = jnp.concatenate([q_ref[:, j * qw + g * dh:j * qw + (g + 1) * dh] for g in range(ATTN_GROUP)], axis=0)
        o = _sink_attention(q4, k_ref[:, ksl], v_ref[:, ksl], _sink_column(sink_ref, j, rows), None)
        for g in range(ATTN_GROUP):
            c0 = j * qw + g * dh
            o_ref[:, c0:c0 + dh] = o[g * rows:(g + 1) * rows].astype(o_ref.dtype)


def context_attention(p, sink, *, batch, ctx_len, q_col, k_col, v_col):
    dh = ATTN_HEAD_DIM
    n_kv = (k_col - q_col) // (ATTN_GROUP * dh)
    qw = n_kv * ATTN_GROUP * dh
    kw = n_kv * dh
    return pl.pallas_call(
        functools.partial(_ctx_attn_body, n_kv=n_kv),
        out_shape=jax.ShapeDtypeStruct((p.shape[0], qw), BF16),
        grid=(batch,),
        in_specs=[
            pl.BlockSpec(memory_space=pltpu.SMEM),
            pl.BlockSpec((ctx_len, qw), lambda b: (b, q_col // qw)),
            pl.BlockSpec((ctx_len, kw), lambda b: (b, k_col // kw)),
            pl.BlockSpec((ctx_len, kw), lambda b: (b, v_col // kw)),
        ],
        out_specs=pl.BlockSpec((ctx_len, qw), lambda b: (b, 0)),
        compiler_params=_cparams(1, 32 * MIB),
        name="context_attention",
    )(sink, p, p, p)


def _direction_mask(size, fwd):
    t = lax.broadcasted_iota(jnp.int32, (size, size), 0)
    s = lax.broadcasted_iota(jnp.int32, (size, size), 1)
    diff = t - s
    return jnp.where(fwd, diff, -diff) >= 0


def _segment_chunk(c, d, n_ctx_chunks, n_lat_chunks):
    is_ctx = c < n_ctx_chunks
    fwd_i = jnp.where(is_ctx, c, c - n_ctx_chunks)
    last = jnp.where(is_ctx, n_ctx_chunks - 1, n_lat_chunks - 1)
    return is_ctx, jnp.where(d == 0, fwd_i, last - fwd_i)


def _mlstm_body(q_ref, k_ref, v0_ref, v1_ref, g_ref, gt_ref, gb_row_ref, gb_col_ref, o_ref, cn_ref, m_ref):
    d, c = pl.program_id(1), pl.program_id(2)
    ch, dk, dv, nh = MLSTM_CHUNK, MLSTM_QK, MLSTM_V, MLSTM_HEADS

    @pl.when(c == 0)
    def _():
        cn_ref[...] = jnp.zeros_like(cn_ref)
        m_ref[...] = jnp.zeros_like(m_ref)

    fwd = d == 0
    valid = _direction_mask(ch, fwd)
    tri = jnp.where(valid, 1.0, 0.0).astype(BF16)
    last_t = jnp.where(fwd, ch - 1, 0)
    t_col = lax.broadcasted_iota(jnp.int32, (ch, 1), 0)
    is_last = t_col == last_t

    g_col = g_ref[...] + gb_row_ref[...]
    g_row = gt_ref[...] + gb_col_ref[...]
    lf_col, lf_row = _log_sigmoid(g_col), _log_sigmoid(g_row)
    bc_col_all = sum(jnp.dot(tri, part, preferred_element_type=F32) for part in _split3(lf_col))
    bc_row_all = sum(lax.dot_general(part, tri, (((1,), (1,)), ((), ())), preferred_element_type=F32)
                     for part in _split3(lf_row))
    ones_col = jnp.where(lax.broadcasted_iota(jnp.int32, (ch, LANE), 1) == 0, 1.0, 0.0)

    for h in range(nh):
        ci_f, ci_b = h, 2 * nh + h
        cf_f, cf_b = nh + h, 3 * nh + h
        ig_col = jnp.where(fwd, g_col[:, ci_f:ci_f + 1], g_col[:, ci_b:ci_b + 1])
        ig_row = jnp.where(fwd, g_row[ci_f:ci_f + 1, :], g_row[ci_b:ci_b + 1, :])
        bc_col = jnp.where(fwd, bc_col_all[:, cf_f:cf_f + 1], bc_col_all[:, cf_b:cf_b + 1])
        bc_row = jnp.where(fwd, bc_row_all[cf_f:cf_f + 1, :], bc_row_all[cf_b:cf_b + 1, :])
        m_st = m_ref[h:h + 1, 0:1]
        q = q_ref[:, h * dk:(h + 1) * dk]
        k = k_ref[:, h * dk:(h + 1) * dk] * (dk ** -0.5)
        v_ref = v0_ref if h < nh // 2 else v1_ref
        hv = h % (nh // 2)
        v = v_ref[:, hv * dv:(hv + 1) * dv]

        log_d = jnp.where(valid, bc_col - bc_row + ig_row, -jnp.inf)
        inter = bc_col + m_st
        m_t = jnp.maximum(inter, jnp.max(log_d, axis=-1, keepdims=True))
        s = lax.dot_general(q, k, (((1,), (1,)), ((), ())), preferred_element_type=F32) * jnp.exp(log_d - m_t)
        w_inter = jnp.exp(inter - m_t)
        state = cn_ref[h]
        q_state = jnp.dot(q, state.astype(BF16), preferred_element_type=F32)
        num = jnp.dot(s.astype(BF16), v, preferred_element_type=F32) + w_inter * q_state[:, :dv]
        den = jnp.sum(s, axis=-1, keepdims=True) + w_inter * q_state[:, dv:dv + 1]
        o_ref[:, h * dv:(h + 1) * dv] = num * (1.0 / jnp.maximum(jnp.abs(den), jnp.exp(-m_t)))

        b_end = jnp.sum(jnp.where(is_last, bc_col, 0.0), axis=0, keepdims=True)
        m_new = jnp.sum(jnp.where(is_last, m_t, 0.0), axis=0, keepdims=True)
        wk = jnp.exp(b_end - bc_col + ig_col - m_new)
        decay = jnp.exp(b_end + m_st - m_new)
        v_ext = jnp.concatenate([v.astype(F32), ones_col], axis=1) * wk
        cn_ref[h] = decay * state + lax.dot_general(
            k, v_ext.astype(BF16), (((0,), (0,)), ((), ())), preferred_element_type=F32)
        m_ref[h:h + 1, :] = jnp.broadcast_to(m_new, (1, LANE))


def mlstm(p, gates, gate_b, *, batch, seq, ctx_len, q_col, k_col, v_col):
    ch, nh, dk, dv = MLSTM_CHUNK, MLSTM_HEADS, MLSTM_QK, MLSTM_V
    ncc, nlc = ctx_len // ch, seq // ch
    rows = p.shape[0]

    def rowblk(b, d, c):
        is_ctx, i = _segment_chunk(c, d, ncc, nlc)
        return jnp.where(is_ctx, b * ncc + i, batch * ncc + b * nlc + i)

    gb = jnp.zeros((LANE,), F32).at[:4 * nh].set(gate_b.astype(F32))
    qw, vw = nh * dk, nh * dv // 2
    return pl.pallas_call(
        _mlstm_body,
        out_shape=jax.ShapeDtypeStruct((2, rows, nh * dv), F32),
        grid=(batch, 2, ncc + nlc),
        in_specs=[
            pl.BlockSpec((ch, qw), lambda b, d, c: (rowblk(b, d, c), q_col // qw)),
            pl.BlockSpec((ch, qw), lambda b, d, c: (rowblk(b, d, c), k_col // qw)),
            pl.BlockSpec((ch, vw), lambda b, d, c: (rowblk(b, d, c), v_col // vw)),
            pl.BlockSpec((ch, vw), lambda b, d, c: (rowblk(b, d, c), v_col // vw + 1)),
            pl.BlockSpec((ch, LANE), lambda b, d, c: (rowblk(b, d, c), 0)),
            pl.BlockSpec((LANE, ch), lambda b, d, c: (0, rowblk(b, d, c))),
            pl.BlockSpec((1, LANE), lambda b, d, c: (0, 0)),
            pl.BlockSpec((LANE, 1), lambda b, d, c: (0, 0)),
        ],
        out_specs=pl.BlockSpec((None, ch, nh * dv), lambda b, d, c: (d, rowblk(b, d, c), 0)),
        scratch_shapes=[pltpu.VMEM((nh, dk, dv + LANE), F32), pltpu.VMEM((8, LANE), F32)],
        compiler_params=_cparams(3, 32 * MIB),
        name="mlstm",
    )(p, p, p, p, gates, gates.T, gb.reshape(1, LANE), gb.reshape(LANE, 1))


def _gla_body(q_ref, k_ref, v_ref, lr_ref, w2_ref, gb_ref, o_ref, st_ref):
    d, c = pl.program_id(1), pl.program_id(2)
    ch, dk, dv, nh = GLA_CHUNK, GLA_QK, GLA_V, GLA_HEADS

    @pl.when(c == 0)
    def _():
        st_ref[...] = jnp.zeros_like(st_ref)

    fwd = d == 0
    valid = _direction_mask(ch, fwd)
    tri = jnp.where(valid, 1.0, 0.0).astype(BF16)

    pre = jnp.dot(lr_ref[...].astype(BF16), w2_ref[...].astype(BF16), preferred_element_type=F32)
    lg = _log_sigmoid(pre + gb_ref[...]) / GLA_GATE_TEMP
    gcum = sum(jnp.dot(tri, part, preferred_element_type=F32) for part in _split3(lg))
    g_end = jnp.where(fwd, gcum[ch - 1:ch, :], gcum[0:1, :])
    e_pos, e_neg = jnp.exp(gcum), jnp.exp(-gcum)
    e_end = jnp.exp(g_end - gcum)
    decay = jnp.exp(g_end)

    for h in range(nh):
        sl = slice(h * dk, (h + 1) * dk)
        q = q_ref[:, sl].astype(F32) * (dk ** -0.5)
        k = k_ref[:, sl].astype(F32)
        v = v_ref[:, h * dv:(h + 1) * dv]
        q_dec = (q * e_pos[:, sl]).astype(BF16)
        k_inv = (k * e_neg[:, sl]).astype(BF16)
        k_end = (k * e_end[:, sl]).astype(BF16)
        att = lax.dot_general(q_dec, k_inv, (((1,), (1,)), ((), ())), preferred_element_type=F32)
        att = jnp.where(valid, att, 0.0).astype(BF16)
        st = st_ref[h]
        o = (jnp.dot(att, v, preferred_element_type=F32)
             + lax.dot_general(q_dec, st.astype(BF16), (((1,), (1,)), ((), ())), preferred_element_type=F32))
        o_ref[:, h * dv:(h + 1) * dv] = o
        st_ref[h] = decay[:, sl] * st + lax.dot_general(
            v, k_end, (((0,), (0,)), ((), ())), preferred_element_type=F32)


def gla(p, lowrank, gate_w2, gate_b, *, batch, seq, ctx_len, q_col, k_col, v_col):
    ch, nh, dk, dv = GLA_CHUNK, GLA_HEADS, GLA_QK, GLA_V
    ncc, nlc = ctx_len // ch, seq // ch
    rank = GLA_GATE_RANK

    def rowblk(b, d, c):
        is_ctx, i = _segment_chunk(c, d, ncc, nlc)
        return jnp.where(is_ctx, b * ncc + i, batch * ncc + b * nlc + i)

    def outblk(b, d, c):
        _, i = _segment_chunk(jnp.maximum(c, ncc), d, ncc, nlc)
        return b * nlc + i

    w2 = jnp.zeros((2, LANE, nh * dk), F32)
    w2 = w2.at[0, :rank].set(gate_w2[0]).at[1, rank:2 * rank].set(gate_w2[1])
    return pl.pallas_call(
        _gla_body,
        out_shape=jax.ShapeDtypeStruct((2, batch * seq, nh * dv), F32),
        grid=(batch, 2, ncc + nlc),
        in_specs=[
            pl.BlockSpec((ch, nh * dk), lambda b, d, c: (rowblk(b, d, c), q_col // (nh * dk))),
            pl.BlockSpec((ch, nh * dk), lambda b, d, c: (rowblk(b, d, c), k_col // (nh * dk))),
            pl.BlockSpec((ch, nh * dv), lambda b, d, c: (rowblk(b, d, c), v_col // (nh * dv))),
            pl.BlockSpec((ch, LANE), lambda b, d, c: (rowblk(b, d, c), 0)),
            pl.BlockSpec((None, LANE, nh * dk), lambda b, d, c: (d, 0, 0)),
            pl.BlockSpec((None, 1, nh * dk), lambda b, d, c: (d, 0, 0)),
        ],
        out_specs=pl.BlockSpec((None, ch, nh * dv), lambda b, d, c: (d, outblk(b, d, c), 0)),
        scratch_shapes=[pltpu.VMEM((nh, dv, dk), F32)],
        compiler_params=_cparams(3, 40 * MIB),
        name="gla",
    )(p, p, p, lowrank, w2, gate_b.reshape(2, 1, nh * dk))


def _head_out_body(hf_ref, hb_ref, g_ref, gain_ref, o_ref, *, silu_gate):
    h = hf_ref[...] + hb_ref[...]
    hn = h * lax.rsqrt(jnp.mean(h * h, axis=-1, keepdims=True) + NORM_EPS)
    g = g_ref[...].astype(F32)
    act = g * _sigmoid(g) if silu_gate else _sigmoid(g)
    o_ref[...] = (hn * gain_ref[...] * act).astype(o_ref.dtype)


def head_output(h2, p, gain, *, n_rows, p_row_off, gate_col, n_heads, dv, silu_gate, tr=512):
    roff = p_row_off // tr
    return pl.pallas_call(
        functools.partial(_head_out_body, silu_gate=silu_gate),
        out_shape=jax.ShapeDtypeStruct((n_rows, n_heads * dv), BF16),
        grid=(n_rows // tr, n_heads),
        in_specs=[
            pl.BlockSpec((None, tr, dv), lambda t, h: (0, t, h)),
            pl.BlockSpec((None, tr, dv), lambda t, h: (1, t, h)),
            pl.BlockSpec((tr, dv), lambda t, h: (t + roff, gate_col // dv + h)),
            pl.BlockSpec((1, dv), lambda t, h: (0, h)),
        ],
        out_specs=pl.BlockSpec((tr, dv), lambda t, h: (t, h)),
        compiler_params=_cparams(2, 32 * MIB),
        name="head_output",
    )(h2, h2, p, gain.reshape(1, n_heads * dv).astype(F32))


def _rope_tables(seq):
    rows = seq // GRID_W
    row = jnp.repeat(jnp.arange(rows), GRID_W).astype(F32)
    col = jnp.tile(jnp.arange(GRID_W), rows).astype(F32)
    n_freq = ATTN_HEAD_DIM // 4
    inv = ROPE_THETA ** (-jnp.arange(n_freq, dtype=F32) / n_freq)
    ang_r, ang_c = row[:, None] * inv, col[:, None] * inv
    cos = jnp.concatenate([jnp.cos(ang_r), jnp.cos(ang_r), jnp.cos(ang_c), jnp.cos(ang_c)], axis=-1)
    sin = jnp.concatenate([-jnp.sin(ang_r), jnp.sin(ang_r), -jnp.sin(ang_c), jnp.sin(ang_c)], axis=-1)
    return cos, sin


def _conv_ffn(h, w_up, conv_w, conv_b, w_down, *, layer, n_ctx_rows, ctx_len, seq_len, tm):
    g = ffn_up(h, w_up, conv_w, conv_b, layer=layer, tm=tm, n_ctx_rows=n_ctx_rows, ctx_len=ctx_len,
               seq_len=seq_len)
    m, d_ff = g.shape
    d = w_down.shape[2]
    half = d_ff // 2
    kw = dict(layer=layer, m_blocks=m // 512, n_blocks=d // 512, tm=512, tn=512, tk=half)
    f = matmul(g, w_down, k_block=0, **kw)
    return matmul(g, w_down, k_block=1, acc=f, **kw)


def kernel(x, c, ctx, c_ctx, ada_w, ada_b, norm_w, ab_w_in, ab_gate_b, ab_sink, ab_head_norm, ab_w_out,
           gla_w_in, gla_gate_w2, gla_gate_b, gla_head_norm, gla_w_out,
           ffn_w_up, ffn_conv_w, ffn_conv_b, ffn_w_down):
    batch, seq, d = x.shape
    ctx_len = ctx.shape[1]
    n_ctx, n_lat = batch * ctx_len, batch * seq
    n_tok = n_ctx + n_lat
    depth = ada_w.shape[0]
    assert depth == 2 and ab_w_in.shape[0] == 1 and gla_w_in.shape[0] == 1
    tr = 256
    tm = math.gcd(math.gcd(n_ctx, seq), 1024)
    assert tm % 512 == 0 and ctx_len % tr == 0 and batch + 1 <= 8
    assert ctx_len & (ctx_len - 1) == 0 and seq & (seq - 1) == 0

    cvec = jnp.zeros((8, d), F32).at[:batch].set(c).at[batch].set(c_ctx)
    mods = ada_mods(cvec, ada_w, ada_b).reshape(depth, 8, 6, 1, d)

    def mod(i, j):
        return mods[i, :, j]

    ctx_tiles = n_ctx // tr
    lat_tiles_per_seq = seq // tr

    def group_all(t):
        return jnp.where(t < ctx_tiles, batch, (t - ctx_tiles) // lat_tiles_per_seq)

    def group_lat(t):
        return t // lat_tiles_per_seq

    ctx2, x2 = ctx.reshape(n_ctx, d), x.reshape(n_lat, d)

    nw = norm_w[0]
    (h,) = row_update(ctx2, group_all, n_tok, x_tail=x2, nwh=nw[0], shift=mod(0, 0), scale=mod(0, 1), tr=tr)
    n_kv = ab_sink.shape[1] // ATTN_GROUP
    aq, akv = n_kv * ATTN_GROUP * ATTN_HEAD_DIM, n_kv * ATTN_HEAD_DIM
    mqk, mv = MLSTM_HEADS * MLSTM_QK, MLSTM_HEADS * MLSTM_V
    cols = [0, aq, aq + akv, aq + 2 * akv, aq + 2 * akv + mqk, aq + 2 * akv + 2 * mqk,
            aq + 2 * akv + 2 * mqk + mv, aq + 2 * akv + 2 * mqk + 2 * mv]
    wide = cols[-1]
    p = matmul(h, ab_w_in, layer=0, m_blocks=n_tok // tm, n_blocks=wide // 512, tm=tm, tn=512,
               out_dtype=BF16)
    gates = matmul(h, ab_w_in, layer=0, m_blocks=n_tok // tm, n_blocks=1, tm=tm, tn=LANE,
                   n_off=wide // LANE, n_valid=ab_w_in.shape[2] - wide)
    cos, sin = _rope_tables(seq)
    sink = ab_sink[0].astype(F32)
    attn = context_attention(p, sink, batch=batch, ctx_len=ctx_len, q_col=cols[0], k_col=cols[1], v_col=cols[2])
    attn = window_attention(p, sink, cos, sin, attn, batch=batch, seq=seq, ctx_len=ctx_len,
                            q_col=cols[0], k_col=cols[1], v_col=cols[2])
    hm = mlstm(p, gates, ab_gate_b[0], batch=batch, seq=seq, ctx_len=ctx_len,
               q_col=cols[3], k_col=cols[4], v_col=cols[5])
    m_all = head_output(hm, p, ab_head_norm[0], n_rows=n_tok, p_row_off=0, gate_col=cols[6],
                        n_heads=MLSTM_HEADS, dv=MLSTM_V, silu_gate=False)
    y = matmul([attn, m_all], ab_w_out, layer=0, m_blocks=n_tok // tm, n_blocks=d // 512, tm=tm, tn=512)
    tok, h = row_update(ctx2, group_all, n_tok, x_tail=x2, y=y, gate=mod(0, 2), nwy=nw[1],
                        nwh=nw[2], shift=mod(0, 3), scale=mod(0, 4), tr=tr)
    f = _conv_ffn(h, ffn_w_up, ffn_conv_w, ffn_conv_b, ffn_w_down, layer=0, n_ctx_rows=n_ctx,
                  ctx_len=ctx_len, seq_len=seq, tm=tm)
    nw1 = norm_w[1]
    tok, h = row_update(tok, group_all, n_tok, y=f, gate=mod(0, 5), nwy=nw[3],
                        nwh=nw1[0], shift=mod(1, 0), scale=mod(1, 1), tr=tr)

    gq = GLA_HEADS * GLA_QK
    gv = GLA_HEADS * GLA_V
    wide = 2 * gq + 2 * gv
    p = matmul(h, gla_w_in, layer=0, m_blocks=n_tok // tm, n_blocks=wide // 512, tm=tm, tn=512,
               out_dtype=BF16)
    lowrank = matmul(h, gla_w_in, layer=0, m_blocks=n_tok // tm, n_blocks=1, tm=tm, tn=LANE,
                     n_off=wide // LANE, n_valid=gla_w_in.shape[2] - wide)
    o2 = gla(p, lowrank, gla_gate_w2[0], gla_gate_b[0], batch=batch, seq=seq, ctx_len=ctx_len,
             q_col=0, k_col=gq, v_col=2 * gq)
    o_all = head_output(o2, p, gla_head_norm[0], n_rows=n_lat, p_row_off=n_ctx, gate_col=2 * gq + gv,
                        n_heads=GLA_HEADS, dv=GLA_V, silu_gate=True)
    y = matmul(o_all, gla_w_out, layer=0, m_blocks=n_lat // tm, n_blocks=d // 512, tm=tm, tn=512)
    xl, h = row_update(tok, group_lat, n_lat, x_row_off=n_ctx, y=y, gate=mod(1, 2), nwy=nw1[1],
                       nwh=nw1[2], shift=mod(1, 3), scale=mod(1, 4), tr=tr)
    f = _conv_ffn(h, ffn_w_up, ffn_conv_w, ffn_conv_b, ffn_w_down, layer=1, n_ctx_rows=0,
                  ctx_len=ctx_len, seq_len=seq, tm=tm)
    (out,) = row_update(xl, group_lat, n_lat, y=f, gate=mod(1, 5), nwy=nw1[3], tr=tr)
    return out.reshape(batch, seq, d)
```

```python
import functools
import math

import jax
import jax.numpy as jnp
from jax import lax
from jax.experimental import pallas as pl
from jax.experimental.pallas import tpu as pltpu

F32 = jnp.float32
BF16 = jnp.bfloat16

NORM_EPS = 1e-6
ROPE_THETA = 10000.0
GRID_W = 64
ATTN_HEAD_DIM = 128
ATTN_GROUP = 4
ATTN_BLOCK = 128
MLSTM_HEADS = 4
MLSTM_QK = 256
MLSTM_V = 512
MLSTM_CHUNK = 128
GLA_HEADS = 8
GLA_QK = 256
GLA_V = 512
GLA_CHUNK = 64
GLA_GATE_TEMP = 16.0
GLA_GATE_RANK = 16
LANE = 128
MXU_DEPTH = 256
MIB = 1024 * 1024
VMEM_CAP = 60 * MIB


def _cparams(n_axes, vmem_bytes, flags=None):
    return pltpu.CompilerParams(
        dimension_semantics=("arbitrary",) * n_axes,
        vmem_limit_bytes=int(min(VMEM_CAP, vmem_bytes)),
        flags=flags,
    )


def _sigmoid(x):
    return 1.0 / (1.0 + jnp.exp(-x))


def _log_sigmoid(x):
    return jnp.minimum(x, 0.0) - jnp.log(1.0 + jnp.exp(-jnp.abs(x)))


def _split3(x):
    hi = x.astype(BF16)
    r1 = x - hi.astype(F32)
    mid = r1.astype(BF16)
    lo = (r1 - mid.astype(F32)).astype(BF16)
    return hi, mid, lo


def _ada_body(s_ref, w_ref, b_ref, o_ref):
    s = s_ref[...]
    s = (s * _sigmoid(s)).astype(BF16)
    o_ref[...] = jnp.dot(s, w_ref[...].astype(BF16), preferred_element_type=F32) + b_ref[...]


def ada_mods(cvec, ada_w, ada_b, tn=512):
    depth, d, n = ada_w.shape
    rows = cvec.shape[0]
    return pl.pallas_call(
        _ada_body,
        out_shape=jax.ShapeDtypeStruct((depth, rows, n), F32),
        grid=(depth, n // tn),
        in_specs=[
            pl.BlockSpec((rows, d), lambda i, j: (0, 0)),
            pl.BlockSpec((None, d, tn), lambda i, j: (i, 0, j)),
            pl.BlockSpec((None, 1, tn), lambda i, j: (i, 0, j)),
        ],
        out_specs=pl.BlockSpec((None, rows, tn), lambda i, j: (i, 0, j)),
        compiler_params=_cparams(2, 2 * d * tn * 4 + d * tn * 2 + 8 * MIB),
        name="ada_mods",
    )(cvec, ada_w, ada_b.reshape(depth, 1, n))


def _rms(x, w):
    return x * lax.rsqrt(jnp.mean(x * x, axis=-1, keepdims=True) + NORM_EPS) * w


def _row_body(*refs, has_y, has_h, split):
    it = iter(refs)
    x = next(it)[...]
    if split is not None:
        x = jnp.where(pl.program_id(0) < split, x, next(it)[...])
    if has_y:
        y = next(it)[...].astype(F32)
        gate = next(it)[...]
        nwy = next(it)[...]
    if has_h:
        nwh = next(it)[...]
        shift = next(it)[...]
        scale = next(it)[...]
    if has_y:
        x = x + gate * _rms(y, nwy)
        next(it)[...] = x
    if has_h:
        next(it)[...] = (_rms(x, nwh) * (1.0 + scale) + shift).astype(BF16)


def row_update(x, group_of_tile, n_rows, *, x_tail=None, x_row_off=0, y=None, gate=None, nwy=None,
               nwh=None, shift=None, scale=None, tr=256):
    d = x.shape[1]
    has_y, has_h = y is not None, nwh is not None
    off = x_row_off // tr
    row = pl.BlockSpec((tr, d), lambda t: (t, 0))
    vec = pl.BlockSpec((1, d), lambda t: (0, 0))
    tab = pl.BlockSpec((None, 1, d), lambda t: (group_of_tile(t), 0, 0))
    split = None
    if x_tail is None:
        args, specs = [x], [pl.BlockSpec((tr, d), lambda t: (t + off, 0))]
    else:
        split = x.shape[0] // tr
        args = [x, x_tail]
        specs = [pl.BlockSpec((tr, d), lambda t: (jnp.minimum(t, split - 1), 0)),
                 pl.BlockSpec((tr, d), lambda t: (jnp.maximum(t - split, 0), 0))]
    if has_y:
        args += [y, gate, nwy.reshape(1, d)]
        specs += [row, tab, vec]
    if has_h:
        args += [nwh.reshape(1, d), shift, scale]
        specs += [vec, tab, tab]
    out_shape, out_specs = [], []
    if has_y:
        out_shape.append(jax.ShapeDtypeStruct((n_rows, d), F32))
        out_specs.append(row)
    if has_h:
        out_shape.append(jax.ShapeDtypeStruct((n_rows, d), BF16))
        out_specs.append(row)
    return pl.pallas_call(
        functools.partial(_row_body, has_y=has_y, has_h=has_h, split=split),
        out_shape=out_shape,
        grid=(n_rows // tr,),
        in_specs=specs,
        out_specs=out_specs,
        compiler_params=_cparams(1, 2 * tr * d * 22 + 8 * MIB),
        name="row_update",
    )(*args)


def _mm_body(*refs, n_a, n_valid, has_acc):
    a_refs = refs[:n_a]
    w_ref = refs[n_a]
    acc_ref = refs[n_a + 1] if has_acc else None
    o_ref, wbf_ref = refs[-2], refs[-1]

    @pl.when(pl.program_id(1) == 0)
    def _():
        w = w_ref[...]
        if n_valid is not None:
            col = lax.broadcasted_iota(jnp.int32, w.shape, 1)
            w = jnp.where(col < n_valid, w, 0.0)
        wbf_ref[...] = w.astype(BF16)

    r, k0 = None, 0
    for a_ref in a_refs:
        k1 = k0 + a_ref.shape[1]
        part = jnp.dot(a_ref[...], wbf_ref[k0:k1, :], preferred_element_type=F32)
        r = part if r is None else r + part
        k0 = k1
    if has_acc:
        r = r + acc_ref[...]
    o_ref[...] = r.astype(o_ref.dtype)


def matmul(a, w, *, layer, m_blocks, n_blocks, tm, tn, tk=None, k_block=0, a_row_off=0, n_off=0,
           n_valid=None, acc=None, out_dtype=F32):
    a_list = list(a) if isinstance(a, (list, tuple)) else [a]
    k_total = sum(x.shape[1] for x in a_list)
    tk = k_total if tk is None else tk
    assert len(a_list) == 1 or tk == k_total
    roff = a_row_off // tm
    in_specs = [pl.BlockSpec((tm, tk if len(a_list) == 1 else x.shape[1]), lambda n, m: (m + roff, k_block))
                for x in a_list]
    in_specs.append(pl.BlockSpec((None, tk, tn), lambda n, m: (layer, k_block, n + n_off)))
    args = a_list + [w]
    out_spec = pl.BlockSpec((tm, tn), lambda n, m: (m, n))
    if acc is not None:
        in_specs.append(out_spec)
        args.append(acc)
    osz = jnp.dtype(out_dtype).itemsize
    vmem = (2 * tm * tk * 2 + 2 * tk * tn * 4 + tk * tn * 2 + 2 * tm * tn * osz
            + (2 * tm * tn * 4 if acc is not None else 0) + 2 * tm * tn * 4 + 4 * MIB)
    return pl.pallas_call(
        functools.partial(_mm_body, n_a=len(a_list), n_valid=n_valid, has_acc=acc is not None),
        out_shape=jax.ShapeDtypeStruct((m_blocks * tm, n_blocks * tn), out_dtype),
        grid=(n_blocks, m_blocks),
        in_specs=in_specs,
        out_specs=out_spec,
        scratch_shapes=[pltpu.VMEM((tk, tn), BF16)],
        compiler_params=_cparams(2, vmem),
        name="matmul",
    )(*args)


FFN_LAG = 2
FFN_RING = FFN_LAG + 1


def _ffn_up_body(a_ref, wg_ref, wv_ref, cwg_ref, cwv_ref, cbg_ref, cbv_ref, o_ref, wg_bf, wv_bf, ug_ref, uv_ref,
                 *, tm, tiles, total, n_ctx_rows, ctx_len, seq_len):
    s = pl.program_id(0)
    m = jnp.minimum(s, total - 1) % tiles
    m_out = jnp.maximum(s - FFN_LAG, 0) % tiles

    @pl.when(s == 0)
    def _():
        ug_ref[...] = jnp.zeros_like(ug_ref)
        uv_ref[...] = jnp.zeros_like(uv_ref)

    @pl.when(m == 0)
    def _():
        wg_bf[...] = wg_ref[...].astype(BF16)
        wv_bf[...] = wv_ref[...].astype(BF16)

    slot_new = s % FFN_RING
    slot_mid = (s + 1) % FFN_RING
    slot_next = (s + 2) % FFN_RING
    k_tiles = a_ref.shape[1] // MXU_DEPTH
    chunk = tm // k_tiles
    unit = min(chunk, ctx_len)
    row8 = lax.broadcasted_iota(jnp.int32, (8, o_ref.shape[1]), 0)

    def starts_sequence(g):
        return (g <= n_ctx_rows) | (((g - n_ctx_rows) & (seq_len - 1)) == 0)

    def conv(u_ref, cw_ref, cb_ref, r0):
        r1 = r0 + unit
        above = u_ref[slot_new, tm - 1:tm, :] if r0 == 0 else u_ref[slot_mid, r0 - 1:r0, :]
        below = u_ref[slot_next, 0:1, :] if r1 == tm else u_ref[slot_mid, r1:r1 + 1, :]
        if r0 % ctx_len == 0:
            above = jnp.where(starts_sequence(m_out * tm + r0), 0.0, above)
        if r1 % ctx_len == 0:
            below = jnp.where(starts_sequence(m_out * tm + r1), 0.0, below)
        u = u_ref[slot_mid, r0:r1, :]
        down, up = pltpu.roll(u, 1, 0), pltpu.roll(u, unit - 1, 0)
        u_prev = jnp.concatenate([jnp.where(row8 == 0, above, down[0:8]), down[8:]], axis=0)
        u_next = jnp.concatenate([up[:unit - 8], jnp.where(row8 == 7, below, up[unit - 8:])], axis=0)
        cw = cw_ref[...]
        return cb_ref[...] + u_prev * cw[0:1] + u * cw[1:2] + u_next * cw[2:3]

    sixteen = jnp.uint32(16)
    zeros = []
    for j in range(k_tiles):
        seen = None
        for r0 in range(j * chunk, (j + 1) * chunk, unit):
            gate = conv(ug_ref, cwg_ref, cbg_ref, r0)
            val = conv(uv_ref, cwv_ref, cbv_ref, r0)
            out = gate * _sigmoid(gate) * val
            o_ref[r0:r0 + unit, :] = out.astype(o_ref.dtype)
            bits = lax.bitcast_convert_type(out, jnp.uint32)
            for r in range(0, unit, 8):
                for c in range(0, bits.shape[1], LANE):
                    blk = bits[r:r + 8, c:c + LANE]
                    seen = blk if seen is None else seen | blk
        zeros.append(pltpu.bitcast(lax.shift_right_logical(lax.shift_right_logical(seen, sixteen), sixteen), BF16))

    def paced(w):
        parts = [w[0:MXU_DEPTH]]
        zr = zeros[0].shape[0]
        for k in range(1, k_tiles):
            blk = w[k * MXU_DEPTH:(k + 1) * MXU_DEPTH]
            head = jnp.concatenate([blk[0:zr, :LANE] + zeros[k - 1], blk[0:zr, LANE:]], axis=1)
            parts.append(jnp.concatenate([head, blk[zr:]], axis=0))
        return jnp.concatenate(parts, axis=0)

    a = a_ref[...]
    ug_ref[slot_new] = jnp.dot(a, paced(wg_bf[...]), preferred_element_type=F32)
    uv_ref[slot_new] = jnp.dot(a, paced(wv_bf[...]), preferred_element_type=F32)


def ffn_up(h, w_up, conv_w, conv_b, *, layer, tm, n_ctx_rows, ctx_len, seq_len, tn=256):
    m_rows, k = h.shape
    tiles = m_rows // tm
    d_ff = w_up.shape[2] // 2
    nb = d_ff // tn
    total = nb * tiles
    assert tm % ctx_len == 0 and seq_len % ctx_len == 0 and n_ctx_rows % ctx_len == 0 and tn % LANE == 0
    chunk = tm // (k // MXU_DEPTH)
    assert k % MXU_DEPTH == 0 and chunk % 8 == 0 and (chunk % ctx_len == 0 or ctx_len % chunk == 0)
    depth = w_up.shape[0]
    cb = conv_b.reshape(depth, 1, 2 * d_ff)

    def cur(s):
        sc = jnp.minimum(s, total - 1)
        return sc // tiles, sc % tiles

    def out(s):
        so = jnp.maximum(s - FFN_LAG, 0)
        return so // tiles, so % tiles

    vmem = (2 * tm * k * 2 + 2 * 2 * k * tn * 4 + 2 * k * tn * 2 + 2 * FFN_RING * tm * tn * 4
            + 2 * tm * tn * 2 + 12 * tm * tn * 4 + 4 * MIB)
    return pl.pallas_call(
        functools.partial(_ffn_up_body, tm=tm, tiles=tiles, total=total, n_ctx_rows=n_ctx_rows,
                          ctx_len=ctx_len, seq_len=seq_len),
        out_shape=jax.ShapeDtypeStruct((m_rows, d_ff), BF16),
        grid=(total + FFN_LAG,),
        in_specs=[
            pl.BlockSpec((tm, k), lambda s: (cur(s)[1], 0)),
            pl.BlockSpec((None, k, tn), lambda s: (layer, 0, cur(s)[0])),
            pl.BlockSpec((None, k, tn), lambda s: (layer, 0, cur(s)[0] + nb)),
            pl.BlockSpec((None, 3, tn), lambda s: (layer, 0, out(s)[0])),
            pl.BlockSpec((None, 3, tn), lambda s: (layer, 0, out(s)[0] + nb)),
            pl.BlockSpec((None, 1, tn), lambda s: (layer, 0, out(s)[0])),
            pl.BlockSpec((None, 1, tn), lambda s: (layer, 0, out(s)[0] + nb)),
        ],
        out_specs=pl.BlockSpec((tm, tn), lambda s: (out(s)[1], out(s)[0])),
        scratch_shapes=[pltpu.VMEM((k, tn), BF16), pltpu.VMEM((k, tn), BF16),
                        pltpu.VMEM((FFN_RING, tm, tn), F32), pltpu.VMEM((FFN_RING, tm, tn), F32)],
        compiler_params=_cparams(1, vmem),
        name="ffn_up",
    )(h, w_up, w_up, conv_w, conv_w, cb, cb)


def _rope(x, cos, sin_signed):
    lane = lax.broadcasted_iota(jnp.int32, x.shape, 1)
    q = ATTN_HEAD_DIM // 4
    partner = jnp.where((lane & (2 * q - 1)) < q, pltpu.roll(x, ATTN_HEAD_DIM - q, 1), pltpu.roll(x, q, 1))
    return x * cos + partner * sin_signed


def _sink_attention(q4, kcat, vcat, sink_col, mask):
    s = lax.dot_general(q4, kcat, (((1,), (1,)), ((), ())), preferred_element_type=F32)
    s = s * (ATTN_HEAD_DIM ** -0.5)
    if mask is not None:
        s = jnp.where(mask, s, -jnp.inf)
    m = jnp.maximum(jnp.max(s, axis=-1, keepdims=True), sink_col)
    e = jnp.exp(s - m)
    p = e / (jnp.sum(e, axis=-1, keepdims=True) + jnp.exp(sink_col - m))
    return jnp.dot(p.astype(BF16), vcat, preferred_element_type=F32)


def _sink_column(sink_ref, j, rows):
    return jnp.concatenate(
        [jnp.full((rows, 1), sink_ref[j * ATTN_GROUP + g], F32) for g in range(ATTN_GROUP)], axis=0)


def _win_attn_body(sink_ref, q_ref, kp_ref, kc_ref, kn_ref, vp_ref, vc_ref, vn_ref, kx_ref, vx_ref,
                   cos_ref, sin_ref, prev_ref, o_ref, *, nb, n_kv):
    del prev_ref
    n = pl.program_id(1)
    blk, dh = ATTN_BLOCK, ATTN_HEAD_DIM
    qw = ATTN_GROUP * dh

    def table(ref, i):
        return ref[pl.ds(pl.multiple_of(i * blk, blk), blk), :]

    n_prev, n_next = jnp.maximum(n - 1, 0), jnp.minimum(n + 1, nb - 1)
    cos_q, sin_q = table(cos_ref, n), table(sin_ref, n)
    cos_p, sin_p = table(cos_ref, n_prev), table(sin_ref, n_prev)
    cos_n, sin_n = table(cos_ref, n_next), table(sin_ref, n_next)
    n_keys = 3 * blk + kx_ref.shape[0]
    qi = lax.broadcasted_iota(jnp.int32, (ATTN_GROUP * blk, n_keys), 0) & (blk - 1)
    kj = lax.broadcasted_iota(jnp.int32, (ATTN_GROUP * blk, n_keys), 1)
    lo = jnp.where(n > 0, 0, blk)
    hi = jnp.where(n < nb - 1, 3 * blk, 2 * blk)
    band = (kj >= qi) & (kj <= qi + 2 * blk) & (kj >= lo) & (kj < hi)
    mask = band | (kj >= 3 * blk)

    for j in range(n_kv):
        ksl = slice(j * dh, (j + 1) * dh)
        q = q_ref[:, j * qw:(j + 1) * qw].astype(F32)
        q4 = jnp.concatenate(
            [_rope(q[:, g * dh:(g + 1) * dh], cos_q, sin_q) for g in range(ATTN_GROUP)], axis=0).astype(BF16)
        kcat = jnp.concatenate([
            _rope(kp_ref[:, ksl].astype(F32), cos_p, sin_p).astype(BF16),
            _rope(kc_ref[:, ksl].astype(F32), cos_q, sin_q).astype(BF16),
            _rope(kn_ref[:, ksl].astype(F32), cos_n, sin_n).astype(BF16),
            kx_ref[:, ksl]], axis=0)
        vcat = jnp.concatenate([vp_ref[:, ksl], vc_ref[:, ksl], vn_ref[:, ksl], vx_ref[:, ksl]], axis=0)
        o = _sink_attention(q4, kcat, vcat, _sink_column(sink_ref, j, blk), mask)
        for g in range(ATTN_GROUP):
            c0 = j * qw + g * dh
            o_ref[:, c0:c0 + dh] = o[g * blk:(g + 1) * blk].astype(o_ref.dtype)


def window_attention(p, sink, cos, sin_signed, prev, *, batch, seq, ctx_len, q_col, k_col, v_col):
    blk, dh = ATTN_BLOCK, ATTN_HEAD_DIM
    nb = seq // blk
    n_kv = (k_col - q_col) // (ATTN_GROUP * dh)
    lat0 = batch * ctx_len // blk
    qw = n_kv * ATTN_GROUP * dh
    kw = n_kv * dh

    def kv_spec(col, shift):
        return pl.BlockSpec((blk, kw), lambda b, n: (lat0 + b * nb + jnp.clip(n + shift, 0, nb - 1), col // kw))

    in_specs = [
        pl.BlockSpec(memory_space=pltpu.SMEM),
        pl.BlockSpec((blk, qw), lambda b, n: (lat0 + b * nb + n, q_col // qw)),
        kv_spec(k_col, -1), kv_spec(k_col, 0), kv_spec(k_col, 1),
        kv_spec(v_col, -1), kv_spec(v_col, 0), kv_spec(v_col, 1),
        pl.BlockSpec((ctx_len, kw), lambda b, n: (b, k_col // kw)),
        pl.BlockSpec((ctx_len, kw), lambda b, n: (b, v_col // kw)),
        pl.BlockSpec((seq, dh), lambda b, n: (0, 0)),
        pl.BlockSpec((seq, dh), lambda b, n: (0, 0)),
        pl.BlockSpec(memory_space=pl.ANY),
    ]
    return pl.pallas_call(
        functools.partial(_win_attn_body, nb=nb, n_kv=n_kv),
        out_shape=jax.ShapeDtypeStruct(prev.shape, prev.dtype),
        grid=(batch, nb),
        in_specs=in_specs,
        out_specs=pl.BlockSpec((blk, qw), lambda b, n: (lat0 + b * nb + n, 0)),
        input_output_aliases={12: 0},
        compiler_params=_cparams(2, 32 * MIB),
        name="window_attention",
    )(sink, p, p, p, p, p, p, p, p, p, cos, sin_signed, prev)


def _ctx_attn_body(sink_ref, q_ref, k_ref, v_ref, o_ref, *, n_kv):
    dh = ATTN_HEAD_DIM
    qw = ATTN_GROUP * dh
    rows = q_ref.shape[0]
    for j in range(n_kv):
        ksl = slice(j * dh, (j + 1) * dh)
        q4 = jnp.concatenate([q_ref[:, j * qw + g * dh:j * qw + (g + 1) * dh] for g in range(ATTN_GROUP)], axis=0)
        o = _sink_attention(q4, k_ref[:, ksl], v_ref[:, ksl], _sink_column(sink_ref, j, rows), None)
        for g in range(ATTN_GROUP):
            c0 = j * qw + g * dh
            o_ref[:, c0:c0 + dh] = o[g * rows:(g + 1) * rows].astype(o_ref.dtype)


def context_attention(p, sink, *, batch, ctx_len, q_col, k_col, v_col):
    dh = ATTN_HEAD_DIM
    n_kv = (k_col - q_col) // (ATTN_GROUP * dh)
    qw = n_kv * ATTN_GROUP * dh
    kw = n_kv * dh
    return pl.pallas_call(
        functools.partial(_ctx_attn_body, n_kv=n_kv),
        out_shape=jax.ShapeDtypeStruct((p.shape[0], qw), BF16),
        grid=(batch,),
        in_specs=[
            pl.BlockSpec(memory_space=pltpu.SMEM),
            pl.BlockSpec((ctx_len, qw), lambda b: (b, q_col // qw)),
            pl.BlockSpec((ctx_len, kw), lambda b: (b, k_col // kw)),
            pl.BlockSpec((ctx_len, kw), lambda b: (b, v_col // kw)),
        ],
        out_specs=pl.BlockSpec((ctx_len, qw), lambda b: (b, 0)),
        compiler_params=_cparams(1, 32 * MIB),
        name="context_attention",
    )(sink, p, p, p)


def _scan_row_block(c, b, batch, n_ctx_blocks, n_lat_blocks, reverse):
    is_ctx = c < n_ctx_blocks
    i = jnp.where(is_ctx, c, c - n_ctx_blocks)
    if reverse:
        i = jnp.where(is_ctx, n_ctx_blocks - 1, n_lat_blocks - 1) - i
    return jnp.where(is_ctx, b * n_ctx_blocks + i, batch * n_ctx_blocks + b * n_lat_blocks + i)


def _mlstm_body(q_ref, k_ref, v0_ref, v1_ref, g_ref, gt_ref, gb_row_ref, gb_col_ref, o_ref, cn_ref, m_ref,
                *, reverse):
    c = pl.program_id(1)
    ch, dk, dv, nh = MLSTM_CHUNK, MLSTM_QK, MLSTM_V, MLSTM_HEADS

    @pl.when(c == 0)
    def _():
        cn_ref[...] = jnp.zeros_like(cn_ref)
        m_ref[...] = jnp.zeros_like(m_ref)

    t = lax.broadcasted_iota(jnp.int32, (ch, ch), 0)
    s = lax.broadcasted_iota(jnp.int32, (ch, ch), 1)
    valid = (s >= t) if reverse else (s <= t)
    tri = jnp.where(valid, 1.0, 0.0).astype(BF16)
    last_t = 0 if reverse else ch - 1
    col_i = 2 * nh if reverse else 0
    col_f = 3 * nh if reverse else nh

    g_col = g_ref[...] + gb_row_ref[...]
    g_row = gt_ref[...] + gb_col_ref[...]
    lf_col, lf_row = _log_sigmoid(g_col), _log_sigmoid(g_row)
    bc_col_all = sum(jnp.dot(tri, part, preferred_element_type=F32) for part in _split3(lf_col))
    bc_row_all = sum(lax.dot_general(part, tri, (((1,), (1,)), ((), ())), preferred_element_type=F32)
                     for part in _split3(lf_row))
    ones_col = jnp.where(lax.broadcasted_iota(jnp.int32, (ch, LANE), 1) == 0, 1.0, 0.0)

    for h in range(nh):
        ig_col = g_col[:, col_i + h:col_i + h + 1]
        ig_row = g_row[col_i + h:col_i + h + 1, :]
        bc_col = bc_col_all[:, col_f + h:col_f + h + 1]
        bc_row = bc_row_all[col_f + h:col_f + h + 1, :]
        m_st = m_ref[h:h + 1, 0:1]
        q = q_ref[:, h * dk:(h + 1) * dk]
        k = k_ref[:, h * dk:(h + 1) * dk] * (dk ** -0.5)
        v_ref = v0_ref if h < nh // 2 else v1_ref
        hv = h % (nh // 2)
        v = v_ref[:, hv * dv:(hv + 1) * dv]

        log_d = jnp.where(valid, bc_col - bc_row + ig_row, -jnp.inf)
        inter = bc_col + m_st
        m_t = jnp.maximum(inter, jnp.max(log_d, axis=-1, keepdims=True))
        sc = lax.dot_general(q, k, (((1,), (1,)), ((), ())), preferred_element_type=F32) * jnp.exp(log_d - m_t)
        w_inter = jnp.exp(inter - m_t)
        state = cn_ref[h]
        q_state = jnp.dot(q, state.astype(BF16), preferred_element_type=F32)
        num = jnp.dot(sc.astype(BF16), v, preferred_element_type=F32) + w_inter * q_state[:, :dv]
        den = jnp.sum(sc, axis=-1, keepdims=True) + w_inter * q_state[:, dv:dv + 1]
        o_ref[:, h * dv:(h + 1) * dv] = (
            num * (1.0 / jnp.maximum(jnp.abs(den), jnp.exp(-m_t)))).astype(o_ref.dtype)

        b_end = bc_col[last_t:last_t + 1, :]
        m_new = m_t[last_t:last_t + 1, :]
        wk = jnp.exp(b_end - bc_col + ig_col - m_new)
        decay = jnp.exp(b_end + m_st - m_new)
        v_ext = jnp.concatenate([v.astype(F32), ones_col], axis=1) * wk
        cn_ref[h] = decay * state + lax.dot_general(
            k, v_ext.astype(BF16), (((0,), (0,)), ((), ())), preferred_element_type=F32)
        m_ref[h:h + 1, :] = jnp.broadcast_to(m_new, (1, LANE))


def mlstm(p, gates, gate_b, *, reverse, batch, seq, ctx_len, q_col, k_col, v_col):
    ch, nh, dk, dv = MLSTM_CHUNK, MLSTM_HEADS, MLSTM_QK, MLSTM_V
    ncc, nlc = ctx_len // ch, seq // ch
    rows = p.shape[0]
    blk = functools.partial(_scan_row_block, batch=batch, n_ctx_blocks=ncc, n_lat_blocks=nlc, reverse=reverse)
    gb = jnp.zeros((LANE,), F32).at[:4 * nh].set(gate_b.astype(F32))
    qw, vw = nh * dk, nh * dv // 2
    return pl.pallas_call(
        functools.partial(_mlstm_body, reverse=reverse),
        out_shape=jax.ShapeDtypeStruct((rows, nh * dv), BF16),
        grid=(batch, ncc + nlc),
        in_specs=[
            pl.BlockSpec((ch, qw), lambda b, c: (blk(c, b), q_col // qw)),
            pl.BlockSpec((ch, qw), lambda b, c: (blk(c, b), k_col // qw)),
            pl.BlockSpec((ch, vw), lambda b, c: (blk(c, b), v_col // vw)),
            pl.BlockSpec((ch, vw), lambda b, c: (blk(c, b), v_col // vw + 1)),
            pl.BlockSpec((ch, LANE), lambda b, c: (blk(c, b), 0)),
            pl.BlockSpec((LANE, ch), lambda b, c: (0, blk(c, b))),
            pl.BlockSpec((1, LANE), lambda b, c: (0, 0)),
            pl.BlockSpec((LANE, 1), lambda b, c: (0, 0)),
        ],
        out_specs=pl.BlockSpec((ch, nh * dv), lambda b, c: (blk(c, b), 0)),
        scratch_shapes=[pltpu.VMEM((nh, dk, dv + LANE), F32), pltpu.VMEM((8, LANE), F32)],
        compiler_params=_cparams(2, 32 * MIB),
        name="mlstm",
    )(p, p, p, p, gates, gates.T, gb.reshape(1, LANE), gb.reshape(LANE, 1))


GLA_BLOCK = 2 * GLA_CHUNK


def _gla_body(q_ref, k_ref, v_ref, lr_ref, w2_ref, gb_ref, o_ref, st_ref, *, reverse):
    c = pl.program_id(1)
    ch, rows, dk, dv, nh = GLA_CHUNK, GLA_BLOCK, GLA_QK, GLA_V, GLA_HEADS

    @pl.when(c == 0)
    def _():
        st_ref[...] = jnp.zeros_like(st_ref)

    t = lax.broadcasted_iota(jnp.int32, (rows, rows), 0)
    s = lax.broadcasted_iota(jnp.int32, (rows, rows), 1)
    same_chunk = (t ^ s) < ch
    intra = same_chunk & ((s >= t) if reverse else (s <= t))
    cross = ((t < ch) & (s >= ch)) if reverse else ((t >= ch) & (s < ch))
    tri = jnp.where(intra, 1.0, 0.0).astype(BF16)
    row = lax.broadcasted_iota(jnp.int32, (rows, 1), 0)
    in_a = (row >= ch) if reverse else (row < ch)

    pre = jnp.dot(lr_ref[...].astype(BF16), w2_ref[...].astype(BF16), preferred_element_type=F32)
    lg = _log_sigmoid(pre + gb_ref[...]) / GLA_GATE_TEMP
    gcum = sum(jnp.dot(tri, part, preferred_element_type=F32) for part in _split3(lg))
    end_a = gcum[ch:ch + 1, :] if reverse else gcum[ch - 1:ch, :]
    end_b = gcum[0:1, :] if reverse else gcum[rows - 1:rows, :]
    d_a, d_b = jnp.exp(end_a), jnp.exp(end_b)
    e_pos, e_neg = jnp.exp(gcum), jnp.exp(-gcum)
    e_end = jnp.exp(jnp.where(in_a, end_a, end_b) - gcum)
    e_pos_state = e_pos * jnp.where(in_a, 1.0, d_a)
    e_end_state = e_end * jnp.where(in_a, d_b, 1.0)
    decay = d_a * d_b

    for h in range(nh):
        sl = slice(h * dk, (h + 1) * dk)
        q = q_ref[:, sl].astype(F32) * (dk ** -0.5)
        k = k_ref[:, sl].astype(F32)
        v = v_ref[:, h * dv:(h + 1) * dv]
        q_dec = (q * e_pos[:, sl]).astype(BF16)
        q_state = (q * e_pos_state[:, sl]).astype(BF16)
        keys = jnp.concatenate([(k * e_neg[:, sl]).astype(BF16), (k * e_end[:, sl]).astype(BF16)], axis=0)
        k_state = (k * e_end_state[:, sl]).astype(BF16)
        sc = lax.dot_general(q_dec, keys, (((1,), (1,)), ((), ())), preferred_element_type=F32)
        att = jnp.where(intra, sc[:, :rows], jnp.where(cross, sc[:, rows:], 0.0)).astype(BF16)
        st = st_ref[h]
        o = (jnp.dot(att, v, preferred_element_type=F32)
             + lax.dot_general(q_state, st.astype(BF16), (((1,), (1,)), ((), ())), preferred_element_type=F32))
        o_ref[:, h * dv:(h + 1) * dv] = o.astype(o_ref.dtype)
        st_ref[h] = decay[:, sl] * st + lax.dot_general(
            v, k_state, (((0,), (0,)), ((), ())), preferred_element_type=F32)


def gla(p, lowrank, gate_w2, gate_b, *, reverse, batch, seq, ctx_len, q_col, k_col, v_col):
    rows, nh, dk, dv = GLA_BLOCK, GLA_HEADS, GLA_QK, GLA_V
    ncb, nlb = ctx_len // rows, seq // rows
    rank = GLA_GATE_RANK
    d = 1 if reverse else 0
    blk = functools.partial(_scan_row_block, batch=batch, n_ctx_blocks=ncb, n_lat_blocks=nlb, reverse=reverse)

    def out_blk(c, b):
        return blk(jnp.maximum(c, ncb), b) - batch * ncb

    w2 = jnp.zeros((LANE, nh * dk), F32).at[d * rank:(d + 1) * rank].set(gate_w2[d])
    return pl.pallas_call(
        functools.partial(_gla_body, reverse=reverse),
        out_shape=jax.ShapeDtypeStruct((batch * seq, nh * dv), BF16),
        grid=(batch, ncb + nlb),
        in_specs=[
            pl.BlockSpec((rows, nh * dk), lambda b, c: (blk(c, b), q_col // (nh * dk))),
            pl.BlockSpec((rows, nh * dk), lambda b, c: (blk(c, b), k_col // (nh * dk))),
            pl.BlockSpec((rows, nh * dv), lambda b, c: (blk(c, b), v_col // (nh * dv))),
            pl.BlockSpec((rows, LANE), lambda b, c: (blk(c, b), 0)),
            pl.BlockSpec((LANE, nh * dk), lambda b, c: (0, 0)),
            pl.BlockSpec((1, nh * dk), lambda b, c: (0, 0)),
        ],
        out_specs=pl.BlockSpec((rows, nh * dv), lambda b, c: (out_blk(c, b), 0)),
        scratch_shapes=[pltpu.VMEM((nh, dv, dk), F32)],
        compiler_params=_cparams(2, 40 * MIB),
        name="gla",
    )(p, p, p, lowrank, w2, gate_b[d].reshape(1, nh * dk))


def _head_out_body(hf_ref, hb_ref, g_ref, gain_ref, o_ref, *, silu_gate):
    h = hf_ref[...].astype(F32) + hb_ref[...].astype(F32)
    hn = h * lax.rsqrt(jnp.mean(h * h, axis=-1, keepdims=True) + NORM_EPS)
    g = g_ref[...].astype(F32)
    act = g * _sigmoid(g) if silu_gate else _sigmoid(g)
    o_ref[...] = (hn * gain_ref[...] * act).astype(o_ref.dtype)


def head_output(hf, hb, p, gain, *, n_rows, p_row_off, gate_col, n_heads, dv, silu_gate, tr=512):
    roff = p_row_off // tr
    return pl.pallas_call(
        functools.partial(_head_out_body, silu_gate=silu_gate),
        out_shape=jax.ShapeDtypeStruct((n_rows, n_heads * dv), BF16),
        grid=(n_rows // tr, n_heads),
        in_specs=[
            pl.BlockSpec((tr, dv), lambda t, h: (t, h)),
            pl.BlockSpec((tr, dv), lambda t, h: (t, h)),
            pl.BlockSpec((tr, dv), lambda t, h: (t + roff, gate_col // dv + h)),
            pl.BlockSpec((1, dv), lambda t, h: (0, h)),
        ],
        out_specs=pl.BlockSpec((tr, dv), lambda t, h: (t, h)),
        compiler_params=_cparams(2, 32 * MIB),
        name="head_output",
    )(hf, hb, p, gain.reshape(1, n_heads * dv).astype(F32))


def _rope_tables(seq):
    rows = seq // GRID_W
    row = jnp.repeat(jnp.arange(rows), GRID_W).astype(F32)
    col = jnp.tile(jnp.arange(GRID_W), rows).astype(F32)
    n_freq = ATTN_HEAD_DIM // 4
    inv = ROPE_THETA ** (-jnp.arange(n_freq, dtype=F32) / n_freq)
    ang_r, ang_c = row[:, None] * inv, col[:, None] * inv
    cos = jnp.concatenate([jnp.cos(ang_r), jnp.cos(ang_r), jnp.cos(ang_c), jnp.cos(ang_c)], axis=-1)
    sin = jnp.concatenate([-jnp.sin(ang_r), jnp.sin(ang_r), -jnp.sin(ang_c), jnp.sin(ang_c)], axis=-1)
    return cos, sin


def _conv_ffn(h, w_up, conv_w, conv_b, w_down, *, layer, n_ctx_rows, ctx_len, seq_len, tm):
    g = ffn_up(h, w_up, conv_w, conv_b, layer=layer, tm=tm, n_ctx_rows=n_ctx_rows, ctx_len=ctx_len,
               seq_len=seq_len)
    m, d_ff = g.shape
    d = w_down.shape[2]
    half = d_ff // 2
    kw = dict(layer=layer, m_blocks=m // 512, n_blocks=d // 512, tm=512, tn=512, tk=half)
    f = matmul(g, w_down, k_block=0, **kw)
    return matmul(g, w_down, k_block=1, acc=f, out_dtype=BF16, **kw)


def kernel(x, c, ctx, c_ctx, ada_w, ada_b, norm_w, ab_w_in, ab_gate_b, ab_sink, ab_head_norm, ab_w_out,
           gla_w_in, gla_gate_w2, gla_gate_b, gla_head_norm, gla_w_out,
           ffn_w_up, ffn_conv_w, ffn_conv_b, ffn_w_down):
    batch, seq, d = x.shape
    ctx_len = ctx.shape[1]
    n_ctx, n_lat = batch * ctx_len, batch * seq
    n_tok = n_ctx + n_lat
    depth = ada_w.shape[0]
    assert depth == 2 and ab_w_in.shape[0] == 1 and gla_w_in.shape[0] == 1
    tr = 256
    tm = math.gcd(math.gcd(n_ctx, seq), 1024)
    assert tm % 512 == 0 and ctx_len % tr == 0 and batch + 1 <= 8
    assert ctx_len & (ctx_len - 1) == 0 and seq & (seq - 1) == 0

    cvec = jnp.zeros((8, d), F32).at[:batch].set(c).at[batch].set(c_ctx)
    mods = ada_mods(cvec, ada_w, ada_b).reshape(depth, 8, 6, 1, d)

    def mod(i, j):
        return mods[i, :, j]

    ctx_tiles = n_ctx // tr
    lat_tiles_per_seq = seq // tr

    def group_all(t):
        return jnp.where(t < ctx_tiles, batch, (t - ctx_tiles) // lat_tiles_per_seq)

    def group_lat(t):
        return t // lat_tiles_per_seq

    ctx2, x2 = ctx.reshape(n_ctx, d), x.reshape(n_lat, d)

    nw = norm_w[0]
    (h,) = row_update(ctx2, group_all, n_tok, x_tail=x2, nwh=nw[0], shift=mod(0, 0), scale=mod(0, 1), tr=tr)
    n_kv = ab_sink.shape[1] // ATTN_GROUP
    aq, akv = n_kv * ATTN_GROUP * ATTN_HEAD_DIM, n_kv * ATTN_HEAD_DIM
    mqk, mv = MLSTM_HEADS * MLSTM_QK, MLSTM_HEADS * MLSTM_V
    cols = [0, aq, aq + akv, aq + 2 * akv, aq + 2 * akv + mqk, aq + 2 * akv + 2 * mqk,
            aq + 2 * akv + 2 * mqk + mv, aq + 2 * akv + 2 * mqk + 2 * mv]
    wide = cols[-1]
    p = matmul(h, ab_w_in, layer=0, m_blocks=n_tok // tm, n_blocks=wide // 512, tm=tm, tn=512,
               out_dtype=BF16)
    gates = matmul(h, ab_w_in, layer=0, m_blocks=n_tok // tm, n_blocks=1, tm=tm, tn=LANE,
                   n_off=wide // LANE, n_valid=ab_w_in.shape[2] - wide)
    cos, sin = _rope_tables(seq)
    sink = ab_sink[0].astype(F32)
    attn = context_attention(p, sink, batch=batch, ctx_len=ctx_len, q_col=cols[0], k_col=cols[1], v_col=cols[2])
    attn = window_attention(p, sink, cos, sin, attn, batch=batch, seq=seq, ctx_len=ctx_len,
                            q_col=cols[0], k_col=cols[1], v_col=cols[2])
    hf, hb = (mlstm(p, gates, ab_gate_b[0], reverse=rev, batch=batch, seq=seq, ctx_len=ctx_len,
                    q_col=cols[3], k_col=cols[4], v_col=cols[5]) for rev in (False, True))
    m_all = head_output(hf, hb, p, ab_head_norm[0], n_rows=n_tok, p_row_off=0, gate_col=cols[6],
                        n_heads=MLSTM_HEADS, dv=MLSTM_V, silu_gate=False)
    y = matmul([attn, m_all], ab_w_out, layer=0, m_blocks=n_tok // tm, n_blocks=d // 512, tm=tm, tn=512,
               out_dtype=BF16)
    tok, h = row_update(ctx2, group_all, n_tok, x_tail=x2, y=y, gate=mod(0, 2), nwy=nw[1],
                        nwh=nw[2], shift=mod(0, 3), scale=mod(0, 4), tr=tr)
    f = _conv_ffn(h, ffn_w_up, ffn_conv_w, ffn_conv_b, ffn_w_down, layer=0, n_ctx_rows=n_ctx,
                  ctx_len=ctx_len, seq_len=seq, tm=tm)
    nw1 = norm_w[1]
    tok, h = row_update(tok, group_all, n_tok, y=f, gate=mod(0, 5), nwy=nw[3],
                        nwh=nw1[0], shift=mod(1, 0), scale=mod(1, 1), tr=tr)

    gq = GLA_HEADS * GLA_QK
    gv = GLA_HEADS * GLA_V
    wide = 2 * gq + 2 * gv
    p = matmul(h, gla_w_in, layer=0, m_blocks=n_tok // tm, n_blocks=wide // 512, tm=tm, tn=512,
               out_dtype=BF16)
    lowrank = matmul(h, gla_w_in, layer=0, m_blocks=n_tok // tm, n_blocks=1, tm=tm, tn=LANE,
                     n_off=wide // LANE, n_valid=gla_w_in.shape[2] - wide)
    of, ob = (gla(p, lowrank, gla_gate_w2[0], gla_gate_b[0], reverse=rev, batch=batch, seq=seq,
                  ctx_len=ctx_len, q_col=0, k_col=gq, v_col=2 * gq) for rev in (False, True))
    o_all = head_output(of, ob, p, gla_head_norm[0], n_rows=n_lat, p_row_off=n_ctx, gate_col=2 * gq + gv,
                        n_heads=GLA_HEADS, dv=GLA_V, silu_gate=True)
    y = matmul(o_all, gla_w_out, layer=0, m_blocks=n_lat // tm, n_blocks=d // 512, tm=tm, tn=512,
               out_dtype=BF16)
    xl, h = row_update(tok, group_lat, n_lat, x_row_off=n_ctx, y=y, gate=mod(1, 2), nwy=nw1[1],
                       nwh=nw1[2], shift=mod(1, 3), scale=mod(1, 4), tr=tr)
    f = _conv_ffn(h, ffn_w_up, ffn_conv_w, ffn_conv_b, ffn_w_down, layer=1, n_ctx_rows=0,
                  ctx_len=ctx_len, seq_len=seq, tm=tm)
    (out,) = row_update(xl, group_lat, n_lat, y=f, gate=mod(1, 5), nwy=nw1[3], tr=tr)
    return out.reshape(batch, seq, d)
```

```python
import functools
import math

import jax
import jax.numpy as jnp
from jax import lax
from jax.experimental import pallas as pl
from jax.experimental.pallas import tpu as pltpu

F32 = jnp.float32
BF16 = jnp.bfloat16

NORM_EPS = 1e-6
ROPE_THETA = 10000.0
GRID_W = 64
ATTN_HEAD_DIM = 128
ATTN_GROUP = 4
ATTN_BLOCK = 128
MLSTM_HEADS = 4
MLSTM_QK = 256
MLSTM_V = 512
MLSTM_CHUNK = 128
GLA_HEADS = 8
GLA_QK = 256
GLA_V = 512
GLA_CHUNK = 64
GLA_GATE_TEMP = 16.0
GLA_GATE_RANK = 16
LANE = 128
MXU_DEPTH = 256
MIB = 1024 * 1024
VMEM_CAP = 60 * MIB


def _cparams(n_axes, vmem_bytes, flags=None):
    return pltpu.CompilerParams(
        dimension_semantics=("arbitrary",) * n_axes,
        vmem_limit_bytes=int(min(VMEM_CAP, vmem_bytes)),
        flags=flags,
    )


def _sigmoid(x):
    return 1.0 / (1.0 + jnp.exp(-x))


def _log_sigmoid(x):
    return jnp.minimum(x, 0.0) - jnp.log(1.0 + jnp.exp(-jnp.abs(x)))


def _split3(x):
    hi = x.astype(BF16)
    r1 = x - hi.astype(F32)
    mid = r1.astype(BF16)
    lo = (r1 - mid.astype(F32)).astype(BF16)
    return hi, mid, lo


def _ada_body(s_ref, w_ref, b_ref, o_ref):
    s = s_ref[...]
    s = (s * _sigmoid(s)).astype(BF16)
    o_ref[...] = jnp.dot(s, w_ref[...].astype(BF16), preferred_element_type=F32) + b_ref[...]


def ada_mods(cvec, ada_w, ada_b, tn=512):
    depth, d, n = ada_w.shape
    rows = cvec.shape[0]
    return pl.pallas_call(
        _ada_body,
        out_shape=jax.ShapeDtypeStruct((depth, rows, n), F32),
        grid=(depth, n // tn),
        in_specs=[
            pl.BlockSpec((rows, d), lambda i, j: (0, 0)),
            pl.BlockSpec((None, d, tn), lambda i, j: (i, 0, j)),
            pl.BlockSpec((None, 1, tn), lambda i, j: (i, 0, j)),
        ],
        out_specs=pl.BlockSpec((None, rows, tn), lambda i, j: (i, 0, j)),
        compiler_params=_cparams(2, 2 * d * tn * 4 + d * tn * 2 + 8 * MIB),
        name="ada_mods",
    )(cvec, ada_w, ada_b.reshape(depth, 1, n))


def _rms(x, w):
    return x * lax.rsqrt(jnp.mean(x * x, axis=-1, keepdims=True) + NORM_EPS) * w


def _row_body(*refs, has_y, has_h, split):
    it = iter(refs)
    x = next(it)[...]
    if split is not None:
        x = jnp.where(pl.program_id(0) < split, x, next(it)[...])
    if has_y:
        y = next(it)[...].astype(F32)
        gate = next(it)[...]
        nwy = next(it)[...]
    if has_h:
        nwh = next(it)[...]
        shift = next(it)[...]
        scale = next(it)[...]
    if has_y:
        x = x + gate * _rms(y, nwy)
        next(it)[...] = x
    if has_h:
        next(it)[...] = (_rms(x, nwh) * (1.0 + scale) + shift).astype(BF16)


def row_update(x, group_of_tile, n_rows, *, x_tail=None, x_row_off=0, y=None, gate=None, nwy=None,
               nwh=None, shift=None, scale=None, tr=256):
    d = x.shape[1]
    has_y, has_h = y is not None, nwh is not None
    off = x_row_off // tr
    row = pl.BlockSpec((tr, d), lambda t: (t, 0))
    vec = pl.BlockSpec((1, d), lambda t: (0, 0))
    tab = pl.BlockSpec((None, 1, d), lambda t: (group_of_tile(t), 0, 0))
    split = None
    if x_tail is None:
        args, specs = [x], [pl.BlockSpec((tr, d), lambda t: (t + off, 0))]
    else:
        split = x.shape[0] // tr
        args = [x, x_tail]
        specs = [pl.BlockSpec((tr, d), lambda t: (jnp.minimum(t, split - 1), 0)),
                 pl.BlockSpec((tr, d), lambda t: (jnp.maximum(t - split, 0), 0))]
    if has_y:
        args += [y, gate, nwy.reshape(1, d)]
        specs += [row, tab, vec]
    if has_h:
        args += [nwh.reshape(1, d), shift, scale]
        specs += [vec, tab, tab]
    out_shape, out_specs = [], []
    if has_y:
        out_shape.append(jax.ShapeDtypeStruct((n_rows, d), F32))
        out_specs.append(row)
    if has_h:
        out_shape.append(jax.ShapeDtypeStruct((n_rows, d), BF16))
        out_specs.append(row)
    return pl.pallas_call(
        functools.partial(_row_body, has_y=has_y, has_h=has_h, split=split),
        out_shape=out_shape,
        grid=(n_rows // tr,),
        in_specs=specs,
        out_specs=out_specs,
        compiler_params=_cparams(1, 2 * tr * d * 22 + 8 * MIB),
        name="row_update",
    )(*args)


def _mm_body(*refs, n_a, n_valid, has_acc, w_is_nk):
    a_refs = refs[:n_a]
    w_ref = refs[n_a]
    acc_ref = refs[n_a + 1] if has_acc else None
    o_ref, wbf_ref = refs[-2], refs[-1]

    @pl.when(pl.program_id(1) == 0)
    def _():
        w = w_ref[...]
        if n_valid is not None:
            col = lax.broadcasted_iota(jnp.int32, w.shape, 0 if w_is_nk else 1)
            w = jnp.where(col < n_valid, w, 0.0)
        wbf_ref[...] = w.astype(BF16)

    r, k0 = None, 0
    for a_ref in a_refs:
        k1 = k0 + a_ref.shape[1]
        if w_is_nk:
            part = lax.dot_general(a_ref[...], wbf_ref[:, k0:k1], (((1,), (1,)), ((), ())),
                                   preferred_element_type=F32)
        else:
            part = jnp.dot(a_ref[...], wbf_ref[k0:k1, :], preferred_element_type=F32)
        r = part if r is None else r + part
        k0 = k1
    if has_acc:
        r = r + acc_ref[...]
    o_ref[...] = r.astype(o_ref.dtype)


def matmul(a, w, *, layer, m_blocks, n_blocks, tm, tn, tk=None, k_block=0, a_row_off=0, n_off=0,
           n_valid=None, acc=None, out_dtype=F32, w_is_nk=False):
    a_list = list(a) if isinstance(a, (list, tuple)) else [a]
    k_total = sum(x.shape[1] for x in a_list)
    tk = k_total if tk is None else tk
    assert len(a_list) == 1 or tk == k_total
    roff = a_row_off // tm
    in_specs = [pl.BlockSpec((tm, tk if len(a_list) == 1 else x.shape[1]), lambda n, m: (m + roff, k_block))
                for x in a_list]
    if w_is_nk:
        in_specs.append(pl.BlockSpec((None, tn, tk), lambda n, m: (layer, n + n_off, k_block)))
    else:
        in_specs.append(pl.BlockSpec((None, tk, tn), lambda n, m: (layer, k_block, n + n_off)))
    args = a_list + [w]
    out_spec = pl.BlockSpec((tm, tn), lambda n, m: (m, n))
    if acc is not None:
        in_specs.append(out_spec)
        args.append(acc)
    osz = jnp.dtype(out_dtype).itemsize
    vmem = (2 * tm * tk * 2 + 2 * tk * tn * 4 + tk * tn * 2 + 2 * tm * tn * osz
            + (2 * tm * tn * 4 if acc is not None else 0) + 2 * tm * tn * 4 + 4 * MIB)
    return pl.pallas_call(
        functools.partial(_mm_body, n_a=len(a_list), n_valid=n_valid, has_acc=acc is not None, w_is_nk=w_is_nk),
        out_shape=jax.ShapeDtypeStruct((m_blocks * tm, n_blocks * tn), out_dtype),
        grid=(n_blocks, m_blocks),
        in_specs=in_specs,
        out_specs=out_spec,
        scratch_shapes=[pltpu.VMEM((tn, tk) if w_is_nk else (tk, tn), BF16)],
        compiler_params=_cparams(2, vmem),
        name="matmul",
    )(*args)


FFN_LAG = 2
FFN_RING = FFN_LAG + 1


def _ffn_up_body(a_ref, wg_ref, wv_ref, cwg_ref, cwv_ref, cbg_ref, cbv_ref, o_ref, wg_bf, wv_bf, ug_ref, uv_ref,
                 *, tm, tiles, total, n_ctx_rows, ctx_len, seq_len):
    s = pl.program_id(0)
    m = jnp.minimum(s, total - 1) % tiles
    m_out = jnp.maximum(s - FFN_LAG, 0) % tiles

    @pl.when(s == 0)
    def _():
        ug_ref[...] = jnp.zeros_like(ug_ref)
        uv_ref[...] = jnp.zeros_like(uv_ref)

    @pl.when(m == 0)
    def _():
        wg_bf[...] = wg_ref[...].astype(BF16)
        wv_bf[...] = wv_ref[...].astype(BF16)

    slot_new = s % FFN_RING
    slot_mid = (s + 1) % FFN_RING
    slot_next = (s + 2) % FFN_RING
    k_tiles = a_ref.shape[1] // MXU_DEPTH
    chunk = tm // k_tiles
    unit = min(chunk, ctx_len)
    row8 = lax.broadcasted_iota(jnp.int32, (8, o_ref.shape[1]), 0)

    def starts_sequence(g):
        return (g <= n_ctx_rows) | (((g - n_ctx_rows) & (seq_len - 1)) == 0)

    def conv(u_ref, cw_ref, cb_ref, r0):
        r1 = r0 + unit
        above = u_ref[slot_new, tm - 1:tm, :] if r0 == 0 else u_ref[slot_mid, r0 - 1:r0, :]
        below = u_ref[slot_next, 0:1, :] if r1 == tm else u_ref[slot_mid, r1:r1 + 1, :]
        if r0 % ctx_len == 0:
            above = jnp.where(starts_sequence(m_out * tm + r0), 0.0, above)
        if r1 % ctx_len == 0:
            below = jnp.where(starts_sequence(m_out * tm + r1), 0.0, below)
        u = u_ref[slot_mid, r0:r1, :]
        down, up = pltpu.roll(u, 1, 0), pltpu.roll(u, unit - 1, 0)
        u_prev = jnp.concatenate([jnp.where(row8 == 0, above, down[0:8]), down[8:]], axis=0)
        u_next = jnp.concatenate([up[:unit - 8], jnp.where(row8 == 7, below, up[unit - 8:])], axis=0)
        cw = cw_ref[...]
        return cb_ref[...] + u_prev * cw[0:1] + u * cw[1:2] + u_next * cw[2:3]

    sixteen = jnp.uint32(16)
    zeros = []
    for j in range(k_tiles):
        seen = None
        for r0 in range(j * chunk, (j + 1) * chunk, unit):
            gate = conv(ug_ref, cwg_ref, cbg_ref, r0)
            val = conv(uv_ref, cwv_ref, cbv_ref, r0)
            out = gate * _sigmoid(gate) * val
            o_ref[r0:r0 + unit, :] = out.astype(o_ref.dtype)
            bits = lax.bitcast_convert_type(out, jnp.uint32)
            for r in range(0, unit, 8):
                for c in range(0, bits.shape[1], LANE):
                    blk = bits[r:r + 8, c:c + LANE]
                    seen = blk if seen is None else seen | blk
        zeros.append(pltpu.bitcast(lax.shift_right_logical(lax.shift_right_logical(seen, sixteen), sixteen), BF16))

    def paced(w):
        parts = [w[0:MXU_DEPTH]]
        zr = zeros[0].shape[0]
        for k in range(1, k_tiles):
            blk = w[k * MXU_DEPTH:(k + 1) * MXU_DEPTH]
            head = jnp.concatenate([blk[0:zr, :LANE] + zeros[k - 1], blk[0:zr, LANE:]], axis=1)
            parts.append(jnp.concatenate([head, blk[zr:]], axis=0))
        return jnp.concatenate(parts, axis=0)

    a = a_ref[...]
    ug_ref[slot_new] = jnp.dot(a, paced(wg_bf[...]), preferred_element_type=F32)
    uv_ref[slot_new] = jnp.dot(a, paced(wv_bf[...]), preferred_element_type=F32)


def ffn_up(h, w_up, conv_w, conv_b, *, layer, tm, n_ctx_rows, ctx_len, seq_len, tn=256):
    m_rows, k = h.shape
    tiles = m_rows // tm
    d_ff = w_up.shape[2] // 2
    nb = d_ff // tn
    total = nb * tiles
    assert tm % ctx_len == 0 and seq_len % ctx_len == 0 and n_ctx_rows % ctx_len == 0 and tn % LANE == 0
    chunk = tm // (k // MXU_DEPTH)
    assert k % MXU_DEPTH == 0 and chunk % 8 == 0 and (chunk % ctx_len == 0 or ctx_len % chunk == 0)
    depth = w_up.shape[0]
    cb = conv_b.reshape(depth, 1, 2 * d_ff)

    def cur(s):
        sc = jnp.minimum(s, total - 1)
        return sc // tiles, sc % tiles

    def out(s):
        so = jnp.maximum(s - FFN_LAG, 0)
        return so // tiles, so % tiles

    vmem = (2 * tm * k * 2 + 2 * 2 * k * tn * 4 + 2 * k * tn * 2 + 2 * FFN_RING * tm * tn * 4
            + 2 * tm * tn * 2 + 12 * tm * tn * 4 + 4 * MIB)
    return pl.pallas_call(
        functools.partial(_ffn_up_body, tm=tm, tiles=tiles, total=total, n_ctx_rows=n_ctx_rows,
                          ctx_len=ctx_len, seq_len=seq_len),
        out_shape=jax.ShapeDtypeStruct((m_rows, d_ff), BF16),
        grid=(total + FFN_LAG,),
        in_specs=[
            pl.BlockSpec((tm, k), lambda s: (cur(s)[1], 0)),
            pl.BlockSpec((None, k, tn), lambda s: (layer, 0, cur(s)[0])),
            pl.BlockSpec((None, k, tn), lambda s: (layer, 0, cur(s)[0] + nb)),
            pl.BlockSpec((None, 3, tn), lambda s: (layer, 0, out(s)[0])),
            pl.BlockSpec((None, 3, tn), lambda s: (layer, 0, out(s)[0] + nb)),
            pl.BlockSpec((None, 1, tn), lambda s: (layer, 0, out(s)[0])),
            pl.BlockSpec((None, 1, tn), lambda s: (layer, 0, out(s)[0] + nb)),
        ],
        out_specs=pl.BlockSpec((tm, tn), lambda s: (out(s)[1], out(s)[0])),
        scratch_shapes=[pltpu.VMEM((k, tn), BF16), pltpu.VMEM((k, tn), BF16),
                        pltpu.VMEM((FFN_RING, tm, tn), F32), pltpu.VMEM((FFN_RING, tm, tn), F32)],
        compiler_params=_cparams(1, vmem),
        name="ffn_up",
    )(h, w_up, w_up, conv_w, conv_w, cb, cb)


def _rope(x, cos, sin_signed):
    lane = lax.broadcasted_iota(jnp.int32, x.shape, 1)
    q = ATTN_HEAD_DIM // 4
    partner = jnp.where((lane & (2 * q - 1)) < q, pltpu.roll(x, ATTN_HEAD_DIM - q, 1), pltpu.roll(x, q, 1))
    return x * cos + partner * sin_signed


def _sink_attention(q4, kcat, vcat, sink_col, mask):
    s = lax.dot_general(q4, kcat, (((1,), (1,)), ((), ())), preferred_element_type=F32)
    s = s * (ATTN_HEAD_DIM ** -0.5)
    if mask is not None:
        s = jnp.where(mask, s, -jnp.inf)
    m = jnp.maximum(jnp.max(s, axis=-1, keepdims=True), sink_col)
    e = jnp.exp(s - m)
    p = e / (jnp.sum(e, axis=-1, keepdims=True) + jnp.exp(sink_col - m))
    return jnp.dot(p.astype(BF16), vcat, preferred_element_type=F32)


def _sink_column(sink_ref, j, rows):
    return jnp.concatenate(
        [jnp.full((rows, 1), sink_ref[j * ATTN_GROUP + g], F32) for g in range(ATTN_GROUP)], axis=0)


def _win_attn_body(sink_ref, q_ref, kp_ref, kc_ref, kn_ref, vp_ref, vc_ref, vn_ref, kx_ref, vx_ref,
                   cos_ref, sin_ref, prev_ref, o_ref, *, nb, n_kv):
    del prev_ref
    n = pl.program_id(1)
    blk, dh = ATTN_BLOCK, ATTN_HEAD_DIM
    qw = ATTN_GROUP * dh

    def table(ref, i):
        return ref[pl.ds(pl.multiple_of(i * blk, blk), blk), :]

    n_prev, n_next = jnp.maximum(n - 1, 0), jnp.minimum(n + 1, nb - 1)
    cos_q, sin_q = table(cos_ref, n), table(sin_ref, n)
    cos_p, sin_p = table(cos_ref, n_prev), table(sin_ref, n_prev)
    cos_n, sin_n = table(cos_ref, n_next), table(sin_ref, n_next)
    n_keys = 3 * blk + kx_ref.shape[0]
    qi = lax.broadcasted_iota(jnp.int32, (ATTN_GROUP * blk, n_keys), 0) & (blk - 1)
    kj = lax.broadcasted_iota(jnp.int32, (ATTN_GROUP * blk, n_keys), 1)
    lo = jnp.where(n > 0, 0, blk)
    hi = jnp.where(n < nb - 1, 3 * blk, 2 * blk)
    band = (kj >= qi) & (kj <= qi + 2 * blk) & (kj >= lo) & (kj < hi)
    mask = band | (kj >= 3 * blk)

    for j in range(n_kv):
        ksl = slice(j * dh, (j + 1) * dh)
        q = q_ref[:, j * qw:(j + 1) * qw].astype(F32)
        q4 = jnp.concatenate(
            [_rope(q[:, g * dh:(g + 1) * dh], cos_q, sin_q) for g in range(ATTN_GROUP)], axis=0).astype(BF16)
        kcat = jnp.concatenate([
            _rope(kp_ref[:, ksl].astype(F32), cos_p, sin_p).astype(BF16),
            _rope(kc_ref[:, ksl].astype(F32), cos_q, sin_q).astype(BF16),
            _rope(kn_ref[:, ksl].astype(F32), cos_n, sin_n).astype(BF16),
            kx_ref[:, ksl]], axis=0)
        vcat = jnp.concatenate([vp_ref[:, ksl], vc_ref[:, ksl], vn_ref[:, ksl], vx_ref[:, ksl]], axis=0)
        o = _sink_attention(q4, kcat, vcat, _sink_column(sink_ref, j, blk), mask)
        for g in range(ATTN_GROUP):
            c0 = j * qw + g * dh
            o_ref[:, c0:c0 + dh] = o[g * blk:(g + 1) * blk].astype(o_ref.dtype)


def window_attention(p, sink, cos, sin_signed, prev, *, batch, seq, ctx_len, q_col, k_col, v_col):
    blk, dh = ATTN_BLOCK, ATTN_HEAD_DIM
    nb = seq // blk
    n_kv = (k_col - q_col) // (ATTN_GROUP * dh)
    lat0 = batch * ctx_len // blk
    qw = n_kv * ATTN_GROUP * dh
    kw = n_kv * dh

    def kv_spec(col, shift):
        return pl.BlockSpec((blk, kw), lambda b, n: (lat0 + b * nb + jnp.clip(n + shift, 0, nb - 1), col // kw))

    in_specs = [
        pl.BlockSpec(memory_space=pltpu.SMEM),
        pl.BlockSpec((blk, qw), lambda b, n: (lat0 + b * nb + n, q_col // qw)),
        kv_spec(k_col, -1), kv_spec(k_col, 0), kv_spec(k_col, 1),
        kv_spec(v_col, -1), kv_spec(v_col, 0), kv_spec(v_col, 1),
        pl.BlockSpec((ctx_len, kw), lambda b, n: (b, k_col // kw)),
        pl.BlockSpec((ctx_len, kw), lambda b, n: (b, v_col // kw)),
        pl.BlockSpec((seq, dh), lambda b, n: (0, 0)),
        pl.BlockSpec((seq, dh), lambda b, n: (0, 0)),
        pl.BlockSpec(memory_space=pl.ANY),
    ]
    return pl.pallas_call(
        functools.partial(_win_attn_body, nb=nb, n_kv=n_kv),
        out_shape=jax.ShapeDtypeStruct(prev.shape, prev.dtype),
        grid=(batch, nb),
        in_specs=in_specs,
        out_specs=pl.BlockSpec((blk, qw), lambda b, n: (lat0 + b * nb + n, 0)),
        input_output_aliases={12: 0},
        compiler_params=_cparams(2, 32 * MIB),
        name="window_attention",
    )(sink, p, p, p, p, p, p, p, p, p, cos, sin_signed, prev)


def _ctx_attn_body(sink_ref, q_ref, k_ref, v_ref, o_ref, *, n_kv, batch):
    dh = ATTN_HEAD_DIM
    qw = ATTN_GROUP * dh
    rows = q_ref.shape[0]

    @pl.when(pl.program_id(0) < batch)
    def _():
        for j in range(n_kv):
            ksl = slice(j * dh, (j + 1) * dh)
            q4 = jnp.concatenate(
                [q_ref[:, j * qw + g * dh:j * qw + (g + 1) * dh] for g in range(ATTN_GROUP)], axis=0)
            o = _sink_attention(q4, k_ref[:, ksl], v_ref[:, ksl], _sink_column(sink_ref, j, rows), None)
            for g in range(ATTN_GROUP):
                c0 = j * qw + g * dh
                o_ref[:, c0:c0 + dh] = o[g * rows:(g + 1) * rows].astype(o_ref.dtype)

    @pl.when(pl.program_id(0) >= batch)
    def _():
        o_ref[...] = jnp.zeros_like(o_ref)


def context_attention(p, sink, *, batch, ctx_len, q_col, k_col, v_col):
    dh = ATTN_HEAD_DIM
    n_kv = (k_col - q_col) // (ATTN_GROUP * dh)
    qw = n_kv * ATTN_GROUP * dh
    kw = n_kv * dh
    last = batch - 1
    return pl.pallas_call(
        functools.partial(_ctx_attn_body, n_kv=n_kv, batch=batch),
        out_shape=jax.ShapeDtypeStruct((p.shape[0], qw), BF16),
        grid=(p.shape[0] // ctx_len,),
        in_specs=[
            pl.BlockSpec(memory_space=pltpu.SMEM),
            pl.BlockSpec((ctx_len, qw), lambda b: (jnp.minimum(b, last), q_col // qw)),
            pl.BlockSpec((ctx_len, kw), lambda b: (jnp.minimum(b, last), k_col // kw)),
            pl.BlockSpec((ctx_len, kw), lambda b: (jnp.minimum(b, last), v_col // kw)),
        ],
        out_specs=pl.BlockSpec((ctx_len, qw), lambda b: (b, 0)),
        compiler_params=_cparams(1, 32 * MIB),
        name="context_attention",
    )(sink, p, p, p)


def _scan_row_block(c, b, batch, n_ctx_blocks, n_lat_blocks, reverse):
    is_ctx = c < n_ctx_blocks
    i = jnp.where(is_ctx, c, c - n_ctx_blocks)
    if reverse:
        i = jnp.where(is_ctx, n_ctx_blocks - 1, n_lat_blocks - 1) - i
    return jnp.where(is_ctx, b * n_ctx_blocks + i, batch * n_ctx_blocks + b * n_lat_blocks + i)


def _mlstm_body(q_ref, k_ref, v0_ref, v1_ref, g_ref, gt_ref, gb_row_ref, gb_col_ref, o_ref, cn_ref, m_ref,
                *, reverse):
    c = pl.program_id(1)
    ch, dk, dv, nh = MLSTM_CHUNK, MLSTM_QK, MLSTM_V, MLSTM_HEADS

    @pl.when(c == 0)
    def _():
        cn_ref[...] = jnp.zeros_like(cn_ref)
        m_ref[...] = jnp.zeros_like(m_ref)

    t = lax.broadcasted_iota(jnp.int32, (ch, ch), 0)
    s = lax.broadcasted_iota(jnp.int32, (ch, ch), 1)
    valid = (s >= t) if reverse else (s <= t)
    tri = jnp.where(valid, 1.0, 0.0).astype(BF16)
    last_t = 0 if reverse else ch - 1
    col_i = 2 * nh if reverse else 0
    col_f = 3 * nh if reverse else nh

    g_col = g_ref[...] + gb_row_ref[...]
    g_row = gt_ref[...] + gb_col_ref[...]
    lf_col, lf_row = _log_sigmoid(g_col), _log_sigmoid(g_row)
    bc_col_all = sum(jnp.dot(tri, part, preferred_element_type=F32) for part in _split3(lf_col))
    bc_row_all = sum(lax.dot_general(part, tri, (((1,), (1,)), ((), ())), preferred_element_type=F32)
                     for part in _split3(lf_row))
    ones_col = jnp.where(lax.broadcasted_iota(jnp.int32, (ch, LANE), 1) == 0, 1.0, 0.0)

    for h in range(nh):
        ig_col = g_col[:, col_i + h:col_i + h + 1]
        ig_row = g_row[col_i + h:col_i + h + 1, :]
        bc_col = bc_col_all[:, col_f + h:col_f + h + 1]
        bc_row = bc_row_all[col_f + h:col_f + h + 1, :]
        m_st = m_ref[h:h + 1, 0:1]
        q = q_ref[:, h * dk:(h + 1) * dk]
        k = k_ref[:, h * dk:(h + 1) * dk] * (dk ** -0.5)
        v_ref = v0_ref if h < nh // 2 else v1_ref
        hv = h % (nh // 2)
        v = v_ref[:, hv * dv:(hv + 1) * dv]

        log_d = jnp.where(valid, bc_col - bc_row + ig_row, -jnp.inf)
        inter = bc_col + m_st
        m_t = jnp.maximum(inter, jnp.max(log_d, axis=-1, keepdims=True))
        sc = lax.dot_general(q, k, (((1,), (1,)), ((), ())), preferred_element_type=F32) * jnp.exp(log_d - m_t)
        w_inter = jnp.exp(inter - m_t)
        state = cn_ref[h]
        q_state = jnp.dot(q, state.astype(BF16), preferred_element_type=F32)
        num = jnp.dot(sc.astype(BF16), v, preferred_element_type=F32) + w_inter * q_state[:, :dv]
        den = jnp.sum(sc, axis=-1, keepdims=True) + w_inter * q_state[:, dv:dv + 1]
        o_ref[:, h * dv:(h + 1) * dv] = (
            num * (1.0 / jnp.maximum(jnp.abs(den), jnp.exp(-m_t)))).astype(o_ref.dtype)

        b_end = bc_col[last_t:last_t + 1, :]
        m_new = m_t[last_t:last_t + 1, :]
        wk = jnp.exp(b_end - bc_col + ig_col - m_new)
        decay = jnp.exp(b_end + m_st - m_new)
        v_ext = jnp.concatenate([v.astype(F32), ones_col], axis=1) * wk
        cn_ref[h] = decay * state + lax.dot_general(
            k, v_ext.astype(BF16), (((0,), (0,)), ((), ())), preferred_element_type=F32)
        m_ref[h:h + 1, :] = jnp.broadcast_to(m_new, (1, LANE))


def mlstm(p, gates, gate_b, *, reverse, batch, seq, ctx_len, q_col, k_col, v_col):
    ch, nh, dk, dv = MLSTM_CHUNK, MLSTM_HEADS, MLSTM_QK, MLSTM_V
    ncc, nlc = ctx_len // ch, seq // ch
    rows = p.shape[0]
    blk = functools.partial(_scan_row_block, batch=batch, n_ctx_blocks=ncc, n_lat_blocks=nlc, reverse=reverse)
    gb = jnp.zeros((LANE,), F32).at[:4 * nh].set(gate_b.astype(F32))
    qw, vw = nh * dk, nh * dv // 2
    return pl.pallas_call(
        functools.partial(_mlstm_body, reverse=reverse),
        out_shape=jax.ShapeDtypeStruct((rows, nh * dv), BF16),
        grid=(batch, ncc + nlc),
        in_specs=[
            pl.BlockSpec((ch, qw), lambda b, c: (blk(c, b), q_col // qw)),
            pl.BlockSpec((ch, qw), lambda b, c: (blk(c, b), k_col // qw)),
            pl.BlockSpec((ch, vw), lambda b, c: (blk(c, b), v_col // vw)),
            pl.BlockSpec((ch, vw), lambda b, c: (blk(c, b), v_col // vw + 1)),
            pl.BlockSpec((ch, LANE), lambda b, c: (blk(c, b), 0)),
            pl.BlockSpec((LANE, ch), lambda b, c: (0, blk(c, b))),
            pl.BlockSpec((1, LANE), lambda b, c: (0, 0)),
            pl.BlockSpec((LANE, 1), lambda b, c: (0, 0)),
        ],
        out_specs=pl.BlockSpec((ch, nh * dv), lambda b, c: (blk(c, b), 0)),
        scratch_shapes=[pltpu.VMEM((nh, dk, dv + LANE), F32), pltpu.VMEM((8, LANE), F32)],
        compiler_params=_cparams(2, 32 * MIB),
        name="mlstm",
    )(p, p, p, p, gates, gates.T, gb.reshape(1, LANE), gb.reshape(LANE, 1))


GLA_BLOCK = 2 * GLA_CHUNK


def _gla_body(q_ref, k_ref, v_ref, lr_ref, w2_ref, gb_ref, o_ref, st_ref, *, reverse):
    c = pl.program_id(1)
    ch, rows, dk, dv, nh = GLA_CHUNK, GLA_BLOCK, GLA_QK, GLA_V, GLA_HEADS

    @pl.when(c == 0)
    def _():
        st_ref[...] = jnp.zeros_like(st_ref)

    t = lax.broadcasted_iota(jnp.int32, (rows, rows), 0)
    s = lax.broadcasted_iota(jnp.int32, (rows, rows), 1)
    same_chunk = (t ^ s) < ch
    intra = same_chunk & ((s >= t) if reverse else (s <= t))
    cross = ((t < ch) & (s >= ch)) if reverse else ((t >= ch) & (s < ch))
    tri = jnp.where(intra, 1.0, 0.0).astype(BF16)
    row = lax.broadcasted_iota(jnp.int32, (rows, 1), 0)
    in_a = (row >= ch) if reverse else (row < ch)

    pre = jnp.dot(lr_ref[...].astype(BF16), w2_ref[...].astype(BF16), preferred_element_type=F32)
    lg = _log_sigmoid(pre + gb_ref[...]) / GLA_GATE_TEMP
    gcum = sum(jnp.dot(tri, part, preferred_element_type=F32) for part in _split3(lg))
    end_a = gcum[ch:ch + 1, :] if reverse else gcum[ch - 1:ch, :]
    end_b = gcum[0:1, :] if reverse else gcum[rows - 1:rows, :]
    d_a, d_b = jnp.exp(end_a), jnp.exp(end_b)
    e_pos, e_neg = jnp.exp(gcum), jnp.exp(-gcum)
    e_end = jnp.exp(jnp.where(in_a, end_a, end_b) - gcum)
    e_pos_state = e_pos * jnp.where(in_a, 1.0, d_a)
    e_end_state = e_end * jnp.where(in_a, d_b, 1.0)
    decay = d_a * d_b

    for h in range(nh):
        sl = slice(h * dk, (h + 1) * dk)
        q = q_ref[:, sl].astype(F32) * (dk ** -0.5)
        k = k_ref[:, sl].astype(F32)
        v = v_ref[:, h * dv:(h + 1) * dv]
        q_dec = (q * e_pos[:, sl]).astype(BF16)
        q_state = (q * e_pos_state[:, sl]).astype(BF16)
        keys = jnp.concatenate([(k * e_neg[:, sl]).astype(BF16), (k * e_end[:, sl]).astype(BF16)], axis=0)
        k_state = (k * e_end_state[:, sl]).astype(BF16)
        sc = lax.dot_general(q_dec, keys, (((1,), (1,)), ((), ())), preferred_element_type=F32)
        att = jnp.where(intra, sc[:, :rows], jnp.where(cross, sc[:, rows:], 0.0)).astype(BF16)
        st = st_ref[h]
        o = (jnp.dot(att, v, preferred_element_type=F32)
             + lax.dot_general(q_state, st.astype(BF16), (((1,), (1,)), ((), ())), preferred_element_type=F32))
        o_ref[:, h * dv:(h + 1) * dv] = o.astype(o_ref.dtype)
        st_ref[h] = decay[:, sl] * st + lax.dot_general(
            v, k_state, (((0,), (0,)), ((), ())), preferred_element_type=F32)


def gla(p, lowrank, gate_w2, gate_b, *, reverse, batch, seq, ctx_len, q_col, k_col, v_col):
    rows, nh, dk, dv = GLA_BLOCK, GLA_HEADS, GLA_QK, GLA_V
    ncb, nlb = ctx_len // rows, seq // rows
    rank = GLA_GATE_RANK
    d = 1 if reverse else 0
    blk = functools.partial(_scan_row_block, batch=batch, n_ctx_blocks=ncb, n_lat_blocks=nlb, reverse=reverse)

    def out_blk(c, b):
        return blk(jnp.maximum(c, ncb), b) - batch * ncb

    w2 = jnp.zeros((LANE, nh * dk), F32).at[d * rank:(d + 1) * rank].set(gate_w2[d])
    return pl.pallas_call(
        functools.partial(_gla_body, reverse=reverse),
        out_shape=jax.ShapeDtypeStruct((batch * seq, nh * dv), BF16),
        grid=(batch, ncb + nlb),
        in_specs=[
            pl.BlockSpec((rows, nh * dk), lambda b, c: (blk(c, b), q_col // (nh * dk))),
            pl.BlockSpec((rows, nh * dk), lambda b, c: (blk(c, b), k_col // (nh * dk))),
            pl.BlockSpec((rows, nh * dv), lambda b, c: (blk(c, b), v_col // (nh * dv))),
            pl.BlockSpec((rows, LANE), lambda b, c: (blk(c, b), 0)),
            pl.BlockSpec((LANE, nh * dk), lambda b, c: (0, 0)),
            pl.BlockSpec((1, nh * dk), lambda b, c: (0, 0)),
        ],
        out_specs=pl.BlockSpec((rows, nh * dv), lambda b, c: (out_blk(c, b), 0)),
        scratch_shapes=[pltpu.VMEM((nh, dv, dk), F32)],
        compiler_params=_cparams(2, 40 * MIB),
        name="gla",
    )(p, p, p, lowrank, w2, gate_b[d].reshape(1, nh * dk))


def _head_out_body(hf_ref, hb_ref, g_ref, gain_ref, o_ref, *, silu_gate):
    h = hf_ref[...].astype(F32) + hb_ref[...].astype(F32)
    hn = h * lax.rsqrt(jnp.mean(h * h, axis=-1, keepdims=True) + NORM_EPS)
    g = g_ref[...].astype(F32)
    act = g * _sigmoid(g) if silu_gate else _sigmoid(g)
    o_ref[...] = (hn * gain_ref[...] * act).astype(o_ref.dtype)


def head_output(hf, hb, p, gain, *, n_rows, p_row_off, gate_col, n_heads, dv, silu_gate, tr=512):
    roff = p_row_off // tr
    return pl.pallas_call(
        functools.partial(_head_out_body, silu_gate=silu_gate),
        out_shape=jax.ShapeDtypeStruct((n_rows, n_heads * dv), BF16),
        grid=(n_rows // tr, n_heads),
        in_specs=[
            pl.BlockSpec((tr, dv), lambda t, h: (t, h)),
            pl.BlockSpec((tr, dv), lambda t, h: (t, h)),
            pl.BlockSpec((tr, dv), lambda t, h: (t + roff, gate_col // dv + h)),
            pl.BlockSpec((1, dv), lambda t, h: (0, h)),
        ],
        out_specs=pl.BlockSpec((tr, dv), lambda t, h: (t, h)),
        compiler_params=_cparams(2, 32 * MIB),
        name="head_output",
    )(hf, hb, p, gain.reshape(1, n_heads * dv).astype(F32))


def _rope_tables(seq):
    rows = seq // GRID_W
    row = jnp.repeat(jnp.arange(rows), GRID_W).astype(F32)
    col = jnp.tile(jnp.arange(GRID_W), rows).astype(F32)
    n_freq = ATTN_HEAD_DIM // 4
    inv = ROPE_THETA ** (-jnp.arange(n_freq, dtype=F32) / n_freq)
    ang_r, ang_c = row[:, None] * inv, col[:, None] * inv
    cos = jnp.concatenate([jnp.cos(ang_r), jnp.cos(ang_r), jnp.cos(ang_c), jnp.cos(ang_c)], axis=-1)
    sin = jnp.concatenate([-jnp.sin(ang_r), jnp.sin(ang_r), -jnp.sin(ang_c), jnp.sin(ang_c)], axis=-1)
    return cos, sin


def _conv_ffn(h, w_up, conv_w, conv_b, w_down, *, layer, n_ctx_rows, ctx_len, seq_len, tm):
    g = ffn_up(h, w_up, conv_w, conv_b, layer=layer, tm=tm, n_ctx_rows=n_ctx_rows, ctx_len=ctx_len,
               seq_len=seq_len)
    m, d_ff = g.shape
    d = w_down.shape[2]
    half = d_ff // 2
    kw = dict(layer=layer, m_blocks=m // 512, n_blocks=d // 512, tm=512, tn=512, tk=half)
    f = matmul(g, w_down, k_block=0, **kw)
    return matmul(g, w_down, k_block=1, acc=f, out_dtype=BF16, **kw)


def kernel(x, c, ctx, c_ctx, ada_w, ada_b, norm_w, ab_w_in, ab_gate_b, ab_sink, ab_head_norm, ab_w_out,
           gla_w_in, gla_gate_w2, gla_gate_b, gla_head_norm, gla_w_out,
           ffn_w_up, ffn_conv_w, ffn_conv_b, ffn_w_down):
    batch, seq, d = x.shape
    ctx_len = ctx.shape[1]
    n_ctx, n_lat = batch * ctx_len, batch * seq
    n_tok = n_ctx + n_lat
    depth = ada_w.shape[0]
    assert depth == 2 and ab_w_in.shape[0] == 1 and gla_w_in.shape[0] == 1
    tr = 256
    tm = math.gcd(math.gcd(n_ctx, seq), 1024)
    assert tm % 512 == 0 and ctx_len % tr == 0 and batch + 1 <= 8
    assert ctx_len & (ctx_len - 1) == 0 and seq & (seq - 1) == 0

    cvec = jnp.zeros((8, d), F32).at[:batch].set(c).at[batch].set(c_ctx)
    mods = ada_mods(cvec, ada_w, ada_b).reshape(depth, 8, 6, 1, d)

    def mod(i, j):
        return mods[i, :, j]

    ctx_tiles = n_ctx // tr
    lat_tiles_per_seq = seq // tr

    def group_all(t):
        return jnp.where(t < ctx_tiles, batch, (t - ctx_tiles) // lat_tiles_per_seq)

    def group_lat(t):
        return t // lat_tiles_per_seq

    ctx2, x2 = ctx.reshape(n_ctx, d), x.reshape(n_lat, d)

    nw = norm_w[0]
    (h,) = row_update(ctx2, group_all, n_tok, x_tail=x2, nwh=nw[0], shift=mod(0, 0), scale=mod(0, 1), tr=tr)
    n_kv = ab_sink.shape[1] // ATTN_GROUP
    aq, akv = n_kv * ATTN_GROUP * ATTN_HEAD_DIM, n_kv * ATTN_HEAD_DIM
    mqk, mv = MLSTM_HEADS * MLSTM_QK, MLSTM_HEADS * MLSTM_V
    cols = [0, aq, aq + akv, aq + 2 * akv, aq + 2 * akv + mqk, aq + 2 * akv + 2 * mqk,
            aq + 2 * akv + 2 * mqk + mv, aq + 2 * akv + 2 * mqk + 2 * mv]
    wide = cols[-1]
    w_in = jnp.swapaxes(ab_w_in, 1, 2)
    p = matmul(h, w_in, layer=0, m_blocks=n_tok // tm, n_blocks=wide // 512, tm=tm, tn=512,
               out_dtype=BF16, w_is_nk=True)
    gates = matmul(h, w_in, layer=0, m_blocks=n_tok // tm, n_blocks=1, tm=tm, tn=LANE,
                   n_off=wide // LANE, n_valid=ab_w_in.shape[2] - wide, w_is_nk=True)
    cos, sin = _rope_tables(seq)
    sink = ab_sink[0].astype(F32)
    attn = context_attention(p, sink, batch=batch, ctx_len=ctx_len, q_col=cols[0], k_col=cols[1], v_col=cols[2])
    attn = window_attention(p, sink, cos, sin, attn, batch=batch, seq=seq, ctx_len=ctx_len,
                            q_col=cols[0], k_col=cols[1], v_col=cols[2])
    hf, hb = (mlstm(p, gates, ab_gate_b[0], reverse=rev, batch=batch, seq=seq, ctx_len=ctx_len,
                    q_col=cols[3], k_col=cols[4], v_col=cols[5]) for rev in (False, True))
    m_all = head_output(hf, hb, p, ab_head_norm[0], n_rows=n_tok, p_row_off=0, gate_col=cols[6],
                        n_heads=MLSTM_HEADS, dv=MLSTM_V, silu_gate=False)
    y = matmul([attn, m_all], ab_w_out, layer=0, m_blocks=n_tok // tm, n_blocks=d // 512, tm=tm, tn=512,
               out_dtype=BF16)
    tok, h = row_update(ctx2, group_all, n_tok, x_tail=x2, y=y, gate=mod(0, 2), nwy=nw[1],
                        nwh=nw[2], shift=mod(0, 3), scale=mod(0, 4), tr=tr)
    f = _conv_ffn(h, ffn_w_up, ffn_conv_w, ffn_conv_b, ffn_w_down, layer=0, n_ctx_rows=n_ctx,
                  ctx_len=ctx_len, seq_len=seq, tm=tm)
    nw1 = norm_w[1]
    tok, h = row_update(tok, group_all, n_tok, y=f, gate=mod(0, 5), nwy=nw[3],
                        nwh=nw1[0], shift=mod(1, 0), scale=mod(1, 1), tr=tr)

    gq = GLA_HEADS * GLA_QK
    gv = GLA_HEADS * GLA_V
    wide = 2 * gq + 2 * gv
    w_in = jnp.swapaxes(gla_w_in, 1, 2)
    p = matmul(h, w_in, layer=0, m_blocks=n_tok // tm, n_blocks=wide // 512, tm=tm, tn=512,
               out_dtype=BF16, w_is_nk=True)
    lowrank = matmul(h, w_in, layer=0, m_blocks=n_tok // tm, n_blocks=1, tm=tm, tn=LANE,
                     n_off=wide // LANE, n_valid=gla_w_in.shape[2] - wide, w_is_nk=True)
    of, ob = (gla(p, lowrank, gla_gate_w2[0], gla_gate_b[0], reverse=rev, batch=batch, seq=seq,
                  ctx_len=ctx_len, q_col=0, k_col=gq, v_col=2 * gq) for rev in (False, True))
    o_all = head_output(of, ob, p, gla_head_norm[0], n_rows=n_lat, p_row_off=n_ctx, gate_col=2 * gq + gv,
                        n_heads=GLA_HEADS, dv=GLA_V, silu_gate=True)
    y = matmul(o_all, gla_w_out, layer=0, m_blocks=n_lat // tm, n_blocks=d // 512, tm=tm, tn=512,
               out_dtype=BF16)
    xl, h = row_update(tok, group_lat, n_lat, x_row_off=n_ctx, y=y, gate=mod(1, 2), nwy=nw1[1],
                       nwh=nw1[2], shift=mod(1, 3), scale=mod(1, 4), tr=tr)
    f = _conv_ffn(h, ffn_w_up, ffn_conv_w, ffn_conv_b, ffn_w_down, layer=1, n_ctx_rows=0,
                  ctx_len=ctx_len, seq_len=seq, tm=tm)
    (out,) = row_update(xl, group_lat, n_lat, y=f, gate=mod(1, 5), nwy=nw1[3], tr=tr)
    return out.reshape(batch, seq, d)
```

```python
import functools
import math

import jax
import jax.numpy as jnp
from jax import lax
from jax.experimental import pallas as pl
from jax.experimental.pallas import tpu as pltpu

F32 = jnp.float32
BF16 = jnp.bfloat16

NORM_EPS = 1e-6
ROPE_THETA = 10000.0
GRID_W = 64
ATTN_HEAD_DIM = 128
ATTN_GROUP = 4
ATTN_BLOCK = 128
MLSTM_HEADS = 4
MLSTM_QK = 256
MLSTM_V = 512
MLSTM_CHUNK = 128
GLA_HEADS = 8
GLA_QK = 256
GLA_V = 512
GLA_CHUNK = 64
GLA_GATE_TEMP = 16.0
GLA_GATE_RANK = 16
LANE = 128
MXU_DEPTH = 256
MIB = 1024 * 1024
VMEM_CAP = 60 * MIB


def _cparams(n_axes, vmem_bytes, flags=None):
    return pltpu.CompilerParams(
        dimension_semantics=("arbitrary",) * n_axes,
        vmem_limit_bytes=int(min(VMEM_CAP, vmem_bytes)),
        flags=flags,
    )


def _sigmoid(x):
    return 1.0 / (1.0 + jnp.exp(-x))


def _log_sigmoid(x):
    return jnp.minimum(x, 0.0) - jnp.log(1.0 + jnp.exp(-jnp.abs(x)))


def _split3(x):
    hi = x.astype(BF16)
    r1 = x - hi.astype(F32)
    mid = r1.astype(BF16)
    lo = (r1 - mid.astype(F32)).astype(BF16)
    return hi, mid, lo


def _ada_body(s_ref, w_ref, b_ref, o_ref):
    s = s_ref[...]
    s = (s * _sigmoid(s)).astype(BF16)
    o_ref[...] = jnp.dot(s, w_ref[...].astype(BF16), preferred_element_type=F32) + b_ref[...]


def ada_mods(cvec, ada_w, ada_b, tn=512):
    depth, d, n = ada_w.shape
    rows = cvec.shape[0]
    return pl.pallas_call(
        _ada_body,
        out_shape=jax.ShapeDtypeStruct((depth, rows, n), F32),
        grid=(depth, n // tn),
        in_specs=[
            pl.BlockSpec((rows, d), lambda i, j: (0, 0)),
            pl.BlockSpec((None, d, tn), lambda i, j: (i, 0, j)),
            pl.BlockSpec((None, 1, tn), lambda i, j: (i, 0, j)),
        ],
        out_specs=pl.BlockSpec((None, rows, tn), lambda i, j: (i, 0, j)),
        compiler_params=_cparams(2, 2 * d * tn * 4 + d * tn * 2 + 8 * MIB),
        name="ada_mods",
    )(cvec, ada_w, ada_b.reshape(depth, 1, n))


def _rms(x, w):
    return x * lax.rsqrt(jnp.mean(x * x, axis=-1, keepdims=True) + NORM_EPS) * w


def _row_body(*refs, has_y, has_h, split):
    it = iter(refs)
    x = next(it)[...]
    if split is not None:
        x = jnp.where(pl.program_id(0) < split, x, next(it)[...])
    if has_y:
        y = next(it)[...].astype(F32)
        gate = next(it)[...]
        nwy = next(it)[...]
    if has_h:
        nwh = next(it)[...]
        shift = next(it)[...]
        scale = next(it)[...]
    if has_y:
        x = x + gate * _rms(y, nwy)
        next(it)[...] = x
    if has_h:
        next(it)[...] = (_rms(x, nwh) * (1.0 + scale) + shift).astype(BF16)


def row_update(x, group_of_tile, n_rows, *, x_tail=None, x_row_off=0, y=None, gate=None, nwy=None,
               nwh=None, shift=None, scale=None, tr=256):
    d = x.shape[1]
    has_y, has_h = y is not None, nwh is not None
    off = x_row_off // tr
    row = pl.BlockSpec((tr, d), lambda t: (t, 0))
    vec = pl.BlockSpec((1, d), lambda t: (0, 0))
    tab = pl.BlockSpec((None, 1, d), lambda t: (group_of_tile(t), 0, 0))
    split = None
    if x_tail is None:
        args, specs = [x], [pl.BlockSpec((tr, d), lambda t: (t + off, 0))]
    else:
        split = x.shape[0] // tr
        args = [x, x_tail]
        specs = [pl.BlockSpec((tr, d), lambda t: (jnp.minimum(t, split - 1), 0)),
                 pl.BlockSpec((tr, d), lambda t: (jnp.maximum(t - split, 0), 0))]
    if has_y:
        args += [y, gate, nwy.reshape(1, d)]
        specs += [row, tab, vec]
    if has_h:
        args += [nwh.reshape(1, d), shift, scale]
        specs += [vec, tab, tab]
    out_shape, out_specs = [], []
    if has_y:
        out_shape.append(jax.ShapeDtypeStruct((n_rows, d), F32))
        out_specs.append(row)
    if has_h:
        out_shape.append(jax.ShapeDtypeStruct((n_rows, d), BF16))
        out_specs.append(row)
    return pl.pallas_call(
        functools.partial(_row_body, has_y=has_y, has_h=has_h, split=split),
        out_shape=out_shape,
        grid=(n_rows // tr,),
        in_specs=specs,
        out_specs=out_specs,
        compiler_params=_cparams(1, 2 * tr * d * 22 + 8 * MIB),
        name="row_update",
    )(*args)


def _mm_body(*refs, n_a, n_valid, has_acc, w_is_nk):
    a_refs = refs[:n_a]
    w_ref = refs[n_a]
    acc_ref = refs[n_a + 1] if has_acc else None
    o_ref, wbf_ref = refs[-2], refs[-1]

    @pl.when(pl.program_id(1) == 0)
    def _():
        w = w_ref[...]
        if n_valid is not None:
            col = lax.broadcasted_iota(jnp.int32, w.shape, 0 if w_is_nk else 1)
            w = jnp.where(col < n_valid, w, 0.0)
        wbf_ref[...] = w.astype(BF16)

    r, k0 = None, 0
    for a_ref in a_refs:
        k1 = k0 + a_ref.shape[1]
        if w_is_nk:
            part = lax.dot_general(a_ref[...], wbf_ref[:, k0:k1], (((1,), (1,)), ((), ())),
                                   preferred_element_type=F32)
        else:
            part = jnp.dot(a_ref[...], wbf_ref[k0:k1, :], preferred_element_type=F32)
        r = part if r is None else r + part
        k0 = k1
    if has_acc:
        r = r + acc_ref[...]
    o_ref[...] = r.astype(o_ref.dtype)


def matmul(a, w, *, layer, m_blocks, n_blocks, tm, tn, tk=None, k_block=0, a_row_off=0, n_off=0,
           n_valid=None, acc=None, out_dtype=F32, w_is_nk=False):
    a_list = list(a) if isinstance(a, (list, tuple)) else [a]
    k_total = sum(x.shape[1] for x in a_list)
    tk = k_total if tk is None else tk
    assert len(a_list) == 1 or tk == k_total
    roff = a_row_off // tm
    in_specs = [pl.BlockSpec((tm, tk if len(a_list) == 1 else x.shape[1]), lambda n, m: (m + roff, k_block))
                for x in a_list]
    if w_is_nk:
        in_specs.append(pl.BlockSpec((None, tn, tk), lambda n, m: (layer, n + n_off, k_block)))
    else:
        in_specs.append(pl.BlockSpec((None, tk, tn), lambda n, m: (layer, k_block, n + n_off)))
    args = a_list + [w]
    out_spec = pl.BlockSpec((tm, tn), lambda n, m: (m, n))
    if acc is not None:
        in_specs.append(out_spec)
        args.append(acc)
    osz = jnp.dtype(out_dtype).itemsize
    vmem = (2 * tm * tk * 2 + 2 * tk * tn * 4 + tk * tn * 2 + 2 * tm * tn * osz
            + (2 * tm * tn * 4 if acc is not None else 0) + 2 * tm * tn * 4 + 4 * MIB)
    return pl.pallas_call(
        functools.partial(_mm_body, n_a=len(a_list), n_valid=n_valid, has_acc=acc is not None, w_is_nk=w_is_nk),
        out_shape=jax.ShapeDtypeStruct((m_blocks * tm, n_blocks * tn), out_dtype),
        grid=(n_blocks, m_blocks),
        in_specs=in_specs,
        out_specs=out_spec,
        scratch_shapes=[pltpu.VMEM((tn, tk) if w_is_nk else (tk, tn), BF16)],
        compiler_params=_cparams(2, vmem),
        name="matmul",
    )(*args)


FFN_LAG = 2
FFN_RING = FFN_LAG + 1


def _ffn_up_body(a_ref, wg_ref, wv_ref, cwg_ref, cwv_ref, cbg_ref, cbv_ref, o_ref, wg_bf, wv_bf, ug_ref, uv_ref,
                 *, tm, tiles, total, n_ctx_rows, ctx_len, seq_len):
    s = pl.program_id(0)
    m = jnp.minimum(s, total - 1) % tiles
    m_out = jnp.maximum(s - FFN_LAG, 0) % tiles

    @pl.when(s == 0)
    def _():
        ug_ref[...] = jnp.zeros_like(ug_ref)
        uv_ref[...] = jnp.zeros_like(uv_ref)

    @pl.when(m == 0)
    def _():
        wg_bf[...] = wg_ref[...].astype(BF16)
        wv_bf[...] = wv_ref[...].astype(BF16)

    slot_new = s % FFN_RING
    slot_mid = (s + 1) % FFN_RING
    slot_next = (s + 2) % FFN_RING
    k_tiles = a_ref.shape[1] // MXU_DEPTH
    chunk = tm // k_tiles
    unit = min(chunk, ctx_len)
    row8 = lax.broadcasted_iota(jnp.int32, (8, o_ref.shape[1]), 0)

    def starts_sequence(g):
        return (g <= n_ctx_rows) | (((g - n_ctx_rows) & (seq_len - 1)) == 0)

    def conv(u_ref, cw_ref, cb_ref, r0):
        r1 = r0 + unit
        above = u_ref[slot_new, tm - 1:tm, :] if r0 == 0 else u_ref[slot_mid, r0 - 1:r0, :]
        below = u_ref[slot_next, 0:1, :] if r1 == tm else u_ref[slot_mid, r1:r1 + 1, :]
        if r0 % ctx_len == 0:
            above = jnp.where(starts_sequence(m_out * tm + r0), 0.0, above)
        if r1 % ctx_len == 0:
            below = jnp.where(starts_sequence(m_out * tm + r1), 0.0, below)
        u = u_ref[slot_mid, r0:r1, :]
        down, up = pltpu.roll(u, 1, 0), pltpu.roll(u, unit - 1, 0)
        u_prev = jnp.concatenate([jnp.where(row8 == 0, above, down[0:8]), down[8:]], axis=0)
        u_next = jnp.concatenate([up[:unit - 8], jnp.where(row8 == 7, below, up[unit - 8:])], axis=0)
        cw = cw_ref[...]
        return cb_ref[...] + u_prev * cw[0:1] + u * cw[1:2] + u_next * cw[2:3]

    sixteen = jnp.uint32(16)
    zeros = []
    for j in range(k_tiles):
        seen = None
        for r0 in range(j * chunk, (j + 1) * chunk, unit):
            gate = conv(ug_ref, cwg_ref, cbg_ref, r0)
            val = conv(uv_ref, cwv_ref, cbv_ref, r0)
            out = gate * _sigmoid(gate) * val
            o_ref[r0:r0 + unit, :] = out.astype(o_ref.dtype)
            bits = lax.bitcast_convert_type(out, jnp.uint32)
            for r in range(0, unit, 8):
                for c in range(0, bits.shape[1], LANE):
                    blk = bits[r:r + 8, c:c + LANE]
                    seen = blk if seen is None else seen | blk
        zeros.append(pltpu.bitcast(lax.shift_right_logical(lax.shift_right_logical(seen, sixteen), sixteen), BF16))

    def paced(w):
        parts = [w[0:MXU_DEPTH]]
        zr = zeros[0].shape[0]
        for k in range(1, k_tiles):
            blk = w[k * MXU_DEPTH:(k + 1) * MXU_DEPTH]
            head = jnp.concatenate([blk[0:zr, :LANE] + zeros[k - 1], blk[0:zr, LANE:]], axis=1)
            parts.append(jnp.concatenate([head, blk[zr:]], axis=0))
        return jnp.concatenate(parts, axis=0)

    ug_ref[slot_new] = jnp.dot(a_ref[...], paced(wg_bf[...]), preferred_element_type=F32)
    uv_ref[slot_new] = jnp.dot(a_ref[...], paced(wv_bf[...]), preferred_element_type=F32)


def ffn_up(h, w_up, conv_w, conv_b, *, layer, tm, n_ctx_rows, ctx_len, seq_len, tn=256):
    m_rows, k = h.shape
    tiles = m_rows // tm
    d_ff = w_up.shape[2] // 2
    nb = d_ff // tn
    total = nb * tiles
    assert tm % ctx_len == 0 and seq_len % ctx_len == 0 and n_ctx_rows % ctx_len == 0 and tn % LANE == 0
    chunk = tm // (k // MXU_DEPTH)
    assert k % MXU_DEPTH == 0 and chunk % 8 == 0 and (chunk % ctx_len == 0 or ctx_len % chunk == 0)
    depth = w_up.shape[0]
    cb = conv_b.reshape(depth, 1, 2 * d_ff)

    def cur(s):
        sc = jnp.minimum(s, total - 1)
        return sc // tiles, sc % tiles

    def out(s):
        so = jnp.maximum(s - FFN_LAG, 0)
        return so // tiles, so % tiles

    vmem = (2 * tm * k * 2 + 2 * 2 * k * tn * 4 + 2 * k * tn * 2 + 2 * FFN_RING * tm * tn * 4
            + 2 * tm * tn * 2 + 12 * tm * tn * 4 + 4 * MIB)
    return pl.pallas_call(
        functools.partial(_ffn_up_body, tm=tm, tiles=tiles, total=total, n_ctx_rows=n_ctx_rows,
                          ctx_len=ctx_len, seq_len=seq_len),
        out_shape=jax.ShapeDtypeStruct((m_rows, d_ff), BF16),
        grid=(total + FFN_LAG,),
        in_specs=[
            pl.BlockSpec((tm, k), lambda s: (cur(s)[1], 0)),
            pl.BlockSpec((None, k, tn), lambda s: (layer, 0, cur(s)[0])),
            pl.BlockSpec((None, k, tn), lambda s: (layer, 0, cur(s)[0] + nb)),
            pl.BlockSpec((None, 3, tn), lambda s: (layer, 0, out(s)[0])),
            pl.BlockSpec((None, 3, tn), lambda s: (layer, 0, out(s)[0] + nb)),
            pl.BlockSpec((None, 1, tn), lambda s: (layer, 0, out(s)[0])),
            pl.BlockSpec((None, 1, tn), lambda s: (layer, 0, out(s)[0] + nb)),
        ],
        out_specs=pl.BlockSpec((tm, tn), lambda s: (out(s)[1], out(s)[0])),
        scratch_shapes=[pltpu.VMEM((k, tn), BF16), pltpu.VMEM((k, tn), BF16),
                        pltpu.VMEM((FFN_RING, tm, tn), F32), pltpu.VMEM((FFN_RING, tm, tn), F32)],
        compiler_params=_cparams(1, vmem),
        name="ffn_up",
    )(h, w_up, w_up, conv_w, conv_w, cb, cb)


def _rope(x, cos, sin_signed):
    lane = lax.broadcasted_iota(jnp.int32, x.shape, 1)
    q = ATTN_HEAD_DIM // 4
    partner = jnp.where((lane & (2 * q - 1)) < q, pltpu.roll(x, ATTN_HEAD_DIM - q, 1), pltpu.roll(x, q, 1))
    return x * cos + partner * sin_signed


def _sink_attention(q4, kcat, vcat, sink_col, mask):
    s = lax.dot_general(q4, kcat, (((1,), (1,)), ((), ())), preferred_element_type=F32)
    s = s * (ATTN_HEAD_DIM ** -0.5)
    if mask is not None:
        s = jnp.where(mask, s, -jnp.inf)
    m = jnp.maximum(jnp.max(s, axis=-1, keepdims=True), sink_col)
    e = jnp.exp(s - m)
    p = e / (jnp.sum(e, axis=-1, keepdims=True) + jnp.exp(sink_col - m))
    return jnp.dot(p.astype(BF16), vcat, preferred_element_type=F32)


def _sink_column(sink_ref, j, rows):
    return jnp.concatenate(
        [jnp.full((rows, 1), sink_ref[j * ATTN_GROUP + g], F32) for g in range(ATTN_GROUP)], axis=0)


def _win_attn_body(sink_ref, q_ref, kp_ref, kc_ref, kn_ref, vp_ref, vc_ref, vn_ref, kx_ref, vx_ref,
                   cos_ref, sin_ref, prev_ref, o_ref, *, nb, n_kv):
    del prev_ref
    n = pl.program_id(1)
    blk, dh = ATTN_BLOCK, ATTN_HEAD_DIM
    qw = ATTN_GROUP * dh

    def table(ref, i):
        return ref[pl.ds(pl.multiple_of(i * blk, blk), blk), :]

    n_prev, n_next = jnp.maximum(n - 1, 0), jnp.minimum(n + 1, nb - 1)
    cos_q, sin_q = table(cos_ref, n), table(sin_ref, n)
    cos_p, sin_p = table(cos_ref, n_prev), table(sin_ref, n_prev)
    cos_n, sin_n = table(cos_ref, n_next), table(sin_ref, n_next)
    n_keys = 3 * blk + kx_ref.shape[0]
    qi = lax.broadcasted_iota(jnp.int32, (ATTN_GROUP * blk, n_keys), 0) & (blk - 1)
    kj = lax.broadcasted_iota(jnp.int32, (ATTN_GROUP * blk, n_keys), 1)
    lo = jnp.where(n > 0, 0, blk)
    hi = jnp.where(n < nb - 1, 3 * blk, 2 * blk)
    band = (kj >= qi) & (kj <= qi + 2 * blk) & (kj >= lo) & (kj < hi)
    mask = band | (kj >= 3 * blk)

    for j in range(n_kv):
        ksl = slice(j * dh, (j + 1) * dh)
        q = q_ref[:, j * qw:(j + 1) * qw].astype(F32)
        q4 = jnp.concatenate(
            [_rope(q[:, g * dh:(g + 1) * dh], cos_q, sin_q) for g in range(ATTN_GROUP)], axis=0).astype(BF16)
        kcat = jnp.concatenate([
            _rope(kp_ref[:, ksl].astype(F32), cos_p, sin_p).astype(BF16),
            _rope(kc_ref[:, ksl].astype(F32), cos_q, sin_q).astype(BF16),
            _rope(kn_ref[:, ksl].astype(F32), cos_n, sin_n).astype(BF16),
            kx_ref[:, ksl]], axis=0)
        vcat = jnp.concatenate([vp_ref[:, ksl], vc_ref[:, ksl], vn_ref[:, ksl], vx_ref[:, ksl]], axis=0)
        o = _sink_attention(q4, kcat, vcat, _sink_column(sink_ref, j, blk), mask)
        for g in range(ATTN_GROUP):
            c0 = j * qw + g * dh
            o_ref[:, c0:c0 + dh] = o[g * blk:(g + 1) * blk].astype(o_ref.dtype)


def window_attention(p, sink, cos, sin_signed, prev, *, batch, seq, ctx_len, q_col, k_col, v_col):
    blk, dh = ATTN_BLOCK, ATTN_HEAD_DIM
    nb = seq // blk
    n_kv = (k_col - q_col) // (ATTN_GROUP * dh)
    lat0 = batch * ctx_len // blk
    qw = n_kv * ATTN_GROUP * dh
    kw = n_kv * dh

    def kv_spec(col, shift):
        return pl.BlockSpec((blk, kw), lambda b, n: (lat0 + b * nb + jnp.clip(n + shift, 0, nb - 1), col // kw))

    in_specs = [
        pl.BlockSpec(memory_space=pltpu.SMEM),
        pl.BlockSpec((blk, qw), lambda b, n: (lat0 + b * nb + n, q_col // qw)),
        kv_spec(k_col, -1), kv_spec(k_col, 0), kv_spec(k_col, 1),
        kv_spec(v_col, -1), kv_spec(v_col, 0), kv_spec(v_col, 1),
        pl.BlockSpec((ctx_len, kw), lambda b, n: (b, k_col // kw)),
        pl.BlockSpec((ctx_len, kw), lambda b, n: (b, v_col // kw)),
        pl.BlockSpec((seq, dh), lambda b, n: (0, 0)),
        pl.BlockSpec((seq, dh), lambda b, n: (0, 0)),
        pl.BlockSpec(memory_space=pl.ANY),
    ]
    return pl.pallas_call(
        functools.partial(_win_attn_body, nb=nb, n_kv=n_kv),
        out_shape=jax.ShapeDtypeStruct(prev.shape, prev.dtype),
        grid=(batch, nb),
        in_specs=in_specs,
        out_specs=pl.BlockSpec((blk, qw), lambda b, n: (lat0 + b * nb + n, 0)),
        input_output_aliases={12: 0},
        compiler_params=_cparams(2, 32 * MIB),
        name="window_attention",
    )(sink, p, p, p, p, p, p, p, p, p, cos, sin_signed, prev)


def _ctx_attn_body(sink_ref, q_ref, k_ref, v_ref, o_ref, *, n_kv, batch):
    dh = ATTN_HEAD_DIM
    qw = ATTN_GROUP * dh
    rows = q_ref.shape[0]

    @pl.when(pl.program_id(0) < batch)
    def _():
        for j in range(n_kv):
            ksl = slice(j * dh, (j + 1) * dh)
            q4 = jnp.concatenate(
                [q_ref[:, j * qw + g * dh:j * qw + (g + 1) * dh] for g in range(ATTN_GROUP)], axis=0)
            o = _sink_attention(q4, k_ref[:, ksl], v_ref[:, ksl], _sink_column(sink_ref, j, rows), None)
            for g in range(ATTN_GROUP):
                c0 = j * qw + g * dh
                o_ref[:, c0:c0 + dh] = o[g * rows:(g + 1) * rows].astype(o_ref.dtype)

    @pl.when(pl.program_id(0) >= batch)
    def _():
        o_ref[...] = jnp.zeros_like(o_ref)


def context_attention(p, sink, *, batch, ctx_len, q_col, k_col, v_col):
    dh = ATTN_HEAD_DIM
    n_kv = (k_col - q_col) // (ATTN_GROUP * dh)
    qw = n_kv * ATTN_GROUP * dh
    kw = n_kv * dh
    last = batch - 1
    return pl.pallas_call(
        functools.partial(_ctx_attn_body, n_kv=n_kv, batch=batch),
        out_shape=jax.ShapeDtypeStruct((p.shape[0], qw), BF16),
        grid=(p.shape[0] // ctx_len,),
        in_specs=[
            pl.BlockSpec(memory_space=pltpu.SMEM),
            pl.BlockSpec((ctx_len, qw), lambda b: (jnp.minimum(b, last), q_col // qw)),
            pl.BlockSpec((ctx_len, kw), lambda b: (jnp.minimum(b, last), k_col // kw)),
            pl.BlockSpec((ctx_len, kw), lambda b: (jnp.minimum(b, last), v_col // kw)),
        ],
        out_specs=pl.BlockSpec((ctx_len, qw), lambda b: (b, 0)),
        compiler_params=_cparams(1, 32 * MIB),
        name="context_attention",
    )(sink, p, p, p)


def _scan_row_block(c, b, batch, n_ctx_blocks, n_lat_blocks, reverse):
    is_ctx = c < n_ctx_blocks
    i = jnp.where(is_ctx, c, c - n_ctx_blocks)
    if reverse:
        i = jnp.where(is_ctx, n_ctx_blocks - 1, n_lat_blocks - 1) - i
    return jnp.where(is_ctx, b * n_ctx_blocks + i, batch * n_ctx_blocks + b * n_lat_blocks + i)


def _head_finish(h_own, h_other, gate, gain, silu_gate):
    h = h_own + h_other.astype(F32)
    hn = h * lax.rsqrt(jnp.mean(h * h, axis=-1, keepdims=True) + NORM_EPS)
    g = gate.astype(F32)
    return hn * gain * (g * _sigmoid(g) if silu_gate else _sigmoid(g))


def _mlstm_body(q_ref, k_ref, v0_ref, v1_ref, g_ref, gt_ref, gb_row_ref, gb_col_ref, *rest, reverse, finish):
    if finish:
        other_ref, og0_ref, og1_ref, gain_ref, o_ref, cn_ref, m_ref = rest
    else:
        o_ref, cn_ref, m_ref = rest
    c = pl.program_id(1)
    ch, dk, dv, nh = MLSTM_CHUNK, MLSTM_QK, MLSTM_V, MLSTM_HEADS

    @pl.when(c == 0)
    def _():
        cn_ref[...] = jnp.zeros_like(cn_ref)
        m_ref[...] = jnp.zeros_like(m_ref)

    t = lax.broadcasted_iota(jnp.int32, (ch, ch), 0)
    s = lax.broadcasted_iota(jnp.int32, (ch, ch), 1)
    valid = (s >= t) if reverse else (s <= t)
    tri = jnp.where(valid, 1.0, 0.0).astype(BF16)
    last_t = 0 if reverse else ch - 1
    col_i = 2 * nh if reverse else 0
    col_f = 3 * nh if reverse else nh

    g_col = g_ref[...] + gb_row_ref[...]
    g_row = gt_ref[...] + gb_col_ref[...]
    lf_col, lf_row = _log_sigmoid(g_col), _log_sigmoid(g_row)
    bc_col_all = sum(jnp.dot(tri, part, preferred_element_type=F32) for part in _split3(lf_col))
    bc_row_all = sum(lax.dot_general(part, tri, (((1,), (1,)), ((), ())), preferred_element_type=F32)
                     for part in _split3(lf_row))
    ones_col = jnp.where(lax.broadcasted_iota(jnp.int32, (ch, LANE), 1) == 0, 1.0, 0.0)

    for h in range(nh):
        ig_col = g_col[:, col_i + h:col_i + h + 1]
        ig_row = g_row[col_i + h:col_i + h + 1, :]
        bc_col = bc_col_all[:, col_f + h:col_f + h + 1]
        bc_row = bc_row_all[col_f + h:col_f + h + 1, :]
        m_st = m_ref[h:h + 1, 0:1]
        q = q_ref[:, h * dk:(h + 1) * dk]
        k = k_ref[:, h * dk:(h + 1) * dk] * (dk ** -0.5)
        v_ref = v0_ref if h < nh // 2 else v1_ref
        hv = h % (nh // 2)
        v = v_ref[:, hv * dv:(hv + 1) * dv]

        log_d = jnp.where(valid, bc_col - bc_row + ig_row, -jnp.inf)
        inter = bc_col + m_st
        m_t = jnp.maximum(inter, jnp.max(log_d, axis=-1, keepdims=True))
        sc = lax.dot_general(q, k, (((1,), (1,)), ((), ())), preferred_element_type=F32) * jnp.exp(log_d - m_t)
        w_inter = jnp.exp(inter - m_t)
        state = cn_ref[h]
        q_state = jnp.dot(q, state.astype(BF16), preferred_element_type=F32)
        num = jnp.dot(sc.astype(BF16), v, preferred_element_type=F32) + w_inter * q_state[:, :dv]
        den = jnp.sum(sc, axis=-1, keepdims=True) + w_inter * q_state[:, dv:dv + 1]
        h_t = num * (1.0 / jnp.maximum(jnp.abs(den), jnp.exp(-m_t)))
        if finish:
            og_ref = og0_ref if h < nh // 2 else og1_ref
            h_t = _head_finish(h_t, other_ref[:, h * dv:(h + 1) * dv], og_ref[:, hv * dv:(hv + 1) * dv],
                               gain_ref[:, h * dv:(h + 1) * dv], False)
        o_ref[:, h * dv:(h + 1) * dv] = h_t.astype(o_ref.dtype)

        b_end = bc_col[last_t:last_t + 1, :]
        m_new = m_t[last_t:last_t + 1, :]
        wk = jnp.exp(b_end - bc_col + ig_col - m_new)
        decay = jnp.exp(b_end + m_st - m_new)
        v_ext = jnp.concatenate([v.astype(F32), ones_col], axis=1) * wk
        cn_ref[h] = decay * state + lax.dot_general(
            k, v_ext.astype(BF16), (((0,), (0,)), ((), ())), preferred_element_type=F32)
        m_ref[h:h + 1, :] = jnp.broadcast_to(m_new, (1, LANE))


def mlstm(p, gates, gate_b, *, reverse, batch, seq, ctx_len, q_col, k_col, v_col,
          other=None, out_gate_col=None, gain=None):
    ch, nh, dk, dv = MLSTM_CHUNK, MLSTM_HEADS, MLSTM_QK, MLSTM_V
    ncc, nlc = ctx_len // ch, seq // ch
    rows = p.shape[0]
    blk = functools.partial(_scan_row_block, batch=batch, n_ctx_blocks=ncc, n_lat_blocks=nlc, reverse=reverse)
    gb = jnp.zeros((LANE,), F32).at[:4 * nh].set(gate_b.astype(F32))
    qw, vw = nh * dk, nh * dv // 2
    finish = other is not None
    in_specs = [
        pl.BlockSpec((ch, qw), lambda b, c: (blk(c, b), q_col // qw)),
        pl.BlockSpec((ch, qw), lambda b, c: (blk(c, b), k_col // qw)),
        pl.BlockSpec((ch, vw), lambda b, c: (blk(c, b), v_col // vw)),
        pl.BlockSpec((ch, vw), lambda b, c: (blk(c, b), v_col // vw + 1)),
        pl.BlockSpec((ch, LANE), lambda b, c: (blk(c, b), 0)),
        pl.BlockSpec((LANE, ch), lambda b, c: (0, blk(c, b))),
        pl.BlockSpec((1, LANE), lambda b, c: (0, 0)),
        pl.BlockSpec((LANE, 1), lambda b, c: (0, 0)),
    ]
    args = [p, p, p, p, gates, gates.T, gb.reshape(1, LANE), gb.reshape(LANE, 1)]
    if finish:
        in_specs += [
            pl.BlockSpec((ch, nh * dv), lambda b, c: (blk(c, b), 0)),
            pl.BlockSpec((ch, vw), lambda b, c: (blk(c, b), out_gate_col // vw)),
            pl.BlockSpec((ch, vw), lambda b, c: (blk(c, b), out_gate_col // vw + 1)),
            pl.BlockSpec((1, nh * dv), lambda b, c: (0, 0)),
        ]
        args += [other, p, p, gain.reshape(1, nh * dv).astype(F32)]
    return pl.pallas_call(
        functools.partial(_mlstm_body, reverse=reverse, finish=finish),
        out_shape=jax.ShapeDtypeStruct((rows, nh * dv), BF16),
        grid=(batch, ncc + nlc),
        in_specs=in_specs,
        out_specs=pl.BlockSpec((ch, nh * dv), lambda b, c: (blk(c, b), 0)),
        scratch_shapes=[pltpu.VMEM((nh, dk, dv + LANE), F32), pltpu.VMEM((8, LANE), F32)],
        compiler_params=_cparams(2, 32 * MIB),
        name="mlstm",
    )(*args)


GLA_BLOCK = 2 * GLA_CHUNK


def _gla_body(q_ref, k_ref, v_ref, lr_ref, w2_ref, gb_ref, *rest, reverse, finish):
    if finish:
        other_ref, og_ref, gain_ref, o_ref, st_ref = rest
    else:
        o_ref, st_ref = rest
    c = pl.program_id(1)
    ch, rows, dk, dv, nh = GLA_CHUNK, GLA_BLOCK, GLA_QK, GLA_V, GLA_HEADS

    @pl.when(c == 0)
    def _():
        st_ref[...] = jnp.zeros_like(st_ref)

    t = lax.broadcasted_iota(jnp.int32, (rows, rows), 0)
    s = lax.broadcasted_iota(jnp.int32, (rows, rows), 1)
    same_chunk = (t ^ s) < ch
    intra = same_chunk & ((s >= t) if reverse else (s <= t))
    cross = ((t < ch) & (s >= ch)) if reverse else ((t >= ch) & (s < ch))
    tri = jnp.where(intra, 1.0, 0.0).astype(BF16)
    row = lax.broadcasted_iota(jnp.int32, (rows, 1), 0)
    in_a = (row >= ch) if reverse else (row < ch)

    pre = jnp.dot(lr_ref[...].astype(BF16), w2_ref[...].astype(BF16), preferred_element_type=F32)
    lg = _log_sigmoid(pre + gb_ref[...]) / GLA_GATE_TEMP
    gcum = sum(jnp.dot(tri, part, preferred_element_type=F32) for part in _split3(lg))
    end_a = gcum[ch:ch + 1, :] if reverse else gcum[ch - 1:ch, :]
    end_b = gcum[0:1, :] if reverse else gcum[rows - 1:rows, :]
    d_a, d_b = jnp.exp(end_a), jnp.exp(end_b)
    e_pos, e_neg = jnp.exp(gcum), jnp.exp(-gcum)
    e_end = jnp.exp(jnp.where(in_a, end_a, end_b) - gcum)
    e_pos_state = e_pos * jnp.where(in_a, 1.0, d_a)
    e_end_state = e_end * jnp.where(in_a, d_b, 1.0)
    decay = d_a * d_b

    for h in range(nh):
        sl = slice(h * dk, (h + 1) * dk)
        q = q_ref[:, sl].astype(F32) * (dk ** -0.5)
        k = k_ref[:, sl].astype(F32)
        v = v_ref[:, h * dv:(h + 1) * dv]
        q_dec = (q * e_pos[:, sl]).astype(BF16)
        q_state = (q * e_pos_state[:, sl]).astype(BF16)
        keys = jnp.concatenate([(k * e_neg[:, sl]).astype(BF16), (k * e_end[:, sl]).astype(BF16)], axis=0)
        k_state = (k * e_end_state[:, sl]).astype(BF16)
        sc = lax.dot_general(q_dec, keys, (((1,), (1,)), ((), ())), preferred_element_type=F32)
        att = jnp.where(intra, sc[:, :rows], jnp.where(cross, sc[:, rows:], 0.0)).astype(BF16)
        st = st_ref[h]
        o = (jnp.dot(att, v, preferred_element_type=F32)
             + lax.dot_general(q_state, st.astype(BF16), (((1,), (1,)), ((), ())), preferred_element_type=F32))
        hsl = slice(h * dv, (h + 1) * dv)
        if finish:
            o = _head_finish(o, other_ref[:, hsl], og_ref[:, hsl], gain_ref[:, hsl], True)
        o_ref[:, hsl] = o.astype(o_ref.dtype)
        st_ref[h] = decay[:, sl] * st + lax.dot_general(
            v, k_state, (((0,), (0,)), ((), ())), preferred_element_type=F32)


def gla(p, lowrank, gate_w2, gate_b, *, reverse, batch, seq, ctx_len, q_col, k_col, v_col,
        other=None, out_gate_col=None, gain=None):
    rows, nh, dk, dv = GLA_BLOCK, GLA_HEADS, GLA_QK, GLA_V
    ncb, nlb = ctx_len // rows, seq // rows
    rank = GLA_GATE_RANK
    d = 1 if reverse else 0
    blk = functools.partial(_scan_row_block, batch=batch, n_ctx_blocks=ncb, n_lat_blocks=nlb, reverse=reverse)

    def lat_blk(c, b):
        return blk(jnp.maximum(c, ncb), b)

    def out_blk(c, b):
        return lat_blk(c, b) - batch * ncb

    w2 = jnp.zeros((LANE, nh * dk), F32).at[d * rank:(d + 1) * rank].set(gate_w2[d])
    finish = other is not None
    in_specs = [
        pl.BlockSpec((rows, nh * dk), lambda b, c: (blk(c, b), q_col // (nh * dk))),
        pl.BlockSpec((rows, nh * dk), lambda b, c: (blk(c, b), k_col // (nh * dk))),
        pl.BlockSpec((rows, nh * dv), lambda b, c: (blk(c, b), v_col // (nh * dv))),
        pl.BlockSpec((rows, LANE), lambda b, c: (blk(c, b), 0)),
        pl.BlockSpec((LANE, nh * dk), lambda b, c: (0, 0)),
        pl.BlockSpec((1, nh * dk), lambda b, c: (0, 0)),
    ]
    args = [p, p, p, lowrank, w2, gate_b[d].reshape(1, nh * dk)]
    if finish:
        in_specs += [
            pl.BlockSpec((rows, nh * dv), lambda b, c: (out_blk(c, b), 0)),
            pl.BlockSpec((rows, nh * dv), lambda b, c: (lat_blk(c, b), out_gate_col // (nh * dv))),
            pl.BlockSpec((1, nh * dv), lambda b, c: (0, 0)),
        ]
        args += [other, p, gain.reshape(1, nh * dv).astype(F32)]
    return pl.pallas_call(
        functools.partial(_gla_body, reverse=reverse, finish=finish),
        out_shape=jax.ShapeDtypeStruct((batch * seq, nh * dv), BF16),
        grid=(batch, ncb + nlb),
        in_specs=in_specs,
        out_specs=pl.BlockSpec((rows, nh * dv), lambda b, c: (out_blk(c, b), 0)),
        scratch_shapes=[pltpu.VMEM((nh, dv, dk), F32)],
        compiler_params=_cparams(2, 44 * MIB),
        name="gla",
    )(*args)


def _rope_tables(seq):
    rows = seq // GRID_W
    row = jnp.repeat(jnp.arange(rows), GRID_W).astype(F32)
    col = jnp.tile(jnp.arange(GRID_W), rows).astype(F32)
    n_freq = ATTN_HEAD_DIM // 4
    inv = ROPE_THETA ** (-jnp.arange(n_freq, dtype=F32) / n_freq)
    ang_r, ang_c = row[:, None] * inv, col[:, None] * inv
    cos = jnp.concatenate([jnp.cos(ang_r), jnp.cos(ang_r), jnp.cos(ang_c), jnp.cos(ang_c)], axis=-1)
    sin = jnp.concatenate([-jnp.sin(ang_r), jnp.sin(ang_r), -jnp.sin(ang_c), jnp.sin(ang_c)], axis=-1)
    return cos, sin


def _conv_ffn(h, w_up, conv_w, conv_b, w_down, *, layer, n_ctx_rows, ctx_len, seq_len, tm):
    g = ffn_up(h, w_up, conv_w, conv_b, layer=layer, tm=tm, n_ctx_rows=n_ctx_rows, ctx_len=ctx_len,
               seq_len=seq_len)
    m, d_ff = g.shape
    d = w_down.shape[2]
    half = d_ff // 2
    kw = dict(layer=layer, m_blocks=m // 512, n_blocks=d // 512, tm=512, tn=512, tk=half)
    f = matmul(g, w_down, k_block=0, **kw)
    return matmul(g, w_down, k_block=1, acc=f, out_dtype=BF16, **kw)


def kernel(x, c, ctx, c_ctx, ada_w, ada_b, norm_w, ab_w_in, ab_gate_b, ab_sink, ab_head_norm, ab_w_out,
           gla_w_in, gla_gate_w2, gla_gate_b, gla_head_norm, gla_w_out,
           ffn_w_up, ffn_conv_w, ffn_conv_b, ffn_w_down):
    batch, seq, d = x.shape
    ctx_len = ctx.shape[1]
    n_ctx, n_lat = batch * ctx_len, batch * seq
    n_tok = n_ctx + n_lat
    depth = ada_w.shape[0]
    assert depth == 2 and ab_w_in.shape[0] == 1 and gla_w_in.shape[0] == 1
    tr = 256
    tm = math.gcd(math.gcd(n_ctx, seq), 1024)
    assert tm % 512 == 0 and ctx_len % tr == 0 and batch + 1 <= 8
    assert ctx_len & (ctx_len - 1) == 0 and seq & (seq - 1) == 0

    cvec = jnp.zeros((8, d), F32).at[:batch].set(c).at[batch].set(c_ctx)
    mods = ada_mods(cvec, ada_w, ada_b).reshape(depth, 8, 6, 1, d)

    def mod(i, j):
        return mods[i, :, j]

    ctx_tiles = n_ctx // tr
    lat_tiles_per_seq = seq // tr

    def group_all(t):
        return jnp.where(t < ctx_tiles, batch, (t - ctx_tiles) // lat_tiles_per_seq)

    def group_lat(t):
        return t // lat_tiles_per_seq

    ctx2, x2 = ctx.reshape(n_ctx, d), x.reshape(n_lat, d)

    nw = norm_w[0]
    (h,) = row_update(ctx2, group_all, n_tok, x_tail=x2, nwh=nw[0], shift=mod(0, 0), scale=mod(0, 1), tr=tr)
    n_kv = ab_sink.shape[1] // ATTN_GROUP
    aq, akv = n_kv * ATTN_GROUP * ATTN_HEAD_DIM, n_kv * ATTN_HEAD_DIM
    mqk, mv = MLSTM_HEADS * MLSTM_QK, MLSTM_HEADS * MLSTM_V
    cols = [0, aq, aq + akv, aq + 2 * akv, aq + 2 * akv + mqk, aq + 2 * akv + 2 * mqk,
            aq + 2 * akv + 2 * mqk + mv, aq + 2 * akv + 2 * mqk + 2 * mv]
    wide = cols[-1]
    w_in = jnp.swapaxes(ab_w_in, 1, 2)
    p = matmul(h, w_in, layer=0, m_blocks=n_tok // tm, n_blocks=wide // 512, tm=tm, tn=512,
               out_dtype=BF16, w_is_nk=True)
    gates = matmul(h, w_in, layer=0, m_blocks=n_tok // tm, n_blocks=1, tm=tm, tn=LANE,
                   n_off=wide // LANE, n_valid=ab_w_in.shape[2] - wide, w_is_nk=True)
    cos, sin = _rope_tables(seq)
    sink = ab_sink[0].astype(F32)
    attn = context_attention(p, sink, batch=batch, ctx_len=ctx_len, q_col=cols[0], k_col=cols[1], v_col=cols[2])
    attn = window_attention(p, sink, cos, sin, attn, batch=batch, seq=seq, ctx_len=ctx_len,
                            q_col=cols[0], k_col=cols[1], v_col=cols[2])
    scan = functools.partial(mlstm, p, gates, ab_gate_b[0], batch=batch, seq=seq, ctx_len=ctx_len,
                             q_col=cols[3], k_col=cols[4], v_col=cols[5])
    m_all = scan(reverse=True, other=scan(reverse=False), out_gate_col=cols[6], gain=ab_head_norm[0])
    y = matmul([attn, m_all], ab_w_out, layer=0, m_blocks=n_tok // tm, n_blocks=d // 512, tm=tm, tn=512,
               out_dtype=BF16)
    tok, h = row_update(ctx2, group_all, n_tok, x_tail=x2, y=y, gate=mod(0, 2), nwy=nw[1],
                        nwh=nw[2], shift=mod(0, 3), scale=mod(0, 4), tr=tr)
    f = _conv_ffn(h, ffn_w_up, ffn_conv_w, ffn_conv_b, ffn_w_down, layer=0, n_ctx_rows=n_ctx,
                  ctx_len=ctx_len, seq_len=seq, tm=tm)
    nw1 = norm_w[1]
    tok, h = row_update(tok, group_all, n_tok, y=f, gate=mod(0, 5), nwy=nw[3],
                        nwh=nw1[0], shift=mod(1, 0), scale=mod(1, 1), tr=tr)

    gq = GLA_HEADS * GLA_QK
    gv = GLA_HEADS * GLA_V
    wide = 2 * gq + 2 * gv
    w_in = jnp.swapaxes(gla_w_in, 1, 2)
    p = matmul(h, w_in, layer=0, m_blocks=n_tok // tm, n_blocks=wide // 512, tm=tm, tn=512,
               out_dtype=BF16, w_is_nk=True)
    lowrank = matmul(h, w_in, layer=0, m_blocks=n_tok // tm, n_blocks=1, tm=tm, tn=LANE,
                     n_off=wide // LANE, n_valid=gla_w_in.shape[2] - wide, w_is_nk=True)
    scan = functools.partial(gla, p, lowrank, gla_gate_w2[0], gla_gate_b[0], batch=batch, seq=seq,
                             ctx_len=ctx_len, q_col=0, k_col=gq, v_col=2 * gq)
    o_all = scan(reverse=True, other=scan(reverse=False), out_gate_col=2 * gq + gv, gain=gla_head_norm[0])
    y = matmul(o_all, gla_w_out, layer=0, m_blocks=n_lat // tm, n_blocks=d // 512, tm=tm, tn=512,
               out_dtype=BF16)
    xl, h = row_update(tok, group_lat, n_lat, x_row_off=n_ctx, y=y, gate=mod(1, 2), nwy=nw1[1],
                       nwh=nw1[2], shift=mod(1, 3), scale=mod(1, 4), tr=tr)
    f = _conv_ffn(h, ffn_w_up, ffn_conv_w, ffn_conv_b, ffn_w_down, layer=1, n_ctx_rows=0,
                  ctx_len=ctx_len, seq_len=seq, tm=tm)
    (out,) = row_update(xl, group_lat, n_lat, y=f, gate=mod(1, 5), nwy=nw1[3], tr=tr)
    return out.reshape(batch, seq, d)
```

```python
import functools
import math

import jax
import jax.numpy as jnp
from jax import lax
from jax.experimental import pallas as pl
from jax.experimental.pallas import tpu as pltpu

F32 = jnp.float32
BF16 = jnp.bfloat16

NORM_EPS = 1e-6
ROPE_THETA = 10000.0
GRID_W = 64
ATTN_HEAD_DIM = 128
ATTN_GROUP = 4
ATTN_BLOCK = 128
MLSTM_HEADS = 4
MLSTM_QK = 256
MLSTM_V = 512
MLSTM_CHUNK = 128
GLA_HEADS = 8
GLA_QK = 256
GLA_V = 512
GLA_CHUNK = 64
GLA_GATE_TEMP = 16.0
GLA_GATE_RANK = 16
LANE = 128
MXU_DEPTH = 256
MIB = 1024 * 1024
VMEM_CAP = 60 * MIB


def _cparams(n_axes, vmem_bytes, flags=None):
    return pltpu.CompilerParams(
        dimension_semantics=("arbitrary",) * n_axes,
        vmem_limit_bytes=int(min(VMEM_CAP, vmem_bytes)),
        flags=flags,
    )


def _sigmoid(x):
    return 1.0 / (1.0 + jnp.exp(-x))


def _log_sigmoid(x):
    return jnp.minimum(x, 0.0) - jnp.log(1.0 + jnp.exp(-jnp.abs(x)))


def _split3(x):
    hi = x.astype(BF16)
    r1 = x - hi.astype(F32)
    mid = r1.astype(BF16)
    lo = (r1 - mid.astype(F32)).astype(BF16)
    return hi, mid, lo


def _ada_body(s_ref, w_ref, b_ref, o_ref):
    s = s_ref[...]
    s = (s * _sigmoid(s)).astype(BF16)
    o_ref[...] = jnp.dot(s, w_ref[...].astype(BF16), preferred_element_type=F32) + b_ref[...]


def ada_mods(cvec, ada_w, ada_b, tn=1024):
    depth, d, n = ada_w.shape
    rows = cvec.shape[0]
    return pl.pallas_call(
        _ada_body,
        out_shape=jax.ShapeDtypeStruct((depth, rows, n), F32),
        grid=(depth, n // tn),
        in_specs=[
            pl.BlockSpec((rows, d), lambda i, j: (0, 0)),
            pl.BlockSpec((None, d, tn), lambda i, j: (i, 0, j)),
            pl.BlockSpec((None, 1, tn), lambda i, j: (i, 0, j)),
        ],
        out_specs=pl.BlockSpec((None, rows, tn), lambda i, j: (i, 0, j)),
        compiler_params=_cparams(2, 2 * d * tn * 4 + d * tn * 2 + 8 * MIB),
        name="ada_mods",
    )(cvec, ada_w, ada_b.reshape(depth, 1, n))


def _rms(x, w):
    return x * lax.rsqrt(jnp.mean(x * x, axis=-1, keepdims=True) + NORM_EPS) * w


def _row_body(*refs, has_y, has_h, split):
    it = iter(refs)
    x = next(it)[...]
    if split is not None:
        x = jnp.where(pl.program_id(0) < split, x, next(it)[...])
    if has_y:
        y = next(it)[...].astype(F32)
        gate = next(it)[...]
        nwy = next(it)[...]
    if has_h:
        nwh = next(it)[...]
        shift = next(it)[...]
        scale = next(it)[...]
    if has_y:
        x = x + gate * _rms(y, nwy)
        next(it)[...] = x
    if has_h:
        next(it)[...] = (_rms(x, nwh) * (1.0 + scale) + shift).astype(BF16)


def row_update(x, group_of_tile, n_rows, *, x_tail=None, x_row_off=0, y=None, gate=None, nwy=None,
               nwh=None, shift=None, scale=None, tr=256):
    d = x.shape[1]
    has_y, has_h = y is not None, nwh is not None
    off = x_row_off // tr
    row = pl.BlockSpec((tr, d), lambda t: (t, 0))
    vec = pl.BlockSpec((1, d), lambda t: (0, 0))
    tab = pl.BlockSpec((None, 1, d), lambda t: (group_of_tile(t), 0, 0))
    split = None
    if x_tail is None:
        args, specs = [x], [pl.BlockSpec((tr, d), lambda t: (t + off, 0))]
    else:
        split = x.shape[0] // tr
        args = [x, x_tail]
        specs = [pl.BlockSpec((tr, d), lambda t: (jnp.minimum(t, split - 1), 0)),
                 pl.BlockSpec((tr, d), lambda t: (jnp.maximum(t - split, 0), 0))]
    if has_y:
        args += [y, gate, nwy.reshape(1, d)]
        specs += [row, tab, vec]
    if has_h:
        args += [nwh.reshape(1, d), shift, scale]
        specs += [vec, tab, tab]
    out_shape, out_specs = [], []
    if has_y:
        out_shape.append(jax.ShapeDtypeStruct((n_rows, d), F32))
        out_specs.append(row)
    if has_h:
        out_shape.append(jax.ShapeDtypeStruct((n_rows, d), BF16))
        out_specs.append(row)
    return pl.pallas_call(
        functools.partial(_row_body, has_y=has_y, has_h=has_h, split=split),
        out_shape=out_shape,
        grid=(n_rows // tr,),
        in_specs=specs,
        out_specs=out_specs,
        compiler_params=_cparams(1, 2 * tr * d * 22 + 8 * MIB),
        name="row_update",
    )(*args)


def _mm_body(*refs, n_a, n_valid, has_acc, w_is_nk):
    a_refs = refs[:n_a]
    w_ref = refs[n_a]
    acc_ref = refs[n_a + 1] if has_acc else None
    o_ref, wbf_ref = refs[-2], refs[-1]

    @pl.when(pl.program_id(1) == 0)
    def _():
        w = w_ref[...]
        if n_valid is not None:
            col = lax.broadcasted_iota(jnp.int32, w.shape, 0 if w_is_nk else 1)
            w = jnp.where(col < n_valid, w, 0.0)
        wbf_ref[...] = w.astype(BF16)

    r, k0 = None, 0
    for a_ref in a_refs:
        k1 = k0 + a_ref.shape[1]
        if w_is_nk:
            part = lax.dot_general(a_ref[...], wbf_ref[:, k0:k1], (((1,), (1,)), ((), ())),
                                   preferred_element_type=F32)
        else:
            part = jnp.dot(a_ref[...], wbf_ref[k0:k1, :], preferred_element_type=F32)
        r = part if r is None else r + part
        k0 = k1
    if has_acc:
        r = r + acc_ref[...]
    o_ref[...] = r.astype(o_ref.dtype)


def matmul(a, w, *, layer, m_blocks, n_blocks, tm, tn, tk=None, k_block=0, a_row_off=0, n_off=0,
           n_valid=None, acc=None, out_dtype=F32, w_is_nk=False):
    a_list = list(a) if isinstance(a, (list, tuple)) else [a]
    k_total = sum(x.shape[1] for x in a_list)
    tk = k_total if tk is None else tk
    assert len(a_list) == 1 or tk == k_total
    roff = a_row_off // tm
    in_specs = [pl.BlockSpec((tm, tk if len(a_list) == 1 else x.shape[1]), lambda n, m: (m + roff, k_block))
                for x in a_list]
    if w_is_nk:
        in_specs.append(pl.BlockSpec((None, tn, tk), lambda n, m: (layer, n + n_off, k_block)))
    else:
        in_specs.append(pl.BlockSpec((None, tk, tn), lambda n, m: (layer, k_block, n + n_off)))
    args = a_list + [w]
    out_spec = pl.BlockSpec((tm, tn), lambda n, m: (m, n))
    if acc is not None:
        in_specs.append(out_spec)
        args.append(acc)
    osz = jnp.dtype(out_dtype).itemsize
    vmem = (2 * tm * tk * 2 + 2 * tk * tn * 4 + tk * tn * 2 + 2 * tm * tn * osz
            + (2 * tm * tn * 4 if acc is not None else 0) + 2 * tm * tn * 4 + 4 * MIB)
    return pl.pallas_call(
        functools.partial(_mm_body, n_a=len(a_list), n_valid=n_valid, has_acc=acc is not None, w_is_nk=w_is_nk),
        out_shape=jax.ShapeDtypeStruct((m_blocks * tm, n_blocks * tn), out_dtype),
        grid=(n_blocks, m_blocks),
        in_specs=in_specs,
        out_specs=out_spec,
        scratch_shapes=[pltpu.VMEM((tn, tk) if w_is_nk else (tk, tn), BF16)],
        compiler_params=_cparams(2, vmem),
        name="matmul",
    )(*args)


FFN_LAG = 2
FFN_RING = FFN_LAG + 1


def _ffn_up_body(a_ref, wg_ref, wv_ref, cwg_ref, cwv_ref, cbg_ref, cbv_ref, o_ref, wg_bf, wv_bf, ug_ref, uv_ref,
                 *, tm, tiles, total, n_ctx_rows, ctx_len, seq_len):
    s = pl.program_id(0)
    m = jnp.minimum(s, total - 1) % tiles
    m_out = jnp.maximum(s - FFN_LAG, 0) % tiles

    @pl.when(s == 0)
    def _():
        ug_ref[...] = jnp.zeros_like(ug_ref)
        uv_ref[...] = jnp.zeros_like(uv_ref)

    @pl.when(m == 0)
    def _():
        wg_bf[...] = wg_ref[...].astype(BF16)
        wv_bf[...] = wv_ref[...].astype(BF16)

    slot_new = s % FFN_RING
    slot_mid = (s + 1) % FFN_RING
    slot_next = (s + 2) % FFN_RING
    k_tiles = a_ref.shape[1] // MXU_DEPTH
    chunk = tm // k_tiles
    unit = min(chunk, ctx_len)
    row8 = lax.broadcasted_iota(jnp.int32, (8, o_ref.shape[1]), 0)

    def starts_sequence(g):
        return (g <= n_ctx_rows) | (((g - n_ctx_rows) & (seq_len - 1)) == 0)

    def conv(u_ref, cw_ref, cb_ref, r0):
        r1 = r0 + unit
        above = u_ref[slot_new, tm - 1:tm, :] if r0 == 0 else u_ref[slot_mid, r0 - 1:r0, :]
        below = u_ref[slot_next, 0:1, :] if r1 == tm else u_ref[slot_mid, r1:r1 + 1, :]
        if r0 % ctx_len == 0:
            above = jnp.where(starts_sequence(m_out * tm + r0), 0.0, above)
        if r1 % ctx_len == 0:
            below = jnp.where(starts_sequence(m_out * tm + r1), 0.0, below)
        u = u_ref[slot_mid, r0:r1, :]
        down, up = pltpu.roll(u, 1, 0), pltpu.roll(u, unit - 1, 0)
        u_prev = jnp.concatenate([jnp.where(row8 == 0, above, down[0:8]), down[8:]], axis=0)
        u_next = jnp.concatenate([up[:unit - 8], jnp.where(row8 == 7, below, up[unit - 8:])], axis=0)
        cw = cw_ref[...]
        return cb_ref[...] + u_prev * cw[0:1] + u * cw[1:2] + u_next * cw[2:3]

    sixteen = jnp.uint32(16)
    zeros = []
    for j in range(k_tiles):
        seen = None
        for r0 in range(j * chunk, (j + 1) * chunk, unit):
            gate = conv(ug_ref, cwg_ref, cbg_ref, r0)
            val = conv(uv_ref, cwv_ref, cbv_ref, r0)
            out = gate * _sigmoid(gate) * val
            o_ref[r0:r0 + unit, :] = out.astype(o_ref.dtype)
            bits = lax.bitcast_convert_type(out, jnp.uint32)
            for r in range(0, unit, 8):
                for c in range(0, bits.shape[1], LANE):
                    blk = bits[r:r + 8, c:c + LANE]
                    seen = blk if seen is None else seen | blk
        zeros.append(pltpu.bitcast(lax.shift_right_logical(lax.shift_right_logical(seen, sixteen), sixteen), BF16))

    def paced(w):
        parts = [w[0:MXU_DEPTH]]
        zr = zeros[0].shape[0]
        for k in range(1, k_tiles):
            blk = w[k * MXU_DEPTH:(k + 1) * MXU_DEPTH]
            head = jnp.concatenate([blk[0:zr, :LANE] + zeros[k - 1], blk[0:zr, LANE:]], axis=1)
            parts.append(jnp.concatenate([head, blk[zr:]], axis=0))
        return jnp.concatenate(parts, axis=0)

    ug_ref[slot_new] = jnp.dot(a_ref[...], paced(wg_bf[...]), preferred_element_type=F32)
    uv_ref[slot_new] = jnp.dot(a_ref[...], paced(wv_bf[...]), preferred_element_type=F32)


def ffn_up(h, w_up, conv_w, conv_b, *, layer, tm, n_ctx_rows, ctx_len, seq_len, tn=256):
    m_rows, k = h.shape
    tiles = m_rows // tm
    d_ff = w_up.shape[2] // 2
    nb = d_ff // tn
    total = nb * tiles
    assert tm % ctx_len == 0 and seq_len % ctx_len == 0 and n_ctx_rows % ctx_len == 0 and tn % LANE == 0
    chunk = tm // (k // MXU_DEPTH)
    assert k % MXU_DEPTH == 0 and chunk % 8 == 0 and (chunk % ctx_len == 0 or ctx_len % chunk == 0)
    depth = w_up.shape[0]
    cb = conv_b.reshape(depth, 1, 2 * d_ff)

    def cur(s):
        sc = jnp.minimum(s, total - 1)
        return sc // tiles, sc % tiles

    def out(s):
        so = jnp.maximum(s - FFN_LAG, 0)
        return so // tiles, so % tiles

    vmem = (2 * tm * k * 2 + 2 * 2 * k * tn * 4 + 2 * k * tn * 2 + 2 * FFN_RING * tm * tn * 4
            + 2 * tm * tn * 2 + 12 * tm * tn * 4 + 4 * MIB)
    return pl.pallas_call(
        functools.partial(_ffn_up_body, tm=tm, tiles=tiles, total=total, n_ctx_rows=n_ctx_rows,
                          ctx_len=ctx_len, seq_len=seq_len),
        out_shape=jax.ShapeDtypeStruct((m_rows, d_ff), BF16),
        grid=(total + FFN_LAG,),
        in_specs=[
            pl.BlockSpec((tm, k), lambda s: (cur(s)[1], 0)),
            pl.BlockSpec((None, k, tn), lambda s: (layer, 0, cur(s)[0])),
            pl.BlockSpec((None, k, tn), lambda s: (layer, 0, cur(s)[0] + nb)),
            pl.BlockSpec((None, 3, tn), lambda s: (layer, 0, out(s)[0])),
            pl.BlockSpec((None, 3, tn), lambda s: (layer, 0, out(s)[0] + nb)),
            pl.BlockSpec((None, 1, tn), lambda s: (layer, 0, out(s)[0])),
            pl.BlockSpec((None, 1, tn), lambda s: (layer, 0, out(s)[0] + nb)),
        ],
        out_specs=pl.BlockSpec((tm, tn), lambda s: (out(s)[1], out(s)[0])),
        scratch_shapes=[pltpu.VMEM((k, tn), BF16), pltpu.VMEM((k, tn), BF16),
                        pltpu.VMEM((FFN_RING, tm, tn), F32), pltpu.VMEM((FFN_RING, tm, tn), F32)],
        compiler_params=_cparams(1, vmem),
        name="ffn_up",
    )(h, w_up, w_up, conv_w, conv_w, cb, cb)


def _rope(x, cos, sin_signed):
    lane = lax.broadcasted_iota(jnp.int32, x.shape, 1)
    q = ATTN_HEAD_DIM // 4
    partner = jnp.where((lane & (2 * q - 1)) < q, pltpu.roll(x, ATTN_HEAD_DIM - q, 1), pltpu.roll(x, q, 1))
    return x * cos + partner * sin_signed


def _sink_attention(q4, kcat, vcat, sink_col, bias=None, score_scale=None):
    s = lax.dot_general(q4, kcat, (((1,), (1,)), ((), ())), preferred_element_type=F32)
    if score_scale is not None:
        s = s * score_scale
    if bias is not None:
        s = s + bias
    m = jnp.maximum(jnp.max(s, axis=-1, keepdims=True), sink_col)
    e = jnp.exp(s - m)
    denom = jnp.sum(e, axis=-1, keepdims=True) + jnp.exp(sink_col - m)
    return jnp.dot(e.astype(BF16), vcat, preferred_element_type=F32) * (1.0 / denom)


def _sink_column(sink_ref, j, rows):
    return jnp.concatenate(
        [jnp.full((rows, 1), sink_ref[j * ATTN_GROUP + g], F32) for g in range(ATTN_GROUP)], axis=0)


def _win_attn_body(sink_ref, q_ref, kp_ref, kc_ref, kn_ref, vp_ref, vc_ref, vn_ref, kx_ref, vx_ref,
                   cos_ref, sin_ref, band_ref, prev_ref, o_ref, *, nb, n_kv):
    del prev_ref
    n = pl.program_id(1)
    blk, dh = ATTN_BLOCK, ATTN_HEAD_DIM
    qw = ATTN_GROUP * dh
    scale = dh ** -0.5

    def table(ref, i):
        return ref[pl.ds(pl.multiple_of(i * blk, blk), blk), :]

    n_prev, n_next = jnp.maximum(n - 1, 0), jnp.minimum(n + 1, nb - 1)
    cos_q, sin_q = table(cos_ref, n), table(sin_ref, n)
    cos_p, sin_p = table(cos_ref, n_prev), table(sin_ref, n_prev)
    cos_n, sin_n = table(cos_ref, n_next), table(sin_ref, n_next)
    n_keys = 3 * blk + kx_ref.shape[0]
    kj = lax.broadcasted_iota(jnp.int32, (1, n_keys), 1)
    lo = jnp.where(n > 0, 0, blk)
    hi = jnp.where(n < nb - 1, 3 * blk, 2 * blk)
    bias = band_ref[...] + jnp.where(((kj >= lo) & (kj < hi)) | (kj >= 3 * blk), 0.0, -jnp.inf)

    for j in range(n_kv):
        ksl = slice(j * dh, (j + 1) * dh)
        q = q_ref[:, j * qw:(j + 1) * qw].astype(F32)
        q4 = jnp.concatenate(
            [_rope(q[:, g * dh:(g + 1) * dh], cos_q, sin_q) * scale for g in range(ATTN_GROUP)],
            axis=0).astype(BF16)
        kcat = jnp.concatenate([
            _rope(kp_ref[:, ksl].astype(F32), cos_p, sin_p).astype(BF16),
            _rope(kc_ref[:, ksl].astype(F32), cos_q, sin_q).astype(BF16),
            _rope(kn_ref[:, ksl].astype(F32), cos_n, sin_n).astype(BF16),
            kx_ref[:, ksl]], axis=0)
        vcat = jnp.concatenate([vp_ref[:, ksl], vc_ref[:, ksl], vn_ref[:, ksl], vx_ref[:, ksl]], axis=0)
        o = _sink_attention(q4, kcat, vcat, _sink_column(sink_ref, j, blk), bias=bias)
        for g in range(ATTN_GROUP):
            c0 = j * qw + g * dh
            o_ref[:, c0:c0 + dh] = o[g * blk:(g + 1) * blk].astype(o_ref.dtype)


def window_attention(p, sink, cos, sin_signed, prev, *, batch, seq, ctx_len, q_col, k_col, v_col):
    blk, dh = ATTN_BLOCK, ATTN_HEAD_DIM
    nb = seq // blk
    n_kv = (k_col - q_col) // (ATTN_GROUP * dh)
    lat0 = batch * ctx_len // blk
    qw = n_kv * ATTN_GROUP * dh
    kw = n_kv * dh

    def kv_spec(col, shift):
        return pl.BlockSpec((blk, kw), lambda b, n: (lat0 + b * nb + jnp.clip(n + shift, 0, nb - 1), col // kw))

    rows, n_keys = ATTN_GROUP * blk, 3 * blk + ctx_len
    qi = jnp.arange(rows)[:, None] % blk
    kj = jnp.arange(n_keys)[None, :]
    band = jnp.where(((kj >= qi) & (kj <= qi + 2 * blk)) | (kj >= 3 * blk), 0.0, -jnp.inf).astype(F32)

    in_specs = [
        pl.BlockSpec(memory_space=pltpu.SMEM),
        pl.BlockSpec((blk, qw), lambda b, n: (lat0 + b * nb + n, q_col // qw)),
        kv_spec(k_col, -1), kv_spec(k_col, 0), kv_spec(k_col, 1),
        kv_spec(v_col, -1), kv_spec(v_col, 0), kv_spec(v_col, 1),
        pl.BlockSpec((ctx_len, kw), lambda b, n: (b, k_col // kw)),
        pl.BlockSpec((ctx_len, kw), lambda b, n: (b, v_col // kw)),
        pl.BlockSpec((seq, dh), lambda b, n: (0, 0)),
        pl.BlockSpec((seq, dh), lambda b, n: (0, 0)),
        pl.BlockSpec((rows, n_keys), lambda b, n: (0, 0)),
        pl.BlockSpec(memory_space=pl.ANY),
    ]
    return pl.pallas_call(
        functools.partial(_win_attn_body, nb=nb, n_kv=n_kv),
        out_shape=jax.ShapeDtypeStruct(prev.shape, prev.dtype),
        grid=(batch, nb),
        in_specs=in_specs,
        out_specs=pl.BlockSpec((blk, qw), lambda b, n: (lat0 + b * nb + n, 0)),
        input_output_aliases={13: 0},
        compiler_params=_cparams(2, 32 * MIB),
        name="window_attention",
    )(sink, p, p, p, p, p, p, p, p, p, cos, sin_signed, band, prev)


def _ctx_attn_body(sink_ref, q_ref, k_ref, v_ref, o_ref, *, n_kv, batch):
    dh = ATTN_HEAD_DIM
    qw = ATTN_GROUP * dh
    rows = q_ref.shape[0]

    @pl.when(pl.program_id(0) < batch)
    def _():
        for j in range(n_kv):
            ksl = slice(j * dh, (j + 1) * dh)
            q4 = jnp.concatenate(
                [q_ref[:, j * qw + g * dh:j * qw + (g + 1) * dh] for g in range(ATTN_GROUP)], axis=0)
            o = _sink_attention(q4, k_ref[:, ksl], v_ref[:, ksl], _sink_column(sink_ref, j, rows),
                                score_scale=dh ** -0.5)
            for g in range(ATTN_GROUP):
                c0 = j * qw + g * dh
                o_ref[:, c0:c0 + dh] = o[g * rows:(g + 1) * rows].astype(o_ref.dtype)

    @pl.when(pl.program_id(0) >= batch)
    def _():
        o_ref[...] = jnp.zeros_like(o_ref)


def context_attention(p, sink, *, batch, ctx_len, q_col, k_col, v_col):
    dh = ATTN_HEAD_DIM
    n_kv = (k_col - q_col) // (ATTN_GROUP * dh)
    qw = n_kv * ATTN_GROUP * dh
    kw = n_kv * dh
    last = batch - 1
    return pl.pallas_call(
        functools.partial(_ctx_attn_body, n_kv=n_kv, batch=batch),
        out_shape=jax.ShapeDtypeStruct((p.shape[0], qw), BF16),
        grid=(p.shape[0] // ctx_len,),
        in_specs=[
            pl.BlockSpec(memory_space=pltpu.SMEM),
            pl.BlockSpec((ctx_len, qw), lambda b: (jnp.minimum(b, last), q_col // qw)),
            pl.BlockSpec((ctx_len, kw), lambda b: (jnp.minimum(b, last), k_col // kw)),
            pl.BlockSpec((ctx_len, kw), lambda b: (jnp.minimum(b, last), v_col // kw)),
        ],
        out_specs=pl.BlockSpec((ctx_len, qw), lambda b: (b, 0)),
        compiler_params=_cparams(1, 32 * MIB),
        name="context_attention",
    )(sink, p, p, p)


def _scan_row_block(c, b, batch, n_ctx_blocks, n_lat_blocks, reverse):
    is_ctx = c < n_ctx_blocks
    i = jnp.where(is_ctx, c, c - n_ctx_blocks)
    if reverse:
        i = jnp.where(is_ctx, n_ctx_blocks - 1, n_lat_blocks - 1) - i
    return jnp.where(is_ctx, b * n_ctx_blocks + i, batch * n_ctx_blocks + b * n_lat_blocks + i)


def _head_finish(h_own, h_other, gate, gain, silu_gate):
    h = h_own + h_other.astype(F32)
    hn = h * lax.rsqrt(jnp.mean(h * h, axis=-1, keepdims=True) + NORM_EPS)
    g = gate.astype(F32)
    return hn * gain * (g * _sigmoid(g) if silu_gate else _sigmoid(g))


def _mlstm_body(q_ref, k_ref, v0_ref, v1_ref, g_ref, gt_ref, gb_row_ref, gb_col_ref, *rest, reverse, finish):
    if finish:
        other_ref, og0_ref, og1_ref, gain_ref, o_ref, cn_ref, m_ref = rest
    else:
        o_ref, cn_ref, m_ref = rest
    c = pl.program_id(1)
    ch, dk, dv, nh = MLSTM_CHUNK, MLSTM_QK, MLSTM_V, MLSTM_HEADS

    @pl.when(c == 0)
    def _():
        cn_ref[...] = jnp.zeros_like(cn_ref)
        m_ref[...] = jnp.zeros_like(m_ref)

    t = lax.broadcasted_iota(jnp.int32, (ch, ch), 0)
    s = lax.broadcasted_iota(jnp.int32, (ch, ch), 1)
    valid = (s >= t) if reverse else (s <= t)
    tri = jnp.where(valid, 1.0, 0.0).astype(BF16)
    last_t = 0 if reverse else ch - 1
    col_i = 2 * nh if reverse else 0
    col_f = 3 * nh if reverse else nh

    g_col = g_ref[...] + gb_row_ref[...]
    g_row = gt_ref[...] + gb_col_ref[...]
    lf_col, lf_row = _log_sigmoid(g_col), _log_sigmoid(g_row)
    bc_col_all = sum(jnp.dot(tri, part, preferred_element_type=F32) for part in _split3(lf_col))
    bc_row_all = sum(lax.dot_general(part, tri, (((1,), (1,)), ((), ())), preferred_element_type=F32)
                     for part in _split3(lf_row))
    ones_col = jnp.where(lax.broadcasted_iota(jnp.int32, (ch, LANE), 1) == 0, 1.0, 0.0)

    for h in range(nh):
        ig_col = g_col[:, col_i + h:col_i + h + 1]
        ig_row = g_row[col_i + h:col_i + h + 1, :]
        bc_col = bc_col_all[:, col_f + h:col_f + h + 1]
        bc_row = bc_row_all[col_f + h:col_f + h + 1, :]
        m_st = m_ref[h:h + 1, 0:1]
        q = q_ref[:, h * dk:(h + 1) * dk]
        k = k_ref[:, h * dk:(h + 1) * dk] * (dk ** -0.5)
        v_ref = v0_ref if h < nh // 2 else v1_ref
        hv = h % (nh // 2)
        v = v_ref[:, hv * dv:(hv + 1) * dv]

        log_d = jnp.where(valid, bc_col - bc_row + ig_row, -jnp.inf)
        inter = bc_col + m_st
        m_t = jnp.maximum(inter, jnp.max(log_d, axis=-1, keepdims=True))
        sc = lax.dot_general(q, k, (((1,), (1,)), ((), ())), preferred_element_type=F32) * jnp.exp(log_d - m_t)
        w_inter = jnp.exp(inter - m_t)
        state = cn_ref[h]
        q_state = jnp.dot(q, state.astype(BF16), preferred_element_type=F32)
        num = jnp.dot(sc.astype(BF16), v, preferred_element_type=F32) + w_inter * q_state[:, :dv]
        den = jnp.sum(sc, axis=-1, keepdims=True) + w_inter * q_state[:, dv:dv + 1]
        h_t = num * (1.0 / jnp.maximum(jnp.abs(den), jnp.exp(-m_t)))
        if finish:
            og_ref = og0_ref if h < nh // 2 else og1_ref
            h_t = _head_finish(h_t, other_ref[:, h * dv:(h + 1) * dv], og_ref[:, hv * dv:(hv + 1) * dv],
                               gain_ref[:, h * dv:(h + 1) * dv], False)
        o_ref[:, h * dv:(h + 1) * dv] = h_t.astype(o_ref.dtype)

        b_end = bc_col[last_t:last_t + 1, :]
        m_new = m_t[last_t:last_t + 1, :]
        wk = jnp.exp(b_end - bc_col + ig_col - m_new)
        decay = jnp.exp(b_end + m_st - m_new)
        v_ext = jnp.concatenate([v.astype(F32), ones_col], axis=1) * wk
        cn_ref[h] = decay * state + lax.dot_general(
            k, v_ext.astype(BF16), (((0,), (0,)), ((), ())), preferred_element_type=F32)
        m_ref[h:h + 1, :] = jnp.broadcast_to(m_new, (1, LANE))


def mlstm(p, gates, gate_b, *, reverse, batch, seq, ctx_len, q_col, k_col, v_col,
          other=None, out_gate_col=None, gain=None):
    ch, nh, dk, dv = MLSTM_CHUNK, MLSTM_HEADS, MLSTM_QK, MLSTM_V
    ncc, nlc = ctx_len // ch, seq // ch
    rows = p.shape[0]
    blk = functools.partial(_scan_row_block, batch=batch, n_ctx_blocks=ncc, n_lat_blocks=nlc, reverse=reverse)
    gb = jnp.zeros((LANE,), F32).at[:4 * nh].set(gate_b.astype(F32))
    qw, vw = nh * dk, nh * dv // 2
    finish = other is not None
    in_specs = [
        pl.BlockSpec((ch, qw), lambda b, c: (blk(c, b), q_col // qw)),
        pl.BlockSpec((ch, qw), lambda b, c: (blk(c, b), k_col // qw)),
        pl.BlockSpec((ch, vw), lambda b, c: (blk(c, b), v_col // vw)),
        pl.BlockSpec((ch, vw), lambda b, c: (blk(c, b), v_col // vw + 1)),
        pl.BlockSpec((ch, LANE), lambda b, c: (blk(c, b), 0)),
        pl.BlockSpec((LANE, ch), lambda b, c: (0, blk(c, b))),
        pl.BlockSpec((1, LANE), lambda b, c: (0, 0)),
        pl.BlockSpec((LANE, 1), lambda b, c: (0, 0)),
    ]
    args = [p, p, p, p, gates, gates.T, gb.reshape(1, LANE), gb.reshape(LANE, 1)]
    if finish:
        in_specs += [
            pl.BlockSpec((ch, nh * dv), lambda b, c: (blk(c, b), 0)),
            pl.BlockSpec((ch, vw), lambda b, c: (blk(c, b), out_gate_col // vw)),
            pl.BlockSpec((ch, vw), lambda b, c: (blk(c, b), out_gate_col // vw + 1)),
            pl.BlockSpec((1, nh * dv), lambda b, c: (0, 0)),
        ]
        args += [other, p, p, gain.reshape(1, nh * dv).astype(F32)]
    return pl.pallas_call(
        functools.partial(_mlstm_body, reverse=reverse, finish=finish),
        out_shape=jax.ShapeDtypeStruct((rows, nh * dv), BF16),
        grid=(batch, ncc + nlc),
        in_specs=in_specs,
        out_specs=pl.BlockSpec((ch, nh * dv), lambda b, c: (blk(c, b), 0)),
        scratch_shapes=[pltpu.VMEM((nh, dk, dv + LANE), F32), pltpu.VMEM((8, LANE), F32)],
        compiler_params=_cparams(2, 32 * MIB),
        name="mlstm",
    )(*args)


GLA_BLOCK = 2 * GLA_CHUNK


def _gla_body(q_ref, k_ref, v_ref, lr_ref, w2_ref, gb_ref, *rest, reverse, finish):
    if finish:
        other_ref, og_ref, gain_ref, o_ref, st_ref = rest
    else:
        o_ref, st_ref = rest
    c = pl.program_id(1)
    ch, rows, dk, dv, nh = GLA_CHUNK, GLA_BLOCK, GLA_QK, GLA_V, GLA_HEADS

    @pl.when(c == 0)
    def _():
        st_ref[...] = jnp.zeros_like(st_ref)

    t = lax.broadcasted_iota(jnp.int32, (rows, rows), 0)
    s = lax.broadcasted_iota(jnp.int32, (rows, rows), 1)
    same_chunk = (t ^ s) < ch
    intra = same_chunk & ((s >= t) if reverse else (s <= t))
    cross = ((t < ch) & (s >= ch)) if reverse else ((t >= ch) & (s < ch))
    tri = jnp.where(intra, 1.0, 0.0).astype(BF16)
    row = lax.broadcasted_iota(jnp.int32, (rows, 1), 0)
    in_a = (row >= ch) if reverse else (row < ch)

    pre = jnp.dot(lr_ref[...].astype(BF16), w2_ref[...].astype(BF16), preferred_element_type=F32)
    lg = _log_sigmoid(pre + gb_ref[...]) / GLA_GATE_TEMP
    gcum = sum(jnp.dot(tri, part, preferred_element_type=F32) for part in _split3(lg))
    end_a = gcum[ch:ch + 1, :] if reverse else gcum[ch - 1:ch, :]
    end_b = gcum[0:1, :] if reverse else gcum[rows - 1:rows, :]
    d_a, d_b = jnp.exp(end_a), jnp.exp(end_b)
    e_pos, e_neg = jnp.exp(gcum), jnp.exp(-gcum)
    e_end = jnp.exp(jnp.where(in_a, end_a, end_b) - gcum)
    e_pos_state = e_pos * jnp.where(in_a, 1.0, d_a)
    e_end_state = e_end * jnp.where(in_a, d_b, 1.0)
    decay = d_a * d_b

    for h in range(nh):
        sl = slice(h * dk, (h + 1) * dk)
        q = q_ref[:, sl].astype(F32) * (dk ** -0.5)
        k = k_ref[:, sl].astype(F32)
        v = v_ref[:, h * dv:(h + 1) * dv]
        q_dec = (q * e_pos[:, sl]).astype(BF16)
        q_state = (q * e_pos_state[:, sl]).astype(BF16)
        keys = jnp.concatenate([(k * e_neg[:, sl]).astype(BF16), (k * e_end[:, sl]).astype(BF16)], axis=0)
        k_state = (k * e_end_state[:, sl]).astype(BF16)
        sc = lax.dot_general(q_dec, keys, (((1,), (1,)), ((), ())), preferred_element_type=F32)
        att = jnp.where(intra, sc[:, :rows], jnp.where(cross, sc[:, rows:], 0.0)).astype(BF16)
        st = st_ref[h]
        o = (jnp.dot(att, v, preferred_element_type=F32)
             + lax.dot_general(q_state, st.astype(BF16), (((1,), (1,)), ((), ())), preferred_element_type=F32))
        hsl = slice(h * dv, (h + 1) * dv)
        if finish:
            o = _head_finish(o, other_ref[:, hsl], og_ref[:, hsl], gain_ref[:, hsl], True)
        o_ref[:, hsl] = o.astype(o_ref.dtype)
        st_ref[h] = decay[:, sl] * st + lax.dot_general(
            v, k_state, (((0,), (0,)), ((), ())), preferred_element_type=F32)


def gla(p, lowrank, gate_w2, gate_b, *, reverse, batch, seq, ctx_len, q_col, k_col, v_col,
        other=None, out_gate_col=None, gain=None):
    rows, nh, dk, dv = GLA_BLOCK, GLA_HEADS, GLA_QK, GLA_V
    ncb, nlb = ctx_len // rows, seq // rows
    rank = GLA_GATE_RANK
    d = 1 if reverse else 0
    blk = functools.partial(_scan_row_block, batch=batch, n_ctx_blocks=ncb, n_lat_blocks=nlb, reverse=reverse)

    def lat_blk(c, b):
        return blk(jnp.maximum(c, ncb), b)

    def out_blk(c, b):
        return lat_blk(c, b) - batch * ncb

    w2 = jnp.zeros((LANE, nh * dk), F32).at[d * rank:(d + 1) * rank].set(gate_w2[d])
    finish = other is not None
    in_specs = [
        pl.BlockSpec((rows, nh * dk), lambda b, c: (blk(c, b), q_col // (nh * dk))),
        pl.BlockSpec((rows, nh * dk), lambda b, c: (blk(c, b), k_col // (nh * dk))),
        pl.BlockSpec((rows, nh * dv), lambda b, c: (blk(c, b), v_col // (nh * dv))),
        pl.BlockSpec((rows, LANE), lambda b, c: (blk(c, b), 0)),
        pl.BlockSpec((LANE, nh * dk), lambda b, c: (0, 0)),
        pl.BlockSpec((1, nh * dk), lambda b, c: (0, 0)),
    ]
    args = [p, p, p, lowrank, w2, gate_b[d].reshape(1, nh * dk)]
    if finish:
        in_specs += [
            pl.BlockSpec((rows, nh * dv), lambda b, c: (out_blk(c, b), 0)),
            pl.BlockSpec((rows, nh * dv), lambda b, c: (lat_blk(c, b), out_gate_col // (nh * dv))),
            pl.BlockSpec((1, nh * dv), lambda b, c: (0, 0)),
        ]
        args += [other, p, gain.reshape(1, nh * dv).astype(F32)]
    return pl.pallas_call(
        functools.partial(_gla_body, reverse=reverse, finish=finish),
        out_shape=jax.ShapeDtypeStruct((batch * seq, nh * dv), BF16),
        grid=(batch, ncb + nlb),
        in_specs=in_specs,
        out_specs=pl.BlockSpec((rows, nh * dv), lambda b, c: (out_blk(c, b), 0)),
        scratch_shapes=[pltpu.VMEM((nh, dv, dk), F32)],
        compiler_params=_cparams(2, 44 * MIB),
        name="gla",
    )(*args)


def _rope_tables(seq):
    rows = seq // GRID_W
    row = jnp.repeat(jnp.arange(rows), GRID_W).astype(F32)
    col = jnp.tile(jnp.arange(GRID_W), rows).astype(F32)
    n_freq = ATTN_HEAD_DIM // 4
    inv = ROPE_THETA ** (-jnp.arange(n_freq, dtype=F32) / n_freq)
    ang_r, ang_c = row[:, None] * inv, col[:, None] * inv
    cos = jnp.concatenate([jnp.cos(ang_r), jnp.cos(ang_r), jnp.cos(ang_c), jnp.cos(ang_c)], axis=-1)
    sin = jnp.concatenate([-jnp.sin(ang_r), jnp.sin(ang_r), -jnp.sin(ang_c), jnp.sin(ang_c)], axis=-1)
    return cos, sin


def _conv_ffn(h, w_up, conv_w, conv_b, w_down, *, layer, n_ctx_rows, ctx_len, seq_len, tm):
    g = ffn_up(h, w_up, conv_w, conv_b, layer=layer, tm=tm, n_ctx_rows=n_ctx_rows, ctx_len=ctx_len,
               seq_len=seq_len)
    m, d_ff = g.shape
    d = w_down.shape[2]
    half = d_ff // 2
    kw = dict(layer=layer, m_blocks=m // 512, n_blocks=d // 512, tm=512, tn=512, tk=half)
    f = matmul(g, w_down, k_block=0, **kw)
    return matmul(g, w_down, k_block=1, acc=f, out_dtype=BF16, **kw)


def kernel(x, c, ctx, c_ctx, ada_w, ada_b, norm_w, ab_w_in, ab_gate_b, ab_sink, ab_head_norm, ab_w_out,
           gla_w_in, gla_gate_w2, gla_gate_b, gla_head_norm, gla_w_out,
           ffn_w_up, ffn_conv_w, ffn_conv_b, ffn_w_down):
    batch, seq, d = x.shape
    ctx_len = ctx.shape[1]
    n_ctx, n_lat = batch * ctx_len, batch * seq
    n_tok = n_ctx + n_lat
    depth = ada_w.shape[0]
    assert depth == 2 and ab_w_in.shape[0] == 1 and gla_w_in.shape[0] == 1
    tr = 256
    tm = math.gcd(math.gcd(n_ctx, seq), 1024)
    assert tm % 512 == 0 and n_ctx % tr == 0 and seq % tr == 0 and batch + 1 <= 8
    assert ctx_len & (ctx_len - 1) == 0 and seq & (seq - 1) == 0

    cvec = jnp.zeros((8, d), F32).at[:batch].set(c).at[batch].set(c_ctx)
    mods = ada_mods(cvec, ada_w, ada_b).reshape(depth, 8, 6, 1, d)

    def mod(i, j):
        return mods[i, :, j]

    ctx_tiles = n_ctx // tr
    lat_tiles_per_seq = seq // tr

    def group_all(t):
        return jnp.where(t < ctx_tiles, batch, (t - ctx_tiles) // lat_tiles_per_seq)

    def group_lat(t):
        return t // lat_tiles_per_seq

    ctx2, x2 = ctx.reshape(n_ctx, d), x.reshape(n_lat, d)

    nw = norm_w[0]
    (h,) = row_update(ctx2, group_all, n_tok, x_tail=x2, nwh=nw[0], shift=mod(0, 0), scale=mod(0, 1), tr=tr)
    n_kv = ab_sink.shape[1] // ATTN_GROUP
    aq, akv = n_kv * ATTN_GROUP * ATTN_HEAD_DIM, n_kv * ATTN_HEAD_DIM
    mqk, mv = MLSTM_HEADS * MLSTM_QK, MLSTM_HEADS * MLSTM_V
    cols = [0, aq, aq + akv, aq + 2 * akv, aq + 2 * akv + mqk, aq + 2 * akv + 2 * mqk,
            aq + 2 * akv + 2 * mqk + mv, aq + 2 * akv + 2 * mqk + 2 * mv]
    wide = cols[-1]
    w_in = jnp.swapaxes(ab_w_in, 1, 2)
    p = matmul(h, w_in, layer=0, m_blocks=n_tok // tm, n_blocks=wide // 512, tm=tm, tn=512,
               out_dtype=BF16, w_is_nk=True)
    gates = matmul(h, w_in, layer=0, m_blocks=n_tok // tm, n_blocks=1, tm=tm, tn=LANE,
                   n_off=wide // LANE, n_valid=ab_w_in.shape[2] - wide, w_is_nk=True)
    cos, sin = _rope_tables(seq)
    sink = ab_sink[0].astype(F32)
    attn = context_attention(p, sink, batch=batch, ctx_len=ctx_len, q_col=cols[0], k_col=cols[1], v_col=cols[2])
    attn = window_attention(p, sink, cos, sin, attn, batch=batch, seq=seq, ctx_len=ctx_len,
                            q_col=cols[0], k_col=cols[1], v_col=cols[2])
    scan = functools.partial(mlstm, p, gates, ab_gate_b[0], batch=batch, seq=seq, ctx_len=ctx_len,
                             q_col=cols[3], k_col=cols[4], v_col=cols[5])
    m_all = scan(reverse=True, other=scan(reverse=False), out_gate_col=cols[6], gain=ab_head_norm[0])
    y = matmul([attn, m_all], ab_w_out, layer=0, m_blocks=n_tok // tm, n_blocks=d // 512, tm=tm, tn=512,
               out_dtype=BF16)
    tok, h = row_update(ctx2, group_all, n_tok, x_tail=x2, y=y, gate=mod(0, 2), nwy=nw[1],
                        nwh=nw[2], shift=mod(0, 3), scale=mod(0, 4), tr=tr)
    f = _conv_ffn(h, ffn_w_up, ffn_conv_w, ffn_conv_b, ffn_w_down, layer=0, n_ctx_rows=n_ctx,
                  ctx_len=ctx_len, seq_len=seq, tm=tm)
    nw1 = norm_w[1]
    tok, h = row_update(tok, group_all, n_tok, y=f, gate=mod(0, 5), nwy=nw[3],
                        nwh=nw1[0], shift=mod(1, 0), scale=mod(1, 1), tr=tr)

    gq = GLA_HEADS * GLA_QK
    gv = GLA_HEADS * GLA_V
    wide = 2 * gq + 2 * gv
    w_in = jnp.swapaxes(gla_w_in, 1, 2)
    p = matmul(h, w_in, layer=0, m_blocks=n_tok // tm, n_blocks=wide // 512, tm=tm, tn=512,
               out_dtype=BF16, w_is_nk=True)
    lowrank = matmul(h, w_in, layer=0, m_blocks=n_tok // tm, n_blocks=1, tm=tm, tn=LANE,
                     n_off=wide // LANE, n_valid=gla_w_in.shape[2] - wide, w_is_nk=True)
    scan = functools.partial(gla, p, lowrank, gla_gate_w2[0], gla_gate_b[0], batch=batch, seq=seq,
                             ctx_len=ctx_len, q_col=0, k_col=gq, v_col=2 * gq)
    o_all = scan(reverse=True, other=scan(reverse=False), out_gate_col=2 * gq + gv, gain=gla_head_norm[0])
    y = matmul(o_all, gla_w_out, layer=0, m_blocks=n_lat // tm, n_blocks=d // 512, tm=tm, tn=512,
               out_dtype=BF16)
    xl, h = row_update(tok, group_lat, n_lat, x_row_off=n_ctx, y=y, gate=mod(1, 2), nwy=nw1[1],
                       nwh=nw1[2], shift=mod(1, 3), scale=mod(1, 4), tr=tr)
    f = _conv_ffn(h, ffn_w_up, ffn_conv_w, ffn_conv_b, ffn_w_down, layer=1, n_ctx_rows=0,
                  ctx_len=ctx_len, seq_len=seq, tm=tm)
    (out,) = row_update(xl, group_lat, n_lat, y=f, gate=mod(1, 5), nwy=nw1[3], tr=tr)
    return out.reshape(batch, seq, d)
```

```python
import functools
import math

import jax
import jax.numpy as jnp
from jax import lax
from jax.experimental import pallas as pl
from jax.experimental.pallas import tpu as pltpu

F32 = jnp.float32
BF16 = jnp.bfloat16

NORM_EPS = 1e-6
ROPE_THETA = 10000.0
GRID_W = 64
ATTN_HEAD_DIM = 128
ATTN_GROUP = 4
ATTN_BLOCK = 128
MLSTM_HEADS = 4
MLSTM_QK = 256
MLSTM_V = 512
MLSTM_CHUNK = 128
GLA_HEADS = 8
GLA_QK = 256
GLA_V = 512
GLA_CHUNK = 64
GLA_GATE_TEMP = 16.0
GLA_GATE_RANK = 16
LANE = 128
MXU_DEPTH = 256
MIB = 1024 * 1024
VMEM_CAP = 60 * MIB


def _cparams(n_axes, vmem_bytes, flags=None):
    return pltpu.CompilerParams(
        dimension_semantics=("arbitrary",) * n_axes,
        vmem_limit_bytes=int(min(VMEM_CAP, vmem_bytes)),
        flags=flags,
    )


def _sigmoid(x):
    return 1.0 / (1.0 + jnp.exp(-x))


def _log_sigmoid(x):
    return jnp.minimum(x, 0.0) - jnp.log(1.0 + jnp.exp(-jnp.abs(x)))


def _split3(x):
    hi = x.astype(BF16)
    r1 = x - hi.astype(F32)
    mid = r1.astype(BF16)
    lo = (r1 - mid.astype(F32)).astype(BF16)
    return hi, mid, lo


def _ada_body(s_ref, w_ref, b_ref, o_ref):
    s = s_ref[...]
    s = (s * _sigmoid(s)).astype(BF16)
    o_ref[...] = jnp.dot(s, w_ref[...].astype(BF16), preferred_element_type=F32) + b_ref[...]


def ada_mods(cvec, ada_w, ada_b, tn=1024):
    depth, d, n = ada_w.shape
    rows = cvec.shape[0]
    return pl.pallas_call(
        _ada_body,
        out_shape=jax.ShapeDtypeStruct((depth, rows, n), F32),
        grid=(depth, n // tn),
        in_specs=[
            pl.BlockSpec((rows, d), lambda i, j: (0, 0)),
            pl.BlockSpec((None, d, tn), lambda i, j: (i, 0, j)),
            pl.BlockSpec((None, 1, tn), lambda i, j: (i, 0, j)),
        ],
        out_specs=pl.BlockSpec((None, rows, tn), lambda i, j: (i, 0, j)),
        compiler_params=_cparams(2, 2 * d * tn * 4 + d * tn * 2 + 8 * MIB),
        name="ada_mods",
    )(cvec, ada_w, ada_b.reshape(depth, 1, n))


def _rms(x, w):
    return x * lax.rsqrt(jnp.mean(x * x, axis=-1, keepdims=True) + NORM_EPS) * w


def _row_body(*refs, has_y, has_h, split):
    it = iter(refs)
    x = next(it)[...]
    if split is not None:
        x = jnp.where(pl.program_id(0) < split, x, next(it)[...])
    if has_y:
        y = next(it)[...].astype(F32)
        gate = next(it)[...]
        nwy = next(it)[...]
    if has_h:
        nwh = next(it)[...]
        shift = next(it)[...]
        scale = next(it)[...]
    if has_y:
        x = x + gate * _rms(y, nwy)
        next(it)[...] = x
    if has_h:
        next(it)[...] = (_rms(x, nwh) * (1.0 + scale) + shift).astype(BF16)


def row_update(x, group_of_tile, n_rows, *, x_tail=None, x_row_off=0, y=None, gate=None, nwy=None,
               nwh=None, shift=None, scale=None, tr=256):
    d = x.shape[1]
    has_y, has_h = y is not None, nwh is not None
    off = x_row_off // tr
    row = pl.BlockSpec((tr, d), lambda t: (t, 0))
    vec = pl.BlockSpec((1, d), lambda t: (0, 0))
    tab = pl.BlockSpec((None, 1, d), lambda t: (group_of_tile(t), 0, 0))
    split = None
    if x_tail is None:
        args, specs = [x], [pl.BlockSpec((tr, d), lambda t: (t + off, 0))]
    else:
        split = x.shape[0] // tr
        args = [x, x_tail]
        specs = [pl.BlockSpec((tr, d), lambda t: (jnp.minimum(t, split - 1), 0)),
                 pl.BlockSpec((tr, d), lambda t: (jnp.maximum(t - split, 0), 0))]
    if has_y:
        args += [y, gate, nwy.reshape(1, d)]
        specs += [row, tab, vec]
    if has_h:
        args += [nwh.reshape(1, d), shift, scale]
        specs += [vec, tab, tab]
    out_shape, out_specs = [], []
    if has_y:
        out_shape.append(jax.ShapeDtypeStruct((n_rows, d), F32))
        out_specs.append(row)
    if has_h:
        out_shape.append(jax.ShapeDtypeStruct((n_rows, d), BF16))
        out_specs.append(row)
    return pl.pallas_call(
        functools.partial(_row_body, has_y=has_y, has_h=has_h, split=split),
        out_shape=out_shape,
        grid=(n_rows // tr,),
        in_specs=specs,
        out_specs=out_specs,
        compiler_params=_cparams(1, 2 * tr * d * 22 + 8 * MIB),
        name="row_update",
    )(*args)


def _mm_body(*refs, n_a, n_valid, has_acc, w_is_nk):
    a_refs = refs[:n_a]
    w_ref = refs[n_a]
    acc_ref = refs[n_a + 1] if has_acc else None
    o_ref, wbf_ref = refs[-2], refs[-1]

    @pl.when(pl.program_id(1) == 0)
    def _():
        w = w_ref[...]
        if n_valid is not None:
            col = lax.broadcasted_iota(jnp.int32, w.shape, 0 if w_is_nk else 1)
            w = jnp.where(col < n_valid, w, 0.0)
        wbf_ref[...] = w.astype(BF16)

    r, k0 = None, 0
    for a_ref in a_refs:
        k1 = k0 + a_ref.shape[1]
        if w_is_nk:
            part = lax.dot_general(a_ref[...], wbf_ref[:, k0:k1], (((1,), (1,)), ((), ())),
                                   preferred_element_type=F32)
        else:
            part = jnp.dot(a_ref[...], wbf_ref[k0:k1, :], preferred_element_type=F32)
        r = part if r is None else r + part
        k0 = k1
    if has_acc:
        r = r + acc_ref[...]
    o_ref[...] = r.astype(o_ref.dtype)


def matmul(a, w, *, layer, m_blocks, n_blocks, tm, tn, tk=None, k_block=0, a_row_off=0, n_off=0,
           n_valid=None, acc=None, out_dtype=F32, w_is_nk=False):
    a_list = list(a) if isinstance(a, (list, tuple)) else [a]
    k_total = sum(x.shape[1] for x in a_list)
    tk = k_total if tk is None else tk
    assert len(a_list) == 1 or tk == k_total
    roff = a_row_off // tm
    in_specs = [pl.BlockSpec((tm, tk if len(a_list) == 1 else x.shape[1]), lambda n, m: (m + roff, k_block))
                for x in a_list]
    if w_is_nk:
        in_specs.append(pl.BlockSpec((None, tn, tk), lambda n, m: (layer, n + n_off, k_block)))
    else:
        in_specs.append(pl.BlockSpec((None, tk, tn), lambda n, m: (layer, k_block, n + n_off)))
    args = a_list + [w]
    out_spec = pl.BlockSpec((tm, tn), lambda n, m: (m, n))
    if acc is not None:
        in_specs.append(out_spec)
        args.append(acc)
    osz = jnp.dtype(out_dtype).itemsize
    vmem = (2 * tm * tk * 2 + 2 * tk * tn * 4 + tk * tn * 2 + 2 * tm * tn * osz
            + (2 * tm * tn * 4 if acc is not None else 0) + 2 * tm * tn * 4 + 4 * MIB)
    return pl.pallas_call(
        functools.partial(_mm_body, n_a=len(a_list), n_valid=n_valid, has_acc=acc is not None, w_is_nk=w_is_nk),
        out_shape=jax.ShapeDtypeStruct((m_blocks * tm, n_blocks * tn), out_dtype),
        grid=(n_blocks, m_blocks),
        in_specs=in_specs,
        out_specs=out_spec,
        scratch_shapes=[pltpu.VMEM((tn, tk) if w_is_nk else (tk, tn), BF16)],
        compiler_params=_cparams(2, vmem),
        name="matmul",
    )(*args)


FFN_LAG = 2
FFN_RING = FFN_LAG + 1


A_SLOTS = 3


def _ffn_up_body(a_hbm, wg_ref, wv_ref, cwg_ref, cwv_ref, cbg_ref, cbv_ref, o_ref, wg_bf, wv_bf, ug_ref, uv_ref,
                 a_buf, a_sem, *, tm, tiles, total, n_ctx_rows, ctx_len, seq_len):
    s = pl.program_id(0)
    last = total + FFN_LAG - 1
    m = jnp.minimum(s, total - 1) % tiles
    m_out = jnp.maximum(s - FFN_LAG, 0) % tiles

    def a_copy(step):
        tile = jnp.minimum(step, total - 1) % tiles
        slot = step % A_SLOTS
        return pltpu.make_async_copy(a_hbm.at[pl.ds(tile * tm, tm), :], a_buf.at[slot], a_sem.at[slot])

    @pl.when(s == 0)
    def _():
        a_copy(0).start()
        a_copy(1).start()

    @pl.when(s + 2 <= last)
    def _():
        a_copy(s + 2).start()

    a_copy(s).wait()

    @pl.when(s == 0)
    def _():
        ug_ref[...] = jnp.zeros_like(ug_ref)
        uv_ref[...] = jnp.zeros_like(uv_ref)

    @pl.when(m == 0)
    def _():
        wg_bf[...] = wg_ref[...].astype(BF16)
        wv_bf[...] = wv_ref[...].astype(BF16)

    slot_new = s % FFN_RING
    slot_mid = (s + 1) % FFN_RING
    slot_next = (s + 2) % FFN_RING
    k_tiles = a_buf.shape[2] // MXU_DEPTH
    chunk = tm // k_tiles
    unit = min(chunk, ctx_len)
    row8 = lax.broadcasted_iota(jnp.int32, (8, o_ref.shape[1]), 0)

    def starts_sequence(g):
        return (g <= n_ctx_rows) | (((g - n_ctx_rows) & (seq_len - 1)) == 0)

    def conv(u_ref, cw_ref, cb_ref, r0):
        r1 = r0 + unit
        above = u_ref[slot_new, tm - 1:tm, :] if r0 == 0 else u_ref[slot_mid, r0 - 1:r0, :]
        below = u_ref[slot_next, 0:1, :] if r1 == tm else u_ref[slot_mid, r1:r1 + 1, :]
        if r0 % ctx_len == 0:
            above = jnp.where(starts_sequence(m_out * tm + r0), 0.0, above)
        if r1 % ctx_len == 0:
            below = jnp.where(starts_sequence(m_out * tm + r1), 0.0, below)
        u = u_ref[slot_mid, r0:r1, :]
        down, up = pltpu.roll(u, 1, 0), pltpu.roll(u, unit - 1, 0)
        u_prev = jnp.concatenate([jnp.where(row8 == 0, above, down[0:8]), down[8:]], axis=0)
        u_next = jnp.concatenate([up[:unit - 8], jnp.where(row8 == 7, below, up[unit - 8:])], axis=0)
        cw = cw_ref[...]
        return cb_ref[...] + u_prev * cw[0:1] + u * cw[1:2] + u_next * cw[2:3]

    sixteen = jnp.uint32(16)
    zeros = []
    for j in range(k_tiles):
        seen = None
        for r0 in range(j * chunk, (j + 1) * chunk, unit):
            gate = conv(ug_ref, cwg_ref, cbg_ref, r0)
            val = conv(uv_ref, cwv_ref, cbv_ref, r0)
            out = gate * _sigmoid(gate) * val
            o_ref[r0:r0 + unit, :] = out.astype(o_ref.dtype)
            bits = lax.bitcast_convert_type(out, jnp.uint32)
            for r in range(0, unit, 8):
                for c in range(0, bits.shape[1], LANE):
                    blk = bits[r:r + 8, c:c + LANE]
                    seen = blk if seen is None else seen | blk
        zeros.append(pltpu.bitcast(lax.shift_right_logical(lax.shift_right_logical(seen, sixteen), sixteen), BF16))

    def paced(w):
        parts = [w[0:MXU_DEPTH]]
        zr = zeros[0].shape[0]
        for k in range(1, k_tiles):
            blk = w[k * MXU_DEPTH:(k + 1) * MXU_DEPTH]
            head = jnp.concatenate([blk[0:zr, :LANE] + zeros[k - 1], blk[0:zr, LANE:]], axis=1)
            parts.append(jnp.concatenate([head, blk[zr:]], axis=0))
        return jnp.concatenate(parts, axis=0)

    a = a_buf[s % A_SLOTS]
    ug_ref[slot_new] = jnp.dot(a, paced(wg_bf[...]), preferred_element_type=F32)
    uv_ref[slot_new] = jnp.dot(a, paced(wv_bf[...]), preferred_element_type=F32)


def ffn_up(h, w_up, conv_w, conv_b, *, layer, tm, n_ctx_rows, ctx_len, seq_len, tn=256):
    m_rows, k = h.shape
    tiles = m_rows // tm
    d_ff = w_up.shape[2] // 2
    nb = d_ff // tn
    total = nb * tiles
    assert tm % ctx_len == 0 and seq_len % ctx_len == 0 and n_ctx_rows % ctx_len == 0 and tn % LANE == 0
    chunk = tm // (k // MXU_DEPTH)
    assert k % MXU_DEPTH == 0 and chunk % 8 == 0 and (chunk % ctx_len == 0 or ctx_len % chunk == 0)
    depth = w_up.shape[0]
    cb = conv_b.reshape(depth, 1, 2 * d_ff)

    def cur(s):
        sc = jnp.minimum(s, total - 1)
        return sc // tiles, sc % tiles

    def out(s):
        so = jnp.maximum(s - FFN_LAG, 0)
        return so // tiles, so % tiles

    vmem = (A_SLOTS * tm * k * 2 + 2 * 2 * k * tn * 4 + 2 * k * tn * 2 + 2 * FFN_RING * tm * tn * 4
            + 2 * tm * tn * 2 + 12 * tm * tn * 4 + 4 * MIB)
    return pl.pallas_call(
        functools.partial(_ffn_up_body, tm=tm, tiles=tiles, total=total, n_ctx_rows=n_ctx_rows,
                          ctx_len=ctx_len, seq_len=seq_len),
        out_shape=jax.ShapeDtypeStruct((m_rows, d_ff), BF16),
        grid=(total + FFN_LAG,),
        in_specs=[
            pl.BlockSpec(memory_space=pl.ANY),
            pl.BlockSpec((None, k, tn), lambda s: (layer, 0, cur(s)[0])),
            pl.BlockSpec((None, k, tn), lambda s: (layer, 0, cur(s)[0] + nb)),
            pl.BlockSpec((None, 3, tn), lambda s: (layer, 0, out(s)[0])),
            pl.BlockSpec((None, 3, tn), lambda s: (layer, 0, out(s)[0] + nb)),
            pl.BlockSpec((None, 1, tn), lambda s: (layer, 0, out(s)[0])),
            pl.BlockSpec((None, 1, tn), lambda s: (layer, 0, out(s)[0] + nb)),
        ],
        out_specs=pl.BlockSpec((tm, tn), lambda s: (out(s)[1], out(s)[0])),
        scratch_shapes=[pltpu.VMEM((k, tn), BF16), pltpu.VMEM((k, tn), BF16),
                        pltpu.VMEM((FFN_RING, tm, tn), F32), pltpu.VMEM((FFN_RING, tm, tn), F32),
                        pltpu.VMEM((A_SLOTS, tm, k), BF16), pltpu.SemaphoreType.DMA((A_SLOTS,))],
        compiler_params=_cparams(1, vmem),
        name="ffn_up",
    )(h, w_up, w_up, conv_w, conv_w, cb, cb)


def _rope(x, cos, sin_signed):
    lane = lax.broadcasted_iota(jnp.int32, x.shape, 1)
    q = ATTN_HEAD_DIM // 4
    partner = jnp.where((lane & (2 * q - 1)) < q, pltpu.roll(x, ATTN_HEAD_DIM - q, 1), pltpu.roll(x, q, 1))
    return x * cos + partner * sin_signed


def _sink_attention(q4, kcat, vcat, sink_col, bias=None, score_scale=None):
    s = lax.dot_general(q4, kcat, (((1,), (1,)), ((), ())), preferred_element_type=F32)
    if score_scale is not None:
        s = s * score_scale
    if bias is not None:
        s = s + bias
    m = jnp.maximum(jnp.max(s, axis=-1, keepdims=True), sink_col)
    e = jnp.exp(s - m)
    denom = jnp.sum(e, axis=-1, keepdims=True) + jnp.exp(sink_col - m)
    return jnp.dot(e.astype(BF16), vcat, preferred_element_type=F32) * (1.0 / denom)


def _sink_column(sink_ref, j, rows):
    return jnp.concatenate(
        [jnp.full((rows, 1), sink_ref[j * ATTN_GROUP + g], F32) for g in range(ATTN_GROUP)], axis=0)


def _win_attn_body(sink_ref, q_ref, kp_ref, kc_ref, kn_ref, vp_ref, vc_ref, vn_ref, kx_ref, vx_ref,
                   cos_ref, sin_ref, band_ref, prev_ref, o_ref, *, nb, n_kv):
    del prev_ref
    n = pl.program_id(1)
    blk, dh = ATTN_BLOCK, ATTN_HEAD_DIM
    qw = ATTN_GROUP * dh
    scale = dh ** -0.5

    def table(ref, i):
        return ref[pl.ds(pl.multiple_of(i * blk, blk), blk), :]

    n_prev, n_next = jnp.maximum(n - 1, 0), jnp.minimum(n + 1, nb - 1)
    cos_q, sin_q = table(cos_ref, n), table(sin_ref, n)
    cos_p, sin_p = table(cos_ref, n_prev), table(sin_ref, n_prev)
    cos_n, sin_n = table(cos_ref, n_next), table(sin_ref, n_next)
    n_keys = 3 * blk + kx_ref.shape[0]
    kj = lax.broadcasted_iota(jnp.int32, (1, n_keys), 1)
    lo = jnp.where(n > 0, 0, blk)
    hi = jnp.where(n < nb - 1, 3 * blk, 2 * blk)
    bias = band_ref[...] + jnp.where(((kj >= lo) & (kj < hi)) | (kj >= 3 * blk), 0.0, -jnp.inf)

    for j in range(n_kv):
        ksl = slice(j * dh, (j + 1) * dh)
        q = q_ref[:, j * qw:(j + 1) * qw].astype(F32)
        q4 = jnp.concatenate(
            [_rope(q[:, g * dh:(g + 1) * dh], cos_q, sin_q) * scale for g in range(ATTN_GROUP)],
            axis=0).astype(BF16)
        kcat = jnp.concatenate([
            _rope(kp_ref[:, ksl].astype(F32), cos_p, sin_p).astype(BF16),
            _rope(kc_ref[:, ksl].astype(F32), cos_q, sin_q).astype(BF16),
            _rope(kn_ref[:, ksl].astype(F32), cos_n, sin_n).astype(BF16),
            kx_ref[:, ksl]], axis=0)
        vcat = jnp.concatenate([vp_ref[:, ksl], vc_ref[:, ksl], vn_ref[:, ksl], vx_ref[:, ksl]], axis=0)
        o = _sink_attention(q4, kcat, vcat, _sink_column(sink_ref, j, blk), bias=bias)
        for g in range(ATTN_GROUP):
            c0 = j * qw + g * dh
            o_ref[:, c0:c0 + dh] = o[g * blk:(g + 1) * blk].astype(o_ref.dtype)


def window_attention(p, sink, cos, sin_signed, prev, *, batch, seq, ctx_len, q_col, k_col, v_col):
    blk, dh = ATTN_BLOCK, ATTN_HEAD_DIM
    nb = seq // blk
    n_kv = (k_col - q_col) // (ATTN_GROUP * dh)
    lat0 = batch * ctx_len // blk
    qw = n_kv * ATTN_GROUP * dh
    kw = n_kv * dh

    def kv_spec(col, shift):
        return pl.BlockSpec((blk, kw), lambda b, n: (lat0 + b * nb + jnp.clip(n + shift, 0, nb - 1), col // kw))

    rows, n_keys = ATTN_GROUP * blk, 3 * blk + ctx_len
    qi = jnp.arange(rows)[:, None] % blk
    kj = jnp.arange(n_keys)[None, :]
    band = jnp.where(((kj >= qi) & (kj <= qi + 2 * blk)) | (kj >= 3 * blk), 0.0, -jnp.inf).astype(F32)

    in_specs = [
        pl.BlockSpec(memory_space=pltpu.SMEM),
        pl.BlockSpec((blk, qw), lambda b, n: (lat0 + b * nb + n, q_col // qw)),
        kv_spec(k_col, -1), kv_spec(k_col, 0), kv_spec(k_col, 1),
        kv_spec(v_col, -1), kv_spec(v_col, 0), kv_spec(v_col, 1),
        pl.BlockSpec((ctx_len, kw), lambda b, n: (b, k_col // kw)),
        pl.BlockSpec((ctx_len, kw), lambda b, n: (b, v_col // kw)),
        pl.BlockSpec((seq, dh), lambda b, n: (0, 0)),
        pl.BlockSpec((seq, dh), lambda b, n: (0, 0)),
        pl.BlockSpec((rows, n_keys), lambda b, n: (0, 0)),
        pl.BlockSpec(memory_space=pl.ANY),
    ]
    return pl.pallas_call(
        functools.partial(_win_attn_body, nb=nb, n_kv=n_kv),
        out_shape=jax.ShapeDtypeStruct(prev.shape, prev.dtype),
        grid=(batch, nb),
        in_specs=in_specs,
        out_specs=pl.BlockSpec((blk, qw), lambda b, n: (lat0 + b * nb + n, 0)),
        input_output_aliases={13: 0},
        compiler_params=_cparams(2, 32 * MIB),
        name="window_attention",
    )(sink, p, p, p, p, p, p, p, p, p, cos, sin_signed, band, prev)


def _ctx_attn_body(sink_ref, q_ref, k_ref, v_ref, o_ref, *, n_kv, batch):
    dh = ATTN_HEAD_DIM
    qw = ATTN_GROUP * dh
    rows = q_ref.shape[0]

    @pl.when(pl.program_id(0) < batch)
    def _():
        for j in range(n_kv):
            ksl = slice(j * dh, (j + 1) * dh)
            q4 = jnp.concatenate(
                [q_ref[:, j * qw + g * dh:j * qw + (g + 1) * dh] for g in range(ATTN_GROUP)], axis=0)
            o = _sink_attention(q4, k_ref[:, ksl], v_ref[:, ksl], _sink_column(sink_ref, j, rows),
                                score_scale=dh ** -0.5)
            for g in range(ATTN_GROUP):
                c0 = j * qw + g * dh
                o_ref[:, c0:c0 + dh] = o[g * rows:(g + 1) * rows].astype(o_ref.dtype)

    @pl.when(pl.program_id(0) >= batch)
    def _():
        o_ref[...] = jnp.zeros_like(o_ref)


def context_attention(p, sink, *, batch, ctx_len, q_col, k_col, v_col):
    dh = ATTN_HEAD_DIM
    n_kv = (k_col - q_col) // (ATTN_GROUP * dh)
    qw = n_kv * ATTN_GROUP * dh
    kw = n_kv * dh
    last = batch - 1
    return pl.pallas_call(
        functools.partial(_ctx_attn_body, n_kv=n_kv, batch=batch),
        out_shape=jax.ShapeDtypeStruct((p.shape[0], qw), BF16),
        grid=(p.shape[0] // ctx_len,),
        in_specs=[
            pl.BlockSpec(memory_space=pltpu.SMEM),
            pl.BlockSpec((ctx_len, qw), lambda b: (jnp.minimum(b, last), q_col // qw)),
            pl.BlockSpec((ctx_len, kw), lambda b: (jnp.minimum(b, last), k_col // kw)),
            pl.BlockSpec((ctx_len, kw), lambda b: (jnp.minimum(b, last), v_col // kw)),
        ],
        out_specs=pl.BlockSpec((ctx_len, qw), lambda b: (b, 0)),
        compiler_params=_cparams(1, 32 * MIB),
        name="context_attention",
    )(sink, p, p, p)


def _scan_row_block(c, b, batch, n_ctx_blocks, n_lat_blocks, reverse):
    is_ctx = c < n_ctx_blocks
    i = jnp.where(is_ctx, c, c - n_ctx_blocks)
    if reverse:
        i = jnp.where(is_ctx, n_ctx_blocks - 1, n_lat_blocks - 1) - i
    return jnp.where(is_ctx, b * n_ctx_blocks + i, batch * n_ctx_blocks + b * n_lat_blocks + i)


def _head_finish(h_own, h_other, gate, gain, silu_gate):
    h = h_own + h_other.astype(F32)
    hn = h * lax.rsqrt(jnp.mean(h * h, axis=-1, keepdims=True) + NORM_EPS)
    g = gate.astype(F32)
    return hn * gain * (g * _sigmoid(g) if silu_gate else _sigmoid(g))


def _mlstm_body(q_ref, k_ref, v0_ref, v1_ref, g_ref, gt_ref, gb_row_ref, gb_col_ref, *rest, reverse, finish):
    if finish:
        other_ref, og0_ref, og1_ref, gain_ref, o_ref, cn_ref, m_ref = rest
    else:
        o_ref, cn_ref, m_ref = rest
    c = pl.program_id(1)
    ch, dk, dv, nh = MLSTM_CHUNK, MLSTM_QK, MLSTM_V, MLSTM_HEADS

    @pl.when(c == 0)
    def _():
        cn_ref[...] = jnp.zeros_like(cn_ref)
        m_ref[...] = jnp.zeros_like(m_ref)

    t = lax.broadcasted_iota(jnp.int32, (ch, ch), 0)
    s = lax.broadcasted_iota(jnp.int32, (ch, ch), 1)
    valid = (s >= t) if reverse else (s <= t)
    tri = jnp.where(valid, 1.0, 0.0).astype(BF16)
    last_t = 0 if reverse else ch - 1
    col_i = 2 * nh if reverse else 0
    col_f = 3 * nh if reverse else nh

    g_col = g_ref[...] + gb_row_ref[...]
    g_row = gt_ref[...] + gb_col_ref[...]
    lf_col, lf_row = _log_sigmoid(g_col), _log_sigmoid(g_row)
    bc_col_all = sum(jnp.dot(tri, part, preferred_element_type=F32) for part in _split3(lf_col))
    bc_row_all = sum(lax.dot_general(part, tri, (((1,), (1,)), ((), ())), preferred_element_type=F32)
                     for part in _split3(lf_row))
    ones_col = jnp.where(lax.broadcasted_iota(jnp.int32, (ch, LANE), 1) == 0, 1.0, 0.0)

    for h in range(nh):
        ig_col = g_col[:, col_i + h:col_i + h + 1]
        ig_row = g_row[col_i + h:col_i + h + 1, :]
        bc_col = bc_col_all[:, col_f + h:col_f + h + 1]
        bc_row = bc_row_all[col_f + h:col_f + h + 1, :]
        m_st = m_ref[h:h + 1, 0:1]
        q = q_ref[:, h * dk:(h + 1) * dk]
        k = k_ref[:, h * dk:(h + 1) * dk] * (dk ** -0.5)
        v_ref = v0_ref if h < nh // 2 else v1_ref
        hv = h % (nh // 2)
        v = v_ref[:, hv * dv:(hv + 1) * dv]

        log_d = jnp.where(valid, bc_col - bc_row + ig_row, -jnp.inf)
        inter = bc_col + m_st
        m_t = jnp.maximum(inter, jnp.max(log_d, axis=-1, keepdims=True))
        sc = lax.dot_general(q, k, (((1,), (1,)), ((), ())), preferred_element_type=F32) * jnp.exp(log_d - m_t)
        w_inter = jnp.exp(inter - m_t)
        state = cn_ref[h]
        q_state = jnp.dot(q, state.astype(BF16), preferred_element_type=F32)
        num = jnp.dot(sc.astype(BF16), v, preferred_element_type=F32) + w_inter * q_state[:, :dv]
        den = jnp.sum(sc, axis=-1, keepdims=True) + w_inter * q_state[:, dv:dv + 1]
        h_t = num * (1.0 / jnp.maximum(jnp.abs(den), jnp.exp(-m_t)))
        if finish:
            og_ref = og0_ref if h < nh // 2 else og1_ref
            h_t = _head_finish(h_t, other_ref[:, h * dv:(h + 1) * dv], og_ref[:, hv * dv:(hv + 1) * dv],
                               gain_ref[:, h * dv:(h + 1) * dv], False)
        o_ref[:, h * dv:(h + 1) * dv] = h_t.astype(o_ref.dtype)

        b_end = bc_col[last_t:last_t + 1, :]
        m_new = m_t[last_t:last_t + 1, :]
        wk = jnp.exp(b_end - bc_col + ig_col - m_new)
        decay = jnp.exp(b_end + m_st - m_new)
        v_ext = jnp.concatenate([v.astype(F32), ones_col], axis=1) * wk
        cn_ref[h] = decay * state + lax.dot_general(
            k, v_ext.astype(BF16), (((0,), (0,)), ((), ())), preferred_element_type=F32)
        m_ref[h:h + 1, :] = jnp.broadcast_to(m_new, (1, LANE))


def mlstm(p, gates, gate_b, *, reverse, batch, seq, ctx_len, q_col, k_col, v_col,
          other=None, out_gate_col=None, gain=None):
    ch, nh, dk, dv = MLSTM_CHUNK, MLSTM_HEADS, MLSTM_QK, MLSTM_V
    ncc, nlc = ctx_len // ch, seq // ch
    rows = p.shape[0]
    blk = functools.partial(_scan_row_block, batch=batch, n_ctx_blocks=ncc, n_lat_blocks=nlc, reverse=reverse)
    gb = jnp.zeros((LANE,), F32).at[:4 * nh].set(gate_b.astype(F32))
    qw, vw = nh * dk, nh * dv // 2
    finish = other is not None
    in_specs = [
        pl.BlockSpec((ch, qw), lambda b, c: (blk(c, b), q_col // qw)),
        pl.BlockSpec((ch, qw), lambda b, c: (blk(c, b), k_col // qw)),
        pl.BlockSpec((ch, vw), lambda b, c: (blk(c, b), v_col // vw)),
        pl.BlockSpec((ch, vw), lambda b, c: (blk(c, b), v_col // vw + 1)),
        pl.BlockSpec((ch, LANE), lambda b, c: (blk(c, b), 0)),
        pl.BlockSpec((LANE, ch), lambda b, c: (0, blk(c, b))),
        pl.BlockSpec((1, LANE), lambda b, c: (0, 0)),
        pl.BlockSpec((LANE, 1), lambda b, c: (0, 0)),
    ]
    args = [p, p, p, p, gates, gates.T, gb.reshape(1, LANE), gb.reshape(LANE, 1)]
    if finish:
        in_specs += [
            pl.BlockSpec((ch, nh * dv), lambda b, c: (blk(c, b), 0)),
            pl.BlockSpec((ch, vw), lambda b, c: (blk(c, b), out_gate_col // vw)),
            pl.BlockSpec((ch, vw), lambda b, c: (blk(c, b), out_gate_col // vw + 1)),
            pl.BlockSpec((1, nh * dv), lambda b, c: (0, 0)),
        ]
        args += [other, p, p, gain.reshape(1, nh * dv).astype(F32)]
    return pl.pallas_call(
        functools.partial(_mlstm_body, reverse=reverse, finish=finish),
        out_shape=jax.ShapeDtypeStruct((rows, nh * dv), BF16),
        grid=(batch, ncc + nlc),
        in_specs=in_specs,
        out_specs=pl.BlockSpec((ch, nh * dv), lambda b, c: (blk(c, b), 0)),
        scratch_shapes=[pltpu.VMEM((nh, dk, dv + LANE), F32), pltpu.VMEM((8, LANE), F32)],
        compiler_params=_cparams(2, 32 * MIB),
        name="mlstm",
    )(*args)


GLA_BLOCK = 2 * GLA_CHUNK


def _gla_body(q_ref, k_ref, v_ref, lr_ref, w2_ref, gb_ref, *rest, reverse, finish):
    if finish:
        other_ref, og_ref, gain_ref, o_ref, st_ref = rest
    else:
        o_ref, st_ref = rest
    c = pl.program_id(1)
    ch, rows, dk, dv, nh = GLA_CHUNK, GLA_BLOCK, GLA_QK, GLA_V, GLA_HEADS

    @pl.when(c == 0)
    def _():
        st_ref[...] = jnp.zeros_like(st_ref)

    t = lax.broadcasted_iota(jnp.int32, (rows, rows), 0)
    s = lax.broadcasted_iota(jnp.int32, (rows, rows), 1)
    same_chunk = (t ^ s) < ch
    intra = same_chunk & ((s >= t) if reverse else (s <= t))
    cross = ((t < ch) & (s >= ch)) if reverse else ((t >= ch) & (s < ch))
    tri = jnp.where(intra, 1.0, 0.0).astype(BF16)
    row = lax.broadcasted_iota(jnp.int32, (rows, 1), 0)
    in_a = (row >= ch) if reverse else (row < ch)

    pre = jnp.dot(lr_ref[...].astype(BF16), w2_ref[...].astype(BF16), preferred_element_type=F32)
    lg = _log_sigmoid(pre + gb_ref[...]) / GLA_GATE_TEMP
    gcum = sum(jnp.dot(tri, part, preferred_element_type=F32) for part in _split3(lg))
    end_a = gcum[ch:ch + 1, :] if reverse else gcum[ch - 1:ch, :]
    end_b = gcum[0:1, :] if reverse else gcum[rows - 1:rows, :]
    d_a, d_b = jnp.exp(end_a), jnp.exp(end_b)
    e_pos, e_neg = jnp.exp(gcum), jnp.exp(-gcum)
    e_end = jnp.exp(jnp.where(in_a, end_a, end_b) - gcum)
    e_pos_state = e_pos * jnp.where(in_a, 1.0, d_a)
    e_end_state = e_end * jnp.where(in_a, d_b, 1.0)
    decay = d_a * d_b

    for h in range(nh):
        sl = slice(h * dk, (h + 1) * dk)
        q = q_ref[:, sl].astype(F32) * (dk ** -0.5)
        k = k_ref[:, sl].astype(F32)
        v = v_ref[:, h * dv:(h + 1) * dv]
        q_dec = (q * e_pos[:, sl]).astype(BF16)
        q_state = (q * e_pos_state[:, sl]).astype(BF16)
        keys = jnp.concatenate([(k * e_neg[:, sl]).astype(BF16), (k * e_end[:, sl]).astype(BF16)], axis=0)
        k_state = (k * e_end_state[:, sl]).astype(BF16)
        sc = lax.dot_general(q_dec, keys, (((1,), (1,)), ((), ())), preferred_element_type=F32)
        att = jnp.where(intra, sc[:, :rows], jnp.where(cross, sc[:, rows:], 0.0)).astype(BF16)
        st = st_ref[h]
        o = (jnp.dot(att, v, preferred_element_type=F32)
             + lax.dot_general(q_state, st.astype(BF16), (((1,), (1,)), ((), ())), preferred_element_type=F32))
        hsl = slice(h * dv, (h + 1) * dv)
        if finish:
            o = _head_finish(o, other_ref[:, hsl], og_ref[:, hsl], gain_ref[:, hsl], True)
        o_ref[:, hsl] = o.astype(o_ref.dtype)
        st_ref[h] = decay[:, sl] * st + lax.dot_general(
            v, k_state, (((0,), (0,)), ((), ())), preferred_element_type=F32)


def gla(p, lowrank, gate_w2, gate_b, *, reverse, batch, seq, ctx_len, q_col, k_col, v_col,
        other=None, out_gate_col=None, gain=None):
    rows, nh, dk, dv = GLA_BLOCK, GLA_HEADS, GLA_QK, GLA_V
    ncb, nlb = ctx_len // rows, seq // rows
    rank = GLA_GATE_RANK
    d = 1 if reverse else 0
    blk = functools.partial(_scan_row_block, batch=batch, n_ctx_blocks=ncb, n_lat_blocks=nlb, reverse=reverse)

    def lat_blk(c, b):
        return blk(jnp.maximum(c, ncb), b)

    def out_blk(c, b):
        return lat_blk(c, b) - batch * ncb

    w2 = jnp.zeros((LANE, nh * dk), F32).at[d * rank:(d + 1) * rank].set(gate_w2[d])
    finish = other is not None
    in_specs = [
        pl.BlockSpec((rows, nh * dk), lambda b, c: (blk(c, b), q_col // (nh * dk))),
        pl.BlockSpec((rows, nh * dk), lambda b, c: (blk(c, b), k_col // (nh * dk))),
        pl.BlockSpec((rows, nh * dv), lambda b, c: (blk(c, b), v_col // (nh * dv))),
        pl.BlockSpec((rows, LANE), lambda b, c: (blk(c, b), 0)),
        pl.BlockSpec((LANE, nh * dk), lambda b, c: (0, 0)),
        pl.BlockSpec((1, nh * dk), lambda b, c: (0, 0)),
    ]
    args = [p, p, p, lowrank, w2, gate_b[d].reshape(1, nh * dk)]
    if finish:
        in_specs += [
            pl.BlockSpec((rows, nh * dv), lambda b, c: (out_blk(c, b), 0)),
            pl.BlockSpec((rows, nh * dv), lambda b, c: (lat_blk(c, b), out_gate_col // (nh * dv))),
            pl.BlockSpec((1, nh * dv), lambda b, c: (0, 0)),
        ]
        args += [other, p, gain.reshape(1, nh * dv).astype(F32)]
    return pl.pallas_call(
        functools.partial(_gla_body, reverse=reverse, finish=finish),
        out_shape=jax.ShapeDtypeStruct((batch * seq, nh * dv), BF16),
        grid=(batch, ncb + nlb),
        in_specs=in_specs,
        out_specs=pl.BlockSpec((rows, nh * dv), lambda b, c: (out_blk(c, b), 0)),
        scratch_shapes=[pltpu.VMEM((nh, dv, dk), F32)],
        compiler_params=_cparams(2, 44 * MIB),
        name="gla",
    )(*args)


def _rope_tables(seq):
    rows = seq // GRID_W
    row = jnp.repeat(jnp.arange(rows), GRID_W).astype(F32)
    col = jnp.tile(jnp.arange(GRID_W), rows).astype(F32)
    n_freq = ATTN_HEAD_DIM // 4
    inv = ROPE_THETA ** (-jnp.arange(n_freq, dtype=F32) / n_freq)
    ang_r, ang_c = row[:, None] * inv, col[:, None] * inv
    cos = jnp.concatenate([jnp.cos(ang_r), jnp.cos(ang_r), jnp.cos(ang_c), jnp.cos(ang_c)], axis=-1)
    sin = jnp.concatenate([-jnp.sin(ang_r), jnp.sin(ang_r), -jnp.sin(ang_c), jnp.sin(ang_c)], axis=-1)
    return cos, sin


def _conv_ffn(h, w_up, conv_w, conv_b, w_down, *, layer, n_ctx_rows, ctx_len, seq_len, tm):
    g = ffn_up(h, w_up, conv_w, conv_b, layer=layer, tm=tm, n_ctx_rows=n_ctx_rows, ctx_len=ctx_len,
               seq_len=seq_len)
    m, d_ff = g.shape
    d = w_down.shape[2]
    half = d_ff // 2
    kw = dict(layer=layer, m_blocks=m // 512, n_blocks=d // 512, tm=512, tn=512, tk=half)
    f = matmul(g, w_down, k_block=0, **kw)
    return matmul(g, w_down, k_block=1, acc=f, out_dtype=BF16, **kw)


def kernel(x, c, ctx, c_ctx, ada_w, ada_b, norm_w, ab_w_in, ab_gate_b, ab_sink, ab_head_norm, ab_w_out,
           gla_w_in, gla_gate_w2, gla_gate_b, gla_head_norm, gla_w_out,
           ffn_w_up, ffn_conv_w, ffn_conv_b, ffn_w_down):
    batch, seq, d = x.shape
    ctx_len = ctx.shape[1]
    n_ctx, n_lat = batch * ctx_len, batch * seq
    n_tok = n_ctx + n_lat
    depth = ada_w.shape[0]
    assert depth == 2 and ab_w_in.shape[0] == 1 and gla_w_in.shape[0] == 1
    tr = 256
    tm = math.gcd(math.gcd(n_ctx, seq), 1024)
    assert tm % 512 == 0 and n_ctx % tr == 0 and seq % tr == 0 and batch + 1 <= 8
    assert ctx_len & (ctx_len - 1) == 0 and seq & (seq - 1) == 0

    cvec = jnp.zeros((8, d), F32).at[:batch].set(c).at[batch].set(c_ctx)
    mods = ada_mods(cvec, ada_w, ada_b).reshape(depth, 8, 6, 1, d)

    def mod(i, j):
        return mods[i, :, j]

    ctx_tiles = n_ctx // tr
    lat_tiles_per_seq = seq // tr

    def group_all(t):
        return jnp.where(t < ctx_tiles, batch, (t - ctx_tiles) // lat_tiles_per_seq)

    def group_lat(t):
        return t // lat_tiles_per_seq

    ctx2, x2 = ctx.reshape(n_ctx, d), x.reshape(n_lat, d)

    nw = norm_w[0]
    (h,) = row_update(ctx2, group_all, n_tok, x_tail=x2, nwh=nw[0], shift=mod(0, 0), scale=mod(0, 1), tr=tr)
    n_kv = ab_sink.shape[1] // ATTN_GROUP
    aq, akv = n_kv * ATTN_GROUP * ATTN_HEAD_DIM, n_kv * ATTN_HEAD_DIM
    mqk, mv = MLSTM_HEADS * MLSTM_QK, MLSTM_HEADS * MLSTM_V
    cols = [0, aq, aq + akv, aq + 2 * akv, aq + 2 * akv + mqk, aq + 2 * akv + 2 * mqk,
            aq + 2 * akv + 2 * mqk + mv, aq + 2 * akv + 2 * mqk + 2 * mv]
    wide = cols[-1]
    w_in = jnp.swapaxes(ab_w_in, 1, 2)
    p = matmul(h, w_in, layer=0, m_blocks=n_tok // tm, n_blocks=wide // 512, tm=tm, tn=512,
               out_dtype=BF16, w_is_nk=True)
    gates = matmul(h, w_in, layer=0, m_blocks=n_tok // tm, n_blocks=1, tm=tm, tn=LANE,
                   n_off=wide // LANE, n_valid=ab_w_in.shape[2] - wide, w_is_nk=True)
    cos, sin = _rope_tables(seq)
    sink = ab_sink[0].astype(F32)
    attn = context_attention(p, sink, batch=batch, ctx_len=ctx_len, q_col=cols[0], k_col=cols[1], v_col=cols[2])
    attn = window_attention(p, sink, cos, sin, attn, batch=batch, seq=seq, ctx_len=ctx_len,
                            q_col=cols[0], k_col=cols[1], v_col=cols[2])
    scan = functools.partial(mlstm, p, gates, ab_gate_b[0], batch=batch, seq=seq, ctx_len=ctx_len,
                             q_col=cols[3], k_col=cols[4], v_col=cols[5])
    m_all = scan(reverse=True, other=scan(reverse=False), out_gate_col=cols[6], gain=ab_head_norm[0])
    y = matmul([attn, m_all], ab_w_out, layer=0, m_blocks=n_tok // tm, n_blocks=d // 512, tm=tm, tn=512,
               out_dtype=BF16)
    tok, h = row_update(ctx2, group_all, n_tok, x_tail=x2, y=y, gate=mod(0, 2), nwy=nw[1],
                        nwh=nw[2], shift=mod(0, 3), scale=mod(0, 4), tr=tr)
    f = _conv_ffn(h, ffn_w_up, ffn_conv_w, ffn_conv_b, ffn_w_down, layer=0, n_ctx_rows=n_ctx,
                  ctx_len=ctx_len, seq_len=seq, tm=tm)
    nw1 = norm_w[1]
    tok, h = row_update(tok, group_all, n_tok, y=f, gate=mod(0, 5), nwy=nw[3],
                        nwh=nw1[0], shift=mod(1, 0), scale=mod(1, 1), tr=tr)

    gq = GLA_HEADS * GLA_QK
    gv = GLA_HEADS * GLA_V
    wide = 2 * gq + 2 * gv
    w_in = jnp.swapaxes(gla_w_in, 1, 2)
    p = matmul(h, w_in, layer=0, m_blocks=n_tok // tm, n_blocks=wide // 512, tm=tm, tn=512,
               out_dtype=BF16, w_is_nk=True)
    lowrank = matmul(h, w_in, layer=0, m_blocks=n_tok // tm, n_blocks=1, tm=tm, tn=LANE,
                     n_off=wide // LANE, n_valid=gla_w_in.shape[2] - wide, w_is_nk=True)
    scan = functools.partial(gla, p, lowrank, gla_gate_w2[0], gla_gate_b[0], batch=batch, seq=seq,
                             ctx_len=ctx_len, q_col=0, k_col=gq, v_col=2 * gq)
    o_all = scan(reverse=True, other=scan(reverse=False), out_gate_col=2 * gq + gv, gain=gla_head_norm[0])
    y = matmul(o_all, gla_w_out, layer=0, m_blocks=n_lat // tm, n_blocks=d // 512, tm=tm, tn=512,
               out_dtype=BF16)
    xl, h = row_update(tok, group_lat, n_lat, x_row_off=n_ctx, y=y, gate=mod(1, 2), nwy=nw1[1],
                       nwh=nw1[2], shift=mod(1, 3), scale=mod(1, 4), tr=tr)
    f = _conv_ffn(h, ffn_w_up, ffn_conv_w, ffn_conv_b, ffn_w_down, layer=1, n_ctx_rows=0,
                  ctx_len=ctx_len, seq_len=seq, tm=tm)
    (out,) = row_update(xl, group_lat, n_lat, y=f, gate=mod(1, 5), nwy=nw1[3], tr=tr)
    return out.reshape(batch, seq, d)
```

```python
import functools
import math

import jax
import jax.numpy as jnp
from jax import lax
from jax.experimental import pallas as pl
from jax.experimental.pallas import tpu as pltpu

F32 = jnp.float32
BF16 = jnp.bfloat16

NORM_EPS = 1e-6
ROPE_THETA = 10000.0
GRID_W = 64
ATTN_HEAD_DIM = 128
ATTN_GROUP = 4
ATTN_BLOCK = 128
MLSTM_HEADS = 4
MLSTM_QK = 256
MLSTM_V = 512
MLSTM_CHUNK = 128
GLA_HEADS = 8
GLA_QK = 256
GLA_V = 512
GLA_CHUNK = 64
GLA_GATE_TEMP = 16.0
GLA_GATE_RANK = 16
LANE = 128
MXU_DEPTH = 256
MIB = 1024 * 1024
VMEM_CAP = 60 * MIB


def _cparams(n_axes, vmem_bytes, flags=None):
    return pltpu.CompilerParams(
        dimension_semantics=("arbitrary",) * n_axes,
        vmem_limit_bytes=int(min(VMEM_CAP, vmem_bytes)),
        flags=flags,
    )


def _sigmoid(x):
    return 1.0 / (1.0 + jnp.exp(-x))


def _log_sigmoid(x):
    return jnp.minimum(x, 0.0) - jnp.log(1.0 + jnp.exp(-jnp.abs(x)))


def _split3(x):
    hi = x.astype(BF16)
    r1 = x - hi.astype(F32)
    mid = r1.astype(BF16)
    lo = (r1 - mid.astype(F32)).astype(BF16)
    return hi, mid, lo


def _ada_body(s_ref, w_ref, b_ref, o_ref):
    s = s_ref[...]
    s = (s * _sigmoid(s)).astype(BF16)
    o_ref[...] = jnp.dot(s, w_ref[...].astype(BF16), preferred_element_type=F32) + b_ref[...]


def ada_mods(cvec, ada_w, ada_b, tn=1024):
    depth, d, n = ada_w.shape
    rows = cvec.shape[0]
    return pl.pallas_call(
        _ada_body,
        out_shape=jax.ShapeDtypeStruct((depth, rows, n), F32),
        grid=(depth, n // tn),
        in_specs=[
            pl.BlockSpec((rows, d), lambda i, j: (0, 0)),
            pl.BlockSpec((None, d, tn), lambda i, j: (i, 0, j)),
            pl.BlockSpec((None, 1, tn), lambda i, j: (i, 0, j)),
        ],
        out_specs=pl.BlockSpec((None, rows, tn), lambda i, j: (i, 0, j)),
        compiler_params=_cparams(2, 2 * d * tn * 4 + d * tn * 2 + 8 * MIB),
        name="ada_mods",
    )(cvec, ada_w, ada_b.reshape(depth, 1, n))


def _rms(x, w):
    return x * lax.rsqrt(jnp.mean(x * x, axis=-1, keepdims=True) + NORM_EPS) * w


def _row_body(*refs, has_y, has_h, split):
    it = iter(refs)
    x = next(it)[...]
    if split is not None:
        x = jnp.where(pl.program_id(0) < split, x, next(it)[...])
    if has_y:
        y = next(it)[...].astype(F32)
        gate = next(it)[...]
        nwy = next(it)[...]
    if has_h:
        nwh = next(it)[...]
        shift = next(it)[...]
        scale = next(it)[...]
    if has_y:
        x = x + gate * _rms(y, nwy)
        next(it)[...] = x
    if has_h:
        next(it)[...] = (_rms(x, nwh) * (1.0 + scale) + shift).astype(BF16)


def row_update(x, group_of_tile, n_rows, *, x_tail=None, x_row_off=0, y=None, gate=None, nwy=None,
               nwh=None, shift=None, scale=None, tr=256):
    d = x.shape[1]
    has_y, has_h = y is not None, nwh is not None
    off = x_row_off // tr
    row = pl.BlockSpec((tr, d), lambda t: (t, 0))
    vec = pl.BlockSpec((1, d), lambda t: (0, 0))
    tab = pl.BlockSpec((None, 1, d), lambda t: (group_of_tile(t), 0, 0))
    split = None
    if x_tail is None:
        args, specs = [x], [pl.BlockSpec((tr, d), lambda t: (t + off, 0))]
    else:
        split = x.shape[0] // tr
        args = [x, x_tail]
        specs = [pl.BlockSpec((tr, d), lambda t: (jnp.minimum(t, split - 1), 0)),
                 pl.BlockSpec((tr, d), lambda t: (jnp.maximum(t - split, 0), 0))]
    if has_y:
        args += [y, gate, nwy.reshape(1, d)]
        specs += [row, tab, vec]
    if has_h:
        args += [nwh.reshape(1, d), shift, scale]
        specs += [vec, tab, tab]
    out_shape, out_specs = [], []
    if has_y:
        out_shape.append(jax.ShapeDtypeStruct((n_rows, d), F32))
        out_specs.append(row)
    if has_h:
        out_shape.append(jax.ShapeDtypeStruct((n_rows, d), BF16))
        out_specs.append(row)
    return pl.pallas_call(
        functools.partial(_row_body, has_y=has_y, has_h=has_h, split=split),
        out_shape=out_shape,
        grid=(n_rows // tr,),
        in_specs=specs,
        out_specs=out_specs,
        compiler_params=_cparams(1, 2 * tr * d * 22 + 8 * MIB),
        name="row_update",
    )(*args)


A_SLOTS = 3


def _mm_body(*refs, n_a, n_valid, has_acc, w_is_nk, tm, tk, k_block, row_off):
    a_hbms = refs[:n_a]
    w_ref = refs[n_a]
    acc_ref = refs[n_a + 1] if has_acc else None
    o_ref, wbf_ref = refs[n_a + 1 + has_acc], refs[n_a + 2 + has_acc]
    a_bufs, a_sem = refs[n_a + 3 + has_acc:-1], refs[-1]
    m_blocks = pl.num_programs(1)
    s = pl.program_id(0) * m_blocks + pl.program_id(1)
    last = pl.num_programs(0) * m_blocks - 1

    def a_copies(step):
        row0 = (step % m_blocks) * tm + row_off
        slot = step % A_SLOTS
        return [pltpu.make_async_copy(
            a_hbm.at[pl.ds(row0, tm), pl.ds(k_block * tk, tk) if n_a == 1 else slice(None)],
            a_buf.at[slot], a_sem.at[i, slot]) for i, (a_hbm, a_buf) in enumerate(zip(a_hbms, a_bufs))]

    @pl.when(s == 0)
    def _():
        for c in a_copies(0) + a_copies(1):
            c.start()

    @pl.when(s + 2 <= last)
    def _():
        for c in a_copies(s + 2):
            c.start()

    for c in a_copies(s):
        c.wait()

    @pl.when(pl.program_id(1) == 0)
    def _():
        w = w_ref[...]
        if n_valid is not None:
            col = lax.broadcasted_iota(jnp.int32, w.shape, 0 if w_is_nk else 1)
            w = jnp.where(col < n_valid, w, 0.0)
        wbf_ref[...] = w.astype(BF16)

    r, k0 = None, 0
    for a_buf in a_bufs:
        a = a_buf[s % A_SLOTS]
        k1 = k0 + a.shape[1]
        if w_is_nk:
            part = lax.dot_general(a, wbf_ref[:, k0:k1], (((1,), (1,)), ((), ())), preferred_element_type=F32)
        else:
            part = jnp.dot(a, wbf_ref[k0:k1, :], preferred_element_type=F32)
        r = part if r is None else r + part
        k0 = k1
    if has_acc:
        r = r + acc_ref[...]
    o_ref[...] = r.astype(o_ref.dtype)


def matmul(a, w, *, layer, m_blocks, n_blocks, tm, tn, tk=None, k_block=0, a_row_off=0, n_off=0,
           n_valid=None, acc=None, out_dtype=F32, w_is_nk=False):
    a_list = list(a) if isinstance(a, (list, tuple)) else [a]
    k_total = sum(x.shape[1] for x in a_list)
    tk = k_total if tk is None else tk
    assert len(a_list) == 1 or tk == k_total
    assert n_blocks * m_blocks >= 2
    widths = [tk] if len(a_list) == 1 else [x.shape[1] for x in a_list]
    in_specs = [pl.BlockSpec(memory_space=pl.ANY) for _ in a_list]
    if w_is_nk:
        in_specs.append(pl.BlockSpec((None, tn, tk), lambda n, m: (layer, n + n_off, k_block)))
    else:
        in_specs.append(pl.BlockSpec((None, tk, tn), lambda n, m: (layer, k_block, n + n_off)))
    args = a_list + [w]
    out_spec = pl.BlockSpec((tm, tn), lambda n, m: (m, n))
    if acc is not None:
        in_specs.append(out_spec)
        args.append(acc)
    osz = jnp.dtype(out_dtype).itemsize
    vmem = (A_SLOTS * tm * tk * 2 + 2 * tk * tn * 4 + tk * tn * 2 + 2 * tm * tn * osz
            + (2 * tm * tn * 4 if acc is not None else 0) + 2 * tm * tn * 4 + 4 * MIB)
    return pl.pallas_call(
        functools.partial(_mm_body, n_a=len(a_list), n_valid=n_valid, has_acc=acc is not None, w_is_nk=w_is_nk,
                          tm=tm, tk=tk, k_block=k_block, row_off=a_row_off),
        out_shape=jax.ShapeDtypeStruct((m_blocks * tm, n_blocks * tn), out_dtype),
        grid=(n_blocks, m_blocks),
        in_specs=in_specs,
        out_specs=out_spec,
        scratch_shapes=([pltpu.VMEM((tn, tk) if w_is_nk else (tk, tn), BF16)]
                        + [pltpu.VMEM((A_SLOTS, tm, wd), BF16) for wd in widths]
                        + [pltpu.SemaphoreType.DMA((len(a_list), A_SLOTS))]),
        compiler_params=_cparams(2, vmem),
        name="matmul",
    )(*args)


FFN_LAG = 2
FFN_RING = FFN_LAG + 1


def _ffn_up_body(a_hbm, wg_ref, wv_ref, cwg_ref, cwv_ref, cbg_ref, cbv_ref, o_ref, wg_bf, wv_bf, ug_ref, uv_ref,
                 a_buf, a_sem, *, tm, tiles, total, n_ctx_rows, ctx_len, seq_len):
    s = pl.program_id(0)
    last = total + FFN_LAG - 1
    m = jnp.minimum(s, total - 1) % tiles
    m_out = jnp.maximum(s - FFN_LAG, 0) % tiles

    def a_copy(step):
        tile = jnp.minimum(step, total - 1) % tiles
        slot = step % A_SLOTS
        return pltpu.make_async_copy(a_hbm.at[pl.ds(tile * tm, tm), :], a_buf.at[slot], a_sem.at[slot])

    @pl.when(s == 0)
    def _():
        a_copy(0).start()
        a_copy(1).start()

    @pl.when(s + 2 <= last)
    def _():
        a_copy(s + 2).start()

    a_copy(s).wait()

    @pl.when(s == 0)
    def _():
        ug_ref[...] = jnp.zeros_like(ug_ref)
        uv_ref[...] = jnp.zeros_like(uv_ref)

    @pl.when(m == 0)
    def _():
        wg_bf[...] = wg_ref[...].astype(BF16)
        wv_bf[...] = wv_ref[...].astype(BF16)

    slot_new = s % FFN_RING
    slot_mid = (s + 1) % FFN_RING
    slot_next = (s + 2) % FFN_RING
    k_tiles = a_buf.shape[2] // MXU_DEPTH
    chunk = tm // k_tiles
    unit = min(chunk, ctx_len)
    row8 = lax.broadcasted_iota(jnp.int32, (8, o_ref.shape[1]), 0)

    def starts_sequence(g):
        return (g <= n_ctx_rows) | (((g - n_ctx_rows) & (seq_len - 1)) == 0)

    def conv(u_ref, cw_ref, cb_ref, r0):
        r1 = r0 + unit
        above = u_ref[slot_new, tm - 1:tm, :] if r0 == 0 else u_ref[slot_mid, r0 - 1:r0, :]
        below = u_ref[slot_next, 0:1, :] if r1 == tm else u_ref[slot_mid, r1:r1 + 1, :]
        if r0 % ctx_len == 0:
            above = jnp.where(starts_sequence(m_out * tm + r0), 0.0, above)
        if r1 % ctx_len == 0:
            below = jnp.where(starts_sequence(m_out * tm + r1), 0.0, below)
        u = u_ref[slot_mid, r0:r1, :]
        down, up = pltpu.roll(u, 1, 0), pltpu.roll(u, unit - 1, 0)
        u_prev = jnp.concatenate([jnp.where(row8 == 0, above, down[0:8]), down[8:]], axis=0)
        u_next = jnp.concatenate([up[:unit - 8], jnp.where(row8 == 7, below, up[unit - 8:])], axis=0)
        cw = cw_ref[...]
        return cb_ref[...] + u_prev * cw[0:1] + u * cw[1:2] + u_next * cw[2:3]

    sixteen = jnp.uint32(16)
    zeros = []
    for j in range(k_tiles):
        seen = None
        for r0 in range(j * chunk, (j + 1) * chunk, unit):
            gate = conv(ug_ref, cwg_ref, cbg_ref, r0)
            val = conv(uv_ref, cwv_ref, cbv_ref, r0)
            out = gate * _sigmoid(gate) * val
            o_ref[r0:r0 + unit, :] = out.astype(o_ref.dtype)
            bits = lax.bitcast_convert_type(out, jnp.uint32)
            for r in range(0, unit, 8):
                for c in range(0, bits.shape[1], LANE):
                    blk = bits[r:r + 8, c:c + LANE]
                    seen = blk if seen is None else seen | blk
        zeros.append(pltpu.bitcast(lax.shift_right_logical(lax.shift_right_logical(seen, sixteen), sixteen), BF16))

    def paced(w):
        parts = [w[0:MXU_DEPTH]]
        zr = zeros[0].shape[0]
        for k in range(1, k_tiles):
            blk = w[k * MXU_DEPTH:(k + 1) * MXU_DEPTH]
            head = jnp.concatenate([blk[0:zr, :LANE] + zeros[k - 1], blk[0:zr, LANE:]], axis=1)
            parts.append(jnp.concatenate([head, blk[zr:]], axis=0))
        return jnp.concatenate(parts, axis=0)

    a = a_buf[s % A_SLOTS]
    ug_ref[slot_new] = jnp.dot(a, paced(wg_bf[...]), preferred_element_type=F32)
    uv_ref[slot_new] = jnp.dot(a, paced(wv_bf[...]), preferred_element_type=F32)


def ffn_up(h, w_up, conv_w, conv_b, *, layer, tm, n_ctx_rows, ctx_len, seq_len, tn=256):
    m_rows, k = h.shape
    tiles = m_rows // tm
    d_ff = w_up.shape[2] // 2
    nb = d_ff // tn
    total = nb * tiles
    assert tm % ctx_len == 0 and seq_len % ctx_len == 0 and n_ctx_rows % ctx_len == 0 and tn % LANE == 0
    chunk = tm // (k // MXU_DEPTH)
    assert k % MXU_DEPTH == 0 and chunk % 8 == 0 and (chunk % ctx_len == 0 or ctx_len % chunk == 0)
    depth = w_up.shape[0]
    cb = conv_b.reshape(depth, 1, 2 * d_ff)

    def cur(s):
        sc = jnp.minimum(s, total - 1)
        return sc // tiles, sc % tiles

    def out(s):
        so = jnp.maximum(s - FFN_LAG, 0)
        return so // tiles, so % tiles

    vmem = (A_SLOTS * tm * k * 2 + 2 * 2 * k * tn * 4 + 2 * k * tn * 2 + 2 * FFN_RING * tm * tn * 4
            + 2 * tm * tn * 2 + 12 * tm * tn * 4 + 4 * MIB)
    return pl.pallas_call(
        functools.partial(_ffn_up_body, tm=tm, tiles=tiles, total=total, n_ctx_rows=n_ctx_rows,
                          ctx_len=ctx_len, seq_len=seq_len),
        out_shape=jax.ShapeDtypeStruct((m_rows, d_ff), BF16),
        grid=(total + FFN_LAG,),
        in_specs=[
            pl.BlockSpec(memory_space=pl.ANY),
            pl.BlockSpec((None, k, tn), lambda s: (layer, 0, cur(s)[0])),
            pl.BlockSpec((None, k, tn), lambda s: (layer, 0, cur(s)[0] + nb)),
            pl.BlockSpec((None, 3, tn), lambda s: (layer, 0, out(s)[0])),
            pl.BlockSpec((None, 3, tn), lambda s: (layer, 0, out(s)[0] + nb)),
            pl.BlockSpec((None, 1, tn), lambda s: (layer, 0, out(s)[0])),
            pl.BlockSpec((None, 1, tn), lambda s: (layer, 0, out(s)[0] + nb)),
        ],
        out_specs=pl.BlockSpec((tm, tn), lambda s: (out(s)[1], out(s)[0])),
        scratch_shapes=[pltpu.VMEM((k, tn), BF16), pltpu.VMEM((k, tn), BF16),
                        pltpu.VMEM((FFN_RING, tm, tn), F32), pltpu.VMEM((FFN_RING, tm, tn), F32),
                        pltpu.VMEM((A_SLOTS, tm, k), BF16), pltpu.SemaphoreType.DMA((A_SLOTS,))],
        compiler_params=_cparams(1, vmem),
        name="ffn_up",
    )(h, w_up, w_up, conv_w, conv_w, cb, cb)


def _rope(x, cos, sin_signed):
    lane = lax.broadcasted_iota(jnp.int32, x.shape, 1)
    q = ATTN_HEAD_DIM // 4
    partner = jnp.where((lane & (2 * q - 1)) < q, pltpu.roll(x, ATTN_HEAD_DIM - q, 1), pltpu.roll(x, q, 1))
    return x * cos + partner * sin_signed


def _sink_attention(q4, kcat, vcat, sink_col, bias=None, score_scale=None):
    s = lax.dot_general(q4, kcat, (((1,), (1,)), ((), ())), preferred_element_type=F32)
    if score_scale is not None:
        s = s * score_scale
    if bias is not None:
        s = s + bias
    m = jnp.maximum(jnp.max(s, axis=-1, keepdims=True), sink_col)
    e = jnp.exp(s - m)
    denom = jnp.sum(e, axis=-1, keepdims=True) + jnp.exp(sink_col - m)
    return jnp.dot(e.astype(BF16), vcat, preferred_element_type=F32) * (1.0 / denom)


def _sink_column(sink_ref, j, rows):
    return jnp.concatenate(
        [jnp.full((rows, 1), sink_ref[j * ATTN_GROUP + g], F32) for g in range(ATTN_GROUP)], axis=0)


def _win_attn_body(sink_ref, q_ref, kp_ref, kc_ref, kn_ref, vp_ref, vc_ref, vn_ref, kx_ref, vx_ref,
                   cos_ref, sin_ref, band_ref, prev_ref, o_ref, *, nb, n_kv):
    del prev_ref
    n = pl.program_id(1)
    blk, dh = ATTN_BLOCK, ATTN_HEAD_DIM
    qw = ATTN_GROUP * dh
    scale = dh ** -0.5

    def table(ref, i):
        return ref[pl.ds(pl.multiple_of(i * blk, blk), blk), :]

    n_prev, n_next = jnp.maximum(n - 1, 0), jnp.minimum(n + 1, nb - 1)
    cos_q, sin_q = table(cos_ref, n), table(sin_ref, n)
    cos_p, sin_p = table(cos_ref, n_prev), table(sin_ref, n_prev)
    cos_n, sin_n = table(cos_ref, n_next), table(sin_ref, n_next)
    n_keys = 3 * blk + kx_ref.shape[0]
    kj = lax.broadcasted_iota(jnp.int32, (1, n_keys), 1)
    lo = jnp.where(n > 0, 0, blk)
    hi = jnp.where(n < nb - 1, 3 * blk, 2 * blk)
    bias = band_ref[...] + jnp.where(((kj >= lo) & (kj < hi)) | (kj >= 3 * blk), 0.0, -jnp.inf)

    for j in range(n_kv):
        ksl = slice(j * dh, (j + 1) * dh)
        q = q_ref[:, j * qw:(j + 1) * qw].astype(F32)
        q4 = jnp.concatenate(
            [_rope(q[:, g * dh:(g + 1) * dh], cos_q, sin_q) * scale for g in range(ATTN_GROUP)],
            axis=0).astype(BF16)
        kcat = jnp.concatenate([
            _rope(kp_ref[:, ksl].astype(F32), cos_p, sin_p).astype(BF16),
            _rope(kc_ref[:, ksl].astype(F32), cos_q, sin_q).astype(BF16),
            _rope(kn_ref[:, ksl].astype(F32), cos_n, sin_n).astype(BF16),
            kx_ref[:, ksl]], axis=0)
        vcat = jnp.concatenate([vp_ref[:, ksl], vc_ref[:, ksl], vn_ref[:, ksl], vx_ref[:, ksl]], axis=0)
        o = _sink_attention(q4, kcat, vcat, _sink_column(sink_ref, j, blk), bias=bias)
        for g in range(ATTN_GROUP):
            c0 = j * qw + g * dh
            o_ref[:, c0:c0 + dh] = o[g * blk:(g + 1) * blk].astype(o_ref.dtype)


def window_attention(p, sink, cos, sin_signed, prev, *, batch, seq, ctx_len, q_col, k_col, v_col):
    blk, dh = ATTN_BLOCK, ATTN_HEAD_DIM
    nb = seq // blk
    n_kv = (k_col - q_col) // (ATTN_GROUP * dh)
    lat0 = batch * ctx_len // blk
    qw = n_kv * ATTN_GROUP * dh
    kw = n_kv * dh

    def kv_spec(col, shift):
        return pl.BlockSpec((blk, kw), lambda b, n: (lat0 + b * nb + jnp.clip(n + shift, 0, nb - 1), col // kw))

    rows, n_keys = ATTN_GROUP * blk, 3 * blk + ctx_len
    qi = jnp.arange(rows)[:, None] % blk
    kj = jnp.arange(n_keys)[None, :]
    band = jnp.where(((kj >= qi) & (kj <= qi + 2 * blk)) | (kj >= 3 * blk), 0.0, -jnp.inf).astype(F32)

    in_specs = [
        pl.BlockSpec(memory_space=pltpu.SMEM),
        pl.BlockSpec((blk, qw), lambda b, n: (lat0 + b * nb + n, q_col // qw)),
        kv_spec(k_col, -1), kv_spec(k_col, 0), kv_spec(k_col, 1),
        kv_spec(v_col, -1), kv_spec(v_col, 0), kv_spec(v_col, 1),
        pl.BlockSpec((ctx_len, kw), lambda b, n: (b, k_col // kw)),
        pl.BlockSpec((ctx_len, kw), lambda b, n: (b, v_col // kw)),
        pl.BlockSpec((seq, dh), lambda b, n: (0, 0)),
        pl.BlockSpec((seq, dh), lambda b, n: (0, 0)),
        pl.BlockSpec((rows, n_keys), lambda b, n: (0, 0)),
        pl.BlockSpec(memory_space=pl.ANY),
    ]
    return pl.pallas_call(
        functools.partial(_win_attn_body, nb=nb, n_kv=n_kv),
        out_shape=jax.ShapeDtypeStruct(prev.shape, prev.dtype),
        grid=(batch, nb),
        in_specs=in_specs,
        out_specs=pl.BlockSpec((blk, qw), lambda b, n: (lat0 + b * nb + n, 0)),
        input_output_aliases={13: 0},
        compiler_params=_cparams(2, 32 * MIB),
        name="window_attention",
    )(sink, p, p, p, p, p, p, p, p, p, cos, sin_signed, band, prev)


def _ctx_attn_body(sink_ref, q_ref, k_ref, v_ref, o_ref, *, n_kv, batch):
    dh = ATTN_HEAD_DIM
    qw = ATTN_GROUP * dh
    rows = q_ref.shape[0]

    @pl.when(pl.program_id(0) < batch)
    def _():
        for j in range(n_kv):
            ksl = slice(j * dh, (j + 1) * dh)
            q4 = jnp.concatenate(
                [q_ref[:, j * qw + g * dh:j * qw + (g + 1) * dh] for g in range(ATTN_GROUP)], axis=0)
            o = _sink_attention(q4, k_ref[:, ksl], v_ref[:, ksl], _sink_column(sink_ref, j, rows),
                                score_scale=dh ** -0.5)
            for g in range(ATTN_GROUP):
                c0 = j * qw + g * dh
                o_ref[:, c0:c0 + dh] = o[g * rows:(g + 1) * rows].astype(o_ref.dtype)

    @pl.when(pl.program_id(0) >= batch)
    def _():
        o_ref[...] = jnp.zeros_like(o_ref)


def context_attention(p, sink, *, batch, ctx_len, q_col, k_col, v_col):
    dh = ATTN_HEAD_DIM
    n_kv = (k_col - q_col) // (ATTN_GROUP * dh)
    qw = n_kv * ATTN_GROUP * dh
    kw = n_kv * dh
    last = batch - 1
    return pl.pallas_call(
        functools.partial(_ctx_attn_body, n_kv=n_kv, batch=batch),
        out_shape=jax.ShapeDtypeStruct((p.shape[0], qw), BF16),
        grid=(p.shape[0] // ctx_len,),
        in_specs=[
            pl.BlockSpec(memory_space=pltpu.SMEM),
            pl.BlockSpec((ctx_len, qw), lambda b: (jnp.minimum(b, last), q_col // qw)),
            pl.BlockSpec((ctx_len, kw), lambda b: (jnp.minimum(b, last), k_col // kw)),
            pl.BlockSpec((ctx_len, kw), lambda b: (jnp.minimum(b, last), v_col // kw)),
        ],
        out_specs=pl.BlockSpec((ctx_len, qw), lambda b: (b, 0)),
        compiler_params=_cparams(1, 32 * MIB),
        name="context_attention",
    )(sink, p, p, p)


def _scan_row_block(c, b, batch, n_ctx_blocks, n_lat_blocks, reverse):
    is_ctx = c < n_ctx_blocks
    i = jnp.where(is_ctx, c, c - n_ctx_blocks)
    if reverse:
        i = jnp.where(is_ctx, n_ctx_blocks - 1, n_lat_blocks - 1) - i
    return jnp.where(is_ctx, b * n_ctx_blocks + i, batch * n_ctx_blocks + b * n_lat_blocks + i)


def _head_finish(h_own, h_other, gate, gain, silu_gate):
    h = h_own + h_other.astype(F32)
    hn = h * lax.rsqrt(jnp.mean(h * h, axis=-1, keepdims=True) + NORM_EPS)
    g = gate.astype(F32)
    return hn * gain * (g * _sigmoid(g) if silu_gate else _sigmoid(g))


def _mlstm_body(q_ref, k_ref, v0_ref, v1_ref, g_ref, gt_ref, gb_row_ref, gb_col_ref, *rest, reverse, finish):
    if finish:
        other_ref, og0_ref, og1_ref, gain_ref, o_ref, cn_ref, m_ref = rest
    else:
        o_ref, cn_ref, m_ref = rest
    c = pl.program_id(1)
    ch, dk, dv, nh = MLSTM_CHUNK, MLSTM_QK, MLSTM_V, MLSTM_HEADS

    @pl.when(c == 0)
    def _():
        cn_ref[...] = jnp.zeros_like(cn_ref)
        m_ref[...] = jnp.zeros_like(m_ref)

    t = lax.broadcasted_iota(jnp.int32, (ch, ch), 0)
    s = lax.broadcasted_iota(jnp.int32, (ch, ch), 1)
    valid = (s >= t) if reverse else (s <= t)
    tri = jnp.where(valid, 1.0, 0.0).astype(BF16)
    last_t = 0 if reverse else ch - 1
    col_i = 2 * nh if reverse else 0
    col_f = 3 * nh if reverse else nh

    g_col = g_ref[...] + gb_row_ref[...]
    g_row = gt_ref[...] + gb_col_ref[...]
    lf_col, lf_row = _log_sigmoid(g_col), _log_sigmoid(g_row)
    bc_col_all = sum(jnp.dot(tri, part, preferred_element_type=F32) for part in _split3(lf_col))
    bc_row_all = sum(lax.dot_general(part, tri, (((1,), (1,)), ((), ())), preferred_element_type=F32)
                     for part in _split3(lf_row))
    ones_col = jnp.where(lax.broadcasted_iota(jnp.int32, (ch, LANE), 1) == 0, 1.0, 0.0)

    for h in range(nh):
        ig_col = g_col[:, col_i + h:col_i + h + 1]
        ig_row = g_row[col_i + h:col_i + h + 1, :]
        bc_col = bc_col_all[:, col_f + h:col_f + h + 1]
        bc_row = bc_row_all[col_f + h:col_f + h + 1, :]
        m_st = m_ref[h:h + 1, 0:1]
        q = q_ref[:, h * dk:(h + 1) * dk]
        k = k_ref[:, h * dk:(h + 1) * dk] * (dk ** -0.5)
        v_ref = v0_ref if h < nh // 2 else v1_ref
        hv = h % (nh // 2)
        v = v_ref[:, hv * dv:(hv + 1) * dv]

        log_d = jnp.where(valid, bc_col - bc_row + ig_row, -jnp.inf)
        inter = bc_col + m_st
        m_t = jnp.maximum(inter, jnp.max(log_d, axis=-1, keepdims=True))
        sc = lax.dot_general(q, k, (((1,), (1,)), ((), ())), preferred_element_type=F32) * jnp.exp(log_d - m_t)
        w_inter = jnp.exp(inter - m_t)
        state = cn_ref[h]
        q_state = jnp.dot(q, state.astype(BF16), preferred_element_type=F32)
        num = jnp.dot(sc.astype(BF16), v, preferred_element_type=F32) + w_inter * q_state[:, :dv]
        den = jnp.sum(sc, axis=-1, keepdims=True) + w_inter * q_state[:, dv:dv + 1]
        h_t = num * (1.0 / jnp.maximum(jnp.abs(den), jnp.exp(-m_t)))
        if finish:
            og_ref = og0_ref if h < nh // 2 else og1_ref
            h_t = _head_finish(h_t, other_ref[:, h * dv:(h + 1) * dv], og_ref[:, hv * dv:(hv + 1) * dv],
                               gain_ref[:, h * dv:(h + 1) * dv], False)
        o_ref[:, h * dv:(h + 1) * dv] = h_t.astype(o_ref.dtype)

        b_end = bc_col[last_t:last_t + 1, :]
        m_new = m_t[last_t:last_t + 1, :]
        wk = jnp.exp(b_end - bc_col + ig_col - m_new)
        decay = jnp.exp(b_end + m_st - m_new)
        v_ext = jnp.concatenate([v.astype(F32), ones_col], axis=1) * wk
        cn_ref[h] = decay * state + lax.dot_general(
            k, v_ext.astype(BF16), (((0,), (0,)), ((), ())), preferred_element_type=F32)
        m_ref[h:h + 1, :] = jnp.broadcast_to(m_new, (1, LANE))


def mlstm(p, gates, gate_b, *, reverse, batch, seq, ctx_len, q_col, k_col, v_col,
          other=None, out_gate_col=None, gain=None):
    ch, nh, dk, dv = MLSTM_CHUNK, MLSTM_HEADS, MLSTM_QK, MLSTM_V
    ncc, nlc = ctx_len // ch, seq // ch
    rows = p.shape[0]
    blk = functools.partial(_scan_row_block, batch=batch, n_ctx_blocks=ncc, n_lat_blocks=nlc, reverse=reverse)
    gb = jnp.zeros((LANE,), F32).at[:4 * nh].set(gate_b.astype(F32))
    qw, vw = nh * dk, nh * dv // 2
    finish = other is not None
    in_specs = [
        pl.BlockSpec((ch, qw), lambda b, c: (blk(c, b), q_col // qw)),
        pl.BlockSpec((ch, qw), lambda b, c: (blk(c, b), k_col // qw)),
        pl.BlockSpec((ch, vw), lambda b, c: (blk(c, b), v_col // vw)),
        pl.BlockSpec((ch, vw), lambda b, c: (blk(c, b), v_col // vw + 1)),
        pl.BlockSpec((ch, LANE), lambda b, c: (blk(c, b), 0)),
        pl.BlockSpec((LANE, ch), lambda b, c: (0, blk(c, b))),
        pl.BlockSpec((1, LANE), lambda b, c: (0, 0)),
        pl.BlockSpec((LANE, 1), lambda b, c: (0, 0)),
    ]
    args = [p, p, p, p, gates, gates.T, gb.reshape(1, LANE), gb.reshape(LANE, 1)]
    if finish:
        in_specs += [
            pl.BlockSpec((ch, nh * dv), lambda b, c: (blk(c, b), 0)),
            pl.BlockSpec((ch, vw), lambda b, c: (blk(c, b), out_gate_col // vw)),
            pl.BlockSpec((ch, vw), lambda b, c: (blk(c, b), out_gate_col // vw + 1)),
            pl.BlockSpec((1, nh * dv), lambda b, c: (0, 0)),
        ]
        args += [other, p, p, gain.reshape(1, nh * dv).astype(F32)]
    return pl.pallas_call(
        functools.partial(_mlstm_body, reverse=reverse, finish=finish),
        out_shape=jax.ShapeDtypeStruct((rows, nh * dv), BF16),
        grid=(batch, ncc + nlc),
        in_specs=in_specs,
        out_specs=pl.BlockSpec((ch, nh * dv), lambda b, c: (blk(c, b), 0)),
        scratch_shapes=[pltpu.VMEM((nh, dk, dv + LANE), F32), pltpu.VMEM((8, LANE), F32)],
        compiler_params=_cparams(2, 32 * MIB),
        name="mlstm",
    )(*args)


GLA_BLOCK = 2 * GLA_CHUNK


def _gla_body(q_ref, k_ref, v_ref, lr_ref, w2_ref, gb_ref, *rest, reverse, finish):
    if finish:
        other_ref, og_ref, gain_ref, o_ref, st_ref = rest
    else:
        o_ref, st_ref = rest
    c = pl.program_id(1)
    ch, rows, dk, dv, nh = GLA_CHUNK, GLA_BLOCK, GLA_QK, GLA_V, GLA_HEADS

    @pl.when(c == 0)
    def _():
        st_ref[...] = jnp.zeros_like(st_ref)

    t = lax.broadcasted_iota(jnp.int32, (rows, rows), 0)
    s = lax.broadcasted_iota(jnp.int32, (rows, rows), 1)
    same_chunk = (t ^ s) < ch
    intra = same_chunk & ((s >= t) if reverse else (s <= t))
    cross = ((t < ch) & (s >= ch)) if reverse else ((t >= ch) & (s < ch))
    tri = jnp.where(intra, 1.0, 0.0).astype(BF16)
    row = lax.broadcasted_iota(jnp.int32, (rows, 1), 0)
    in_a = (row >= ch) if reverse else (row < ch)

    pre = jnp.dot(lr_ref[...].astype(BF16), w2_ref[...].astype(BF16), preferred_element_type=F32)
    lg = _log_sigmoid(pre + gb_ref[...]) / GLA_GATE_TEMP
    gcum = sum(jnp.dot(tri, part, preferred_element_type=F32) for part in _split3(lg))
    end_a = gcum[ch:ch + 1, :] if reverse else gcum[ch - 1:ch, :]
    end_b = gcum[0:1, :] if reverse else gcum[rows - 1:rows, :]
    d_a, d_b = jnp.exp(end_a), jnp.exp(end_b)
    e_pos, e_neg = jnp.exp(gcum), jnp.exp(-gcum)
    e_end = jnp.exp(jnp.where(in_a, end_a, end_b) - gcum)
    e_pos_state = e_pos * jnp.where(in_a, 1.0, d_a)
    e_end_state = e_end * jnp.where(in_a, d_b, 1.0)
    decay = d_a * d_b

    for h in range(nh):
        sl = slice(h * dk, (h + 1) * dk)
        q = q_ref[:, sl].astype(F32) * (dk ** -0.5)
        k = k_ref[:, sl].astype(F32)
        v = v_ref[:, h * dv:(h + 1) * dv]
        q_dec = (q * e_pos[:, sl]).astype(BF16)
        q_state = (q * e_pos_state[:, sl]).astype(BF16)
        keys = jnp.concatenate([(k * e_neg[:, sl]).astype(BF16), (k * e_end[:, sl]).astype(BF16)], axis=0)
        k_state = (k * e_end_state[:, sl]).astype(BF16)
        sc = lax.dot_general(q_dec, keys, (((1,), (1,)), ((), ())), preferred_element_type=F32)
        att = jnp.where(intra, sc[:, :rows], jnp.where(cross, sc[:, rows:], 0.0)).astype(BF16)
        st = st_ref[h]
        o = (jnp.dot(att, v, preferred_element_type=F32)
             + lax.dot_general(q_state, st.astype(BF16), (((1,), (1,)), ((), ())), preferred_element_type=F32))
        hsl = slice(h * dv, (h + 1) * dv)
        if finish:
            o = _head_finish(o, other_ref[:, hsl], og_ref[:, hsl], gain_ref[:, hsl], True)
        o_ref[:, hsl] = o.astype(o_ref.dtype)
        st_ref[h] = decay[:, sl] * st + lax.dot_general(
            v, k_state, (((0,), (0,)), ((), ())), preferred_element_type=F32)


def gla(p, lowrank, gate_w2, gate_b, *, reverse, batch, seq, ctx_len, q_col, k_col, v_col,
        other=None, out_gate_col=None, gain=None):
    rows, nh, dk, dv = GLA_BLOCK, GLA_HEADS, GLA_QK, GLA_V
    ncb, nlb = ctx_len // rows, seq // rows
    rank = GLA_GATE_RANK
    d = 1 if reverse else 0
    blk = functools.partial(_scan_row_block, batch=batch, n_ctx_blocks=ncb, n_lat_blocks=nlb, reverse=reverse)

    def lat_blk(c, b):
        return blk(jnp.maximum(c, ncb), b)

    def out_blk(c, b):
        return lat_blk(c, b) - batch * ncb

    w2 = jnp.zeros((LANE, nh * dk), F32).at[d * rank:(d + 1) * rank].set(gate_w2[d])
    finish = other is not None
    in_specs = [
        pl.BlockSpec((rows, nh * dk), lambda b, c: (blk(c, b), q_col // (nh * dk))),
        pl.BlockSpec((rows, nh * dk), lambda b, c: (blk(c, b), k_col // (nh * dk))),
        pl.BlockSpec((rows, nh * dv), lambda b, c: (blk(c, b), v_col // (nh * dv))),
        pl.BlockSpec((rows, LANE), lambda b, c: (blk(c, b), 0)),
        pl.BlockSpec((LANE, nh * dk), lambda b, c: (0, 0)),
        pl.BlockSpec((1, nh * dk), lambda b, c: (0, 0)),
    ]
    args = [p, p, p, lowrank, w2, gate_b[d].reshape(1, nh * dk)]
    if finish:
        in_specs += [
            pl.BlockSpec((rows, nh * dv), lambda b, c: (out_blk(c, b), 0)),
            pl.BlockSpec((rows, nh * dv), lambda b, c: (lat_blk(c, b), out_gate_col // (nh * dv))),
            pl.BlockSpec((1, nh * dv), lambda b, c: (0, 0)),
        ]
        args += [other, p, gain.reshape(1, nh * dv).astype(F32)]
    return pl.pallas_call(
        functools.partial(_gla_body, reverse=reverse, finish=finish),
        out_shape=jax.ShapeDtypeStruct((batch * seq, nh * dv), BF16),
        grid=(batch, ncb + nlb),
        in_specs=in_specs,
        out_specs=pl.BlockSpec((rows, nh * dv), lambda b, c: (out_blk(c, b), 0)),
        scratch_shapes=[pltpu.VMEM((nh, dv, dk), F32)],
        compiler_params=_cparams(2, 44 * MIB),
        name="gla",
    )(*args)


def _rope_tables(seq):
    rows = seq // GRID_W
    row = jnp.repeat(jnp.arange(rows), GRID_W).astype(F32)
    col = jnp.tile(jnp.arange(GRID_W), rows).astype(F32)
    n_freq = ATTN_HEAD_DIM // 4
    inv = ROPE_THETA ** (-jnp.arange(n_freq, dtype=F32) / n_freq)
    ang_r, ang_c = row[:, None] * inv, col[:, None] * inv
    cos = jnp.concatenate([jnp.cos(ang_r), jnp.cos(ang_r), jnp.cos(ang_c), jnp.cos(ang_c)], axis=-1)
    sin = jnp.concatenate([-jnp.sin(ang_r), jnp.sin(ang_r), -jnp.sin(ang_c), jnp.sin(ang_c)], axis=-1)
    return cos, sin


def _conv_ffn(h, w_up, conv_w, conv_b, w_down, *, layer, n_ctx_rows, ctx_len, seq_len, tm):
    g = ffn_up(h, w_up, conv_w, conv_b, layer=layer, tm=tm, n_ctx_rows=n_ctx_rows, ctx_len=ctx_len,
               seq_len=seq_len)
    m, d_ff = g.shape
    d = w_down.shape[2]
    half = d_ff // 2
    kw = dict(layer=layer, m_blocks=m // 512, n_blocks=d // 512, tm=512, tn=512, tk=half)
    f = matmul(g, w_down, k_block=0, **kw)
    return matmul(g, w_down, k_block=1, acc=f, out_dtype=BF16, **kw)


def kernel(x, c, ctx, c_ctx, ada_w, ada_b, norm_w, ab_w_in, ab_gate_b, ab_sink, ab_head_norm, ab_w_out,
           gla_w_in, gla_gate_w2, gla_gate_b, gla_head_norm, gla_w_out,
           ffn_w_up, ffn_conv_w, ffn_conv_b, ffn_w_down):
    batch, seq, d = x.shape
    ctx_len = ctx.shape[1]
    n_ctx, n_lat = batch * ctx_len, batch * seq
    n_tok = n_ctx + n_lat
    depth = ada_w.shape[0]
    assert depth == 2 and ab_w_in.shape[0] == 1 and gla_w_in.shape[0] == 1
    tr = 256
    tm = math.gcd(math.gcd(n_ctx, seq), 1024)
    assert tm % 512 == 0 and n_ctx % tr == 0 and seq % tr == 0 and batch + 1 <= 8
    assert ctx_len & (ctx_len - 1) == 0 and seq & (seq - 1) == 0

    cvec = jnp.zeros((8, d), F32).at[:batch].set(c).at[batch].set(c_ctx)
    mods = ada_mods(cvec, ada_w, ada_b).reshape(depth, 8, 6, 1, d)

    def mod(i, j):
        return mods[i, :, j]

    ctx_tiles = n_ctx // tr
    lat_tiles_per_seq = seq // tr

    def group_all(t):
        return jnp.where(t < ctx_tiles, batch, (t - ctx_tiles) // lat_tiles_per_seq)

    def group_lat(t):
        return t // lat_tiles_per_seq

    ctx2, x2 = ctx.reshape(n_ctx, d), x.reshape(n_lat, d)

    nw = norm_w[0]
    (h,) = row_update(ctx2, group_all, n_tok, x_tail=x2, nwh=nw[0], shift=mod(0, 0), scale=mod(0, 1), tr=tr)
    n_kv = ab_sink.shape[1] // ATTN_GROUP
    aq, akv = n_kv * ATTN_GROUP * ATTN_HEAD_DIM, n_kv * ATTN_HEAD_DIM
    mqk, mv = MLSTM_HEADS * MLSTM_QK, MLSTM_HEADS * MLSTM_V
    cols = [0, aq, aq + akv, aq + 2 * akv, aq + 2 * akv + mqk, aq + 2 * akv + 2 * mqk,
            aq + 2 * akv + 2 * mqk + mv, aq + 2 * akv + 2 * mqk + 2 * mv]
    wide = cols[-1]
    w_in = jnp.swapaxes(ab_w_in, 1, 2)
    p = matmul(h, w_in, layer=0, m_blocks=n_tok // tm, n_blocks=wide // 512, tm=tm, tn=512,
               out_dtype=BF16, w_is_nk=True)
    gates = matmul(h, w_in, layer=0, m_blocks=n_tok // tm, n_blocks=1, tm=tm, tn=LANE,
                   n_off=wide // LANE, n_valid=ab_w_in.shape[2] - wide, w_is_nk=True)
    cos, sin = _rope_tables(seq)
    sink = ab_sink[0].astype(F32)
    attn = context_attention(p, sink, batch=batch, ctx_len=ctx_len, q_col=cols[0], k_col=cols[1], v_col=cols[2])
    attn = window_attention(p, sink, cos, sin, attn, batch=batch, seq=seq, ctx_len=ctx_len,
                            q_col=cols[0], k_col=cols[1], v_col=cols[2])
    scan = functools.partial(mlstm, p, gates, ab_gate_b[0], batch=batch, seq=seq, ctx_len=ctx_len,
                             q_col=cols[3], k_col=cols[4], v_col=cols[5])
    m_all = scan(reverse=True, other=scan(reverse=False), out_gate_col=cols[6], gain=ab_head_norm[0])
    y = matmul([attn, m_all], ab_w_out, layer=0, m_blocks=n_tok // tm, n_blocks=d // 512, tm=tm, tn=512,
               out_dtype=BF16)
    tok, h = row_update(ctx2, group_all, n_tok, x_tail=x2, y=y, gate=mod(0, 2), nwy=nw[1],
                        nwh=nw[2], shift=mod(0, 3), scale=mod(0, 4), tr=tr)
    f = _conv_ffn(h, ffn_w_up, ffn_conv_w, ffn_conv_b, ffn_w_down, layer=0, n_ctx_rows=n_ctx,
                  ctx_len=ctx_len, seq_len=seq, tm=tm)
    nw1 = norm_w[1]
    tok, h = row_update(tok, group_all, n_tok, y=f, gate=mod(0, 5), nwy=nw[3],
                        nwh=nw1[0], shift=mod(1, 0), scale=mod(1, 1), tr=tr)

    gq = GLA_HEADS * GLA_QK
    gv = GLA_HEADS * GLA_V
    wide = 2 * gq + 2 * gv
    w_in = jnp.swapaxes(gla_w_in, 1, 2)
    p = matmul(h, w_in, layer=0, m_blocks=n_tok // tm, n_blocks=wide // 512, tm=tm, tn=512,
               out_dtype=BF16, w_is_nk=True)
    lowrank = matmul(h, w_in, layer=0, m_blocks=n_tok // tm, n_blocks=1, tm=tm, tn=LANE,
                     n_off=wide // LANE, n_valid=gla_w_in.shape[2] - wide, w_is_nk=True)
    scan = functools.partial(gla, p, lowrank, gla_gate_w2[0], gla_gate_b[0], batch=batch, seq=seq,
                             ctx_len=ctx_len, q_col=0, k_col=gq, v_col=2 * gq)
    o_all = scan(reverse=True, other=scan(reverse=False), out_gate_col=2 * gq + gv, gain=gla_head_norm[0])
    y = matmul(o_all, gla_w_out, layer=0, m_blocks=n_lat // tm, n_blocks=d // 512, tm=tm, tn=512,
               out_dtype=BF16)
    xl, h = row_update(tok, group_lat, n_lat, x_row_off=n_ctx, y=y, gate=mod(1, 2), nwy=nw1[1],
                       nwh=nw1[2], shift=mod(1, 3), scale=mod(1, 4), tr=tr)
    f = _conv_ffn(h, ffn_w_up, ffn_conv_w, ffn_conv_b, ffn_w_down, layer=1, n_ctx_rows=0,
                  ctx_len=ctx_len, seq_len=seq, tm=tm)
    (out,) = row_update(xl, group_lat, n_lat, y=f, gate=mod(1, 5), nwy=nw1[3], tr=tr)
    return out.reshape(batch, seq, d)
```

```python
import functools
import math

import jax
import jax.numpy as jnp
from jax import lax
from jax.experimental import pallas as pl
from jax.experimental.pallas import tpu as pltpu

F32 = jnp.float32
BF16 = jnp.bfloat16

NORM_EPS = 1e-6
ROPE_THETA = 10000.0
GRID_W = 64
ATTN_HEAD_DIM = 128
ATTN_GROUP = 4
ATTN_BLOCK = 128
MLSTM_HEADS = 4
MLSTM_QK = 256
MLSTM_V = 512
MLSTM_CHUNK = 128
GLA_HEADS = 8
GLA_QK = 256
GLA_V = 512
GLA_CHUNK = 64
GLA_GATE_TEMP = 16.0
GLA_GATE_RANK = 16
LANE = 128
MXU_DEPTH = 256
MIB = 1024 * 1024
VMEM_CAP = 60 * MIB


def _cparams(n_axes, vmem_bytes, flags=None):
    return pltpu.CompilerParams(
        dimension_semantics=("arbitrary",) * n_axes,
        vmem_limit_bytes=int(min(VMEM_CAP, vmem_bytes)),
        flags=flags,
    )


def _sigmoid(x):
    return 1.0 / (1.0 + jnp.exp(-x))


def _log_sigmoid(x):
    return jnp.minimum(x, 0.0) - jnp.log(1.0 + jnp.exp(-jnp.abs(x)))


def _split3(x):
    hi = x.astype(BF16)
    r1 = x - hi.astype(F32)
    mid = r1.astype(BF16)
    lo = (r1 - mid.astype(F32)).astype(BF16)
    return hi, mid, lo


def _ada_body(s_ref, w_ref, b_ref, o_ref):
    s = s_ref[...]
    s = (s * _sigmoid(s)).astype(BF16)
    o_ref[...] = jnp.dot(s, w_ref[...].astype(BF16), preferred_element_type=F32) + b_ref[...]


def ada_mods(cvec, ada_w, ada_b, tn=1024):
    depth, d, n = ada_w.shape
    rows = cvec.shape[0]
    return pl.pallas_call(
        _ada_body,
        out_shape=jax.ShapeDtypeStruct((depth, rows, n), F32),
        grid=(depth, n // tn),
        in_specs=[
            pl.BlockSpec((rows, d), lambda i, j: (0, 0)),
            pl.BlockSpec((None, d, tn), lambda i, j: (i, 0, j)),
            pl.BlockSpec((None, 1, tn), lambda i, j: (i, 0, j)),
        ],
        out_specs=pl.BlockSpec((None, rows, tn), lambda i, j: (i, 0, j)),
        compiler_params=_cparams(2, 2 * d * tn * 4 + d * tn * 2 + 8 * MIB),
        name="ada_mods",
    )(cvec, ada_w, ada_b.reshape(depth, 1, n))


def _rms(x, w):
    return x * lax.rsqrt(jnp.mean(x * x, axis=-1, keepdims=True) + NORM_EPS) * w


def _row_body(*refs, has_y, has_h, split):
    it = iter(refs)
    x = next(it)[...]
    if split is not None:
        x = jnp.where(pl.program_id(0) < split, x, next(it)[...])
    if has_y:
        y = next(it)[...].astype(F32)
        gate = next(it)[...]
        nwy = next(it)[...]
    if has_h:
        nwh = next(it)[...]
        shift = next(it)[...]
        scale = next(it)[...]
    if has_y:
        x = x + gate * _rms(y, nwy)
        next(it)[...] = x
    if has_h:
        next(it)[...] = (_rms(x, nwh) * (1.0 + scale) + shift).astype(BF16)


def row_update(x, group_of_tile, n_rows, *, x_tail=None, x_row_off=0, y=None, gate=None, nwy=None,
               nwh=None, shift=None, scale=None, tr=256):
    d = x.shape[1]
    has_y, has_h = y is not None, nwh is not None
    off = x_row_off // tr
    row = pl.BlockSpec((tr, d), lambda t: (t, 0))
    vec = pl.BlockSpec((1, d), lambda t: (0, 0))
    tab = pl.BlockSpec((None, 1, d), lambda t: (group_of_tile(t), 0, 0))
    split = None
    if x_tail is None:
        args, specs = [x], [pl.BlockSpec((tr, d), lambda t: (t + off, 0))]
    else:
        split = x.shape[0] // tr
        args = [x, x_tail]
        specs = [pl.BlockSpec((tr, d), lambda t: (jnp.minimum(t, split - 1), 0)),
                 pl.BlockSpec((tr, d), lambda t: (jnp.maximum(t - split, 0), 0))]
    if has_y:
        args += [y, gate, nwy.reshape(1, d)]
        specs += [row, tab, vec]
    if has_h:
        args += [nwh.reshape(1, d), shift, scale]
        specs += [vec, tab, tab]
    out_shape, out_specs = [], []
    if has_y:
        out_shape.append(jax.ShapeDtypeStruct((n_rows, d), F32))
        out_specs.append(row)
    if has_h:
        out_shape.append(jax.ShapeDtypeStruct((n_rows, d), BF16))
        out_specs.append(row)
    return pl.pallas_call(
        functools.partial(_row_body, has_y=has_y, has_h=has_h, split=split),
        out_shape=out_shape,
        grid=(n_rows // tr,),
        in_specs=specs,
        out_specs=out_specs,
        compiler_params=_cparams(1, 2 * tr * d * 22 + 8 * MIB),
        name="row_update",
    )(*args)


A_SLOTS = 3


def _mm_body(*refs, n_a, n_valid, has_acc, w_is_nk, tm, tk, k_block, row_off, w_copy):
    a_hbms = refs[:n_a]
    w_ref = refs[n_a]
    acc_ref = refs[n_a + 1] if has_acc else None
    o_ref, wbf_ref = refs[n_a + 1 + has_acc], refs[n_a + 2 + has_acc]
    rest = refs[n_a + 3 + has_acc:]
    if w_copy is not None:
        w_stage, w_sem, rest = rest[-2], rest[-1], rest[:-2]
    a_bufs, a_sem = rest[:-1], rest[-1]
    n_blk, m_blocks = pl.program_id(0), pl.num_programs(1)
    s = n_blk * m_blocks + pl.program_id(1)
    last = pl.num_programs(0) * m_blocks - 1

    def a_copies(step):
        row0 = (step % m_blocks) * tm + row_off
        slot = step % A_SLOTS
        return [pltpu.make_async_copy(
            a_hbm.at[pl.ds(row0, tm), pl.ds(k_block * tk, tk) if n_a == 1 else slice(None)],
            a_buf.at[slot], a_sem.at[i, slot]) for i, (a_hbm, a_buf) in enumerate(zip(a_hbms, a_bufs))]

    @pl.when(s == 0)
    def _():
        for c in a_copies(0) + a_copies(1):
            c.start()

    @pl.when(s + 2 <= last)
    def _():
        for c in a_copies(s + 2):
            c.start()

    for c in a_copies(s):
        c.wait()

    if w_copy is not None:
        def w_dma(blk):
            return pltpu.make_async_copy(w_copy(w_ref, blk), w_stage.at[blk % 2], w_sem.at[blk % 2])

        @pl.when(s == 0)
        def _():
            w_dma(0).start()

    @pl.when(pl.program_id(1) == 0)
    def _():
        if w_copy is not None:
            @pl.when(n_blk + 1 < pl.num_programs(0))
            def _():
                w_dma(n_blk + 1).start()

            w_dma(n_blk).wait()
            w = w_stage[n_blk % 2]
        else:
            w = w_ref[...]
        if n_valid is not None:
            col = lax.broadcasted_iota(jnp.int32, w.shape, 0 if w_is_nk else 1)
            w = jnp.where(col < n_valid, w, 0.0)
        wbf_ref[...] = w.astype(BF16)

    r, k0 = None, 0
    for a_buf in a_bufs:
        a = a_buf[s % A_SLOTS]
        k1 = k0 + a.shape[1]
        if w_is_nk:
            part = lax.dot_general(a, wbf_ref[:, k0:k1], (((1,), (1,)), ((), ())), preferred_element_type=F32)
        else:
            part = jnp.dot(a, wbf_ref[k0:k1, :], preferred_element_type=F32)
        r = part if r is None else r + part
        k0 = k1
    if has_acc:
        r = r + acc_ref[...]
    o_ref[...] = r.astype(o_ref.dtype)


def matmul(a, w, *, layer, m_blocks, n_blocks, tm, tn, tk=None, k_block=0, a_row_off=0, n_off=0,
           n_valid=None, acc=None, out_dtype=F32, w_is_nk=False):
    a_list = list(a) if isinstance(a, (list, tuple)) else [a]
    k_total = sum(x.shape[1] for x in a_list)
    tk = k_total if tk is None else tk
    assert len(a_list) == 1 or tk == k_total
    assert n_blocks * m_blocks >= 2
    widths = [tk] if len(a_list) == 1 else [x.shape[1] for x in a_list]
    in_specs = [pl.BlockSpec(memory_space=pl.ANY) for _ in a_list]
    w_block = (tn, tk) if w_is_nk else (tk, tn)
    w_copy, w_scratch = None, []
    if n_valid is not None:
        in_specs.append(pl.BlockSpec((None,) + w_block, (lambda n, m: (layer, n + n_off, k_block)) if w_is_nk
                                     else (lambda n, m: (layer, k_block, n + n_off))))
    else:
        in_specs.append(pl.BlockSpec(memory_space=pl.ANY))
        w_scratch = [pltpu.VMEM((2,) + w_block, w.dtype), pltpu.SemaphoreType.DMA((2,))]

        def w_copy(w_hbm, blk):
            rows, cols = pl.ds(k_block * tk, tk), pl.ds((blk + n_off) * tn, tn)
            return w_hbm.at[layer, cols, rows] if w_is_nk else w_hbm.at[layer, rows, cols]
    args = a_list + [w]
    out_spec = pl.BlockSpec((tm, tn), lambda n, m: (m, n))
    if acc is not None:
        in_specs.append(out_spec)
        args.append(acc)
    osz = jnp.dtype(out_dtype).itemsize
    vmem = (A_SLOTS * tm * tk * 2 + 2 * tk * tn * 4 + tk * tn * 2 + 2 * tm * tn * osz
            + (2 * tm * tn * 4 if acc is not None else 0) + 2 * tm * tn * 4 + 4 * MIB)
    return pl.pallas_call(
        functools.partial(_mm_body, n_a=len(a_list), n_valid=n_valid, has_acc=acc is not None, w_is_nk=w_is_nk,
                          tm=tm, tk=tk, k_block=k_block, row_off=a_row_off, w_copy=w_copy),
        out_shape=jax.ShapeDtypeStruct((m_blocks * tm, n_blocks * tn), out_dtype),
        grid=(n_blocks, m_blocks),
        in_specs=in_specs,
        out_specs=out_spec,
        scratch_shapes=([pltpu.VMEM(w_block, BF16)]
                        + [pltpu.VMEM((A_SLOTS, tm, wd), BF16) for wd in widths]
                        + [pltpu.SemaphoreType.DMA((len(a_list), A_SLOTS))] + w_scratch),
        compiler_params=_cparams(2, vmem),
        name="matmul",
    )(*args)


FFN_LAG = 2
FFN_RING = FFN_LAG + 1


def _ffn_up_body(a_hbm, w_hbm, cwg_ref, cwv_ref, cbg_ref, cbv_ref, o_ref, wg_bf, wv_bf, ug_ref, uv_ref,
                 a_buf, a_sem, w_stage, w_sem, *, layer, nb, tm, tiles, total, n_ctx_rows, ctx_len, seq_len):
    s = pl.program_id(0)
    last = total + FFN_LAG - 1
    m = jnp.minimum(s, total - 1) % tiles
    m_out = jnp.maximum(s - FFN_LAG, 0) % tiles

    def a_copy(step):
        tile = jnp.minimum(step, total - 1) % tiles
        slot = step % A_SLOTS
        return pltpu.make_async_copy(a_hbm.at[pl.ds(tile * tm, tm), :], a_buf.at[slot], a_sem.at[slot])

    @pl.when(s == 0)
    def _():
        a_copy(0).start()
        a_copy(1).start()

    @pl.when(s + 2 <= last)
    def _():
        a_copy(s + 2).start()

    a_copy(s).wait()

    @pl.when(s == 0)
    def _():
        ug_ref[...] = jnp.zeros_like(ug_ref)
        uv_ref[...] = jnp.zeros_like(uv_ref)

    n_blk = jnp.minimum(s, total - 1) // tiles
    tn = wg_bf.shape[1]

    def w_dmas(blk):
        slot = blk % 2
        return [pltpu.make_async_copy(w_hbm.at[layer, :, pl.ds((half * nb + blk) * tn, tn)],
                                      w_stage.at[slot, half], w_sem.at[slot, half]) for half in range(2)]

    @pl.when(s == 0)
    def _():
        for c in w_dmas(0):
            c.start()

    @pl.when((m == 0) & (s < total))
    def _():
        @pl.when(n_blk + 1 < nb)
        def _():
            for c in w_dmas(n_blk + 1):
                c.start()

        for c in w_dmas(n_blk):
            c.wait()
        wg_bf[...] = w_stage[n_blk % 2, 0].astype(BF16)
        wv_bf[...] = w_stage[n_blk % 2, 1].astype(BF16)

    slot_new = s % FFN_RING
    slot_mid = (s + 1) % FFN_RING
    slot_next = (s + 2) % FFN_RING
    k_tiles = a_buf.shape[2] // MXU_DEPTH
    chunk = tm // k_tiles
    unit = min(chunk, ctx_len)
    row8 = lax.broadcasted_iota(jnp.int32, (8, o_ref.shape[1]), 0)

    def starts_sequence(g):
        return (g <= n_ctx_rows) | (((g - n_ctx_rows) & (seq_len - 1)) == 0)

    def conv(u_ref, cw_ref, cb_ref, r0):
        r1 = r0 + unit
        above = u_ref[slot_new, tm - 1:tm, :] if r0 == 0 else u_ref[slot_mid, r0 - 1:r0, :]
        below = u_ref[slot_next, 0:1, :] if r1 == tm else u_ref[slot_mid, r1:r1 + 1, :]
        if r0 % ctx_len == 0:
            above = jnp.where(starts_sequence(m_out * tm + r0), 0.0, above)
        if r1 % ctx_len == 0:
            below = jnp.where(starts_sequence(m_out * tm + r1), 0.0, below)
        u = u_ref[slot_mid, r0:r1, :]
        down, up = pltpu.roll(u, 1, 0), pltpu.roll(u, unit - 1, 0)
        u_prev = jnp.concatenate([jnp.where(row8 == 0, above, down[0:8]), down[8:]], axis=0)
        u_next = jnp.concatenate([up[:unit - 8], jnp.where(row8 == 7, below, up[unit - 8:])], axis=0)
        cw = cw_ref[...]
        return cb_ref[...] + u_prev * cw[0:1] + u * cw[1:2] + u_next * cw[2:3]

    sixteen = jnp.uint32(16)
    zeros = []
    for j in range(k_tiles):
        seen = None
        for r0 in range(j * chunk, (j + 1) * chunk, unit):
            gate = conv(ug_ref, cwg_ref, cbg_ref, r0)
            val = conv(uv_ref, cwv_ref, cbv_ref, r0)
            out = gate * _sigmoid(gate) * val
            o_ref[r0:r0 + unit, :] = out.astype(o_ref.dtype)
            bits = lax.bitcast_convert_type(out, jnp.uint32)
            for r in range(0, unit, 8):
                for c in range(0, bits.shape[1], LANE):
                    blk = bits[r:r + 8, c:c + LANE]
                    seen = blk if seen is None else seen | blk
        zeros.append(pltpu.bitcast(lax.shift_right_logical(lax.shift_right_logical(seen, sixteen), sixteen), BF16))

    def paced(w):
        parts = [w[0:MXU_DEPTH]]
        zr = zeros[0].shape[0]
        for k in range(1, k_tiles):
            blk = w[k * MXU_DEPTH:(k + 1) * MXU_DEPTH]
            head = jnp.concatenate([blk[0:zr, :LANE] + zeros[k - 1], blk[0:zr, LANE:]], axis=1)
            parts.append(jnp.concatenate([head, blk[zr:]], axis=0))
        return jnp.concatenate(parts, axis=0)

    a = a_buf[s % A_SLOTS]
    ug_ref[slot_new] = jnp.dot(a, paced(wg_bf[...]), preferred_element_type=F32)
    uv_ref[slot_new] = jnp.dot(a, paced(wv_bf[...]), preferred_element_type=F32)


def ffn_up(h, w_up, conv_w, conv_b, *, layer, tm, n_ctx_rows, ctx_len, seq_len, tn=256):
    m_rows, k = h.shape
    tiles = m_rows // tm
    d_ff = w_up.shape[2] // 2
    nb = d_ff // tn
    total = nb * tiles
    assert tm % ctx_len == 0 and seq_len % ctx_len == 0 and n_ctx_rows % ctx_len == 0 and tn % LANE == 0
    chunk = tm // (k // MXU_DEPTH)
    assert k % MXU_DEPTH == 0 and chunk % 8 == 0 and (chunk % ctx_len == 0 or ctx_len % chunk == 0)
    depth = w_up.shape[0]
    cb = conv_b.reshape(depth, 1, 2 * d_ff)

    def out(s):
        so = jnp.maximum(s - FFN_LAG, 0)
        return so // tiles, so % tiles

    vmem = (A_SLOTS * tm * k * 2 + 2 * 2 * k * tn * 4 + 2 * k * tn * 2 + 2 * FFN_RING * tm * tn * 4
            + 2 * tm * tn * 2 + 12 * tm * tn * 4 + 4 * MIB)
    return pl.pallas_call(
        functools.partial(_ffn_up_body, layer=layer, nb=nb, tm=tm, tiles=tiles, total=total,
                          n_ctx_rows=n_ctx_rows, ctx_len=ctx_len, seq_len=seq_len),
        out_shape=jax.ShapeDtypeStruct((m_rows, d_ff), BF16),
        grid=(total + FFN_LAG,),
        in_specs=[
            pl.BlockSpec(memory_space=pl.ANY),
            pl.BlockSpec(memory_space=pl.ANY),
            pl.BlockSpec((None, 3, tn), lambda s: (layer, 0, out(s)[0])),
            pl.BlockSpec((None, 3, tn), lambda s: (layer, 0, out(s)[0] + nb)),
            pl.BlockSpec((None, 1, tn), lambda s: (layer, 0, out(s)[0])),
            pl.BlockSpec((None, 1, tn), lambda s: (layer, 0, out(s)[0] + nb)),
        ],
        out_specs=pl.BlockSpec((tm, tn), lambda s: (out(s)[1], out(s)[0])),
        scratch_shapes=[pltpu.VMEM((k, tn), BF16), pltpu.VMEM((k, tn), BF16),
                        pltpu.VMEM((FFN_RING, tm, tn), F32), pltpu.VMEM((FFN_RING, tm, tn), F32),
                        pltpu.VMEM((A_SLOTS, tm, k), BF16), pltpu.SemaphoreType.DMA((A_SLOTS,)),
                        pltpu.VMEM((2, 2, k, tn), w_up.dtype), pltpu.SemaphoreType.DMA((2, 2))],
        compiler_params=_cparams(1, vmem),
        name="ffn_up",
    )(h, w_up, conv_w, conv_w, cb, cb)


def _rope(x, cos, sin_signed):
    lane = lax.broadcasted_iota(jnp.int32, x.shape, 1)
    q = ATTN_HEAD_DIM // 4
    partner = jnp.where((lane & (2 * q - 1)) < q, pltpu.roll(x, ATTN_HEAD_DIM - q, 1), pltpu.roll(x, q, 1))
    return x * cos + partner * sin_signed


def _sink_attention(q4, kcat, vcat, sink_col, bias=None, score_scale=None):
    s = lax.dot_general(q4, kcat, (((1,), (1,)), ((), ())), preferred_element_type=F32)
    if score_scale is not None:
        s = s * score_scale
    if bias is not None:
        s = s + bias
    m = jnp.maximum(jnp.max(s, axis=-1, keepdims=True), sink_col)
    e = jnp.exp(s - m)
    denom = jnp.sum(e, axis=-1, keepdims=True) + jnp.exp(sink_col - m)
    return jnp.dot(e.astype(BF16), vcat, preferred_element_type=F32) * (1.0 / denom)


def _sink_column(sink_ref, j, rows):
    return jnp.concatenate(
        [jnp.full((rows, 1), sink_ref[j * ATTN_GROUP + g], F32) for g in range(ATTN_GROUP)], axis=0)


def _win_attn_body(sink_ref, q_ref, kp_ref, kc_ref, kn_ref, vp_ref, vc_ref, vn_ref, kx_ref, vx_ref,
                   cos_ref, sin_ref, band_ref, prev_ref, o_ref, *, nb, n_kv):
    del prev_ref
    n = pl.program_id(1)
    blk, dh = ATTN_BLOCK, ATTN_HEAD_DIM
    qw = ATTN_GROUP * dh
    scale = dh ** -0.5

    def table(ref, i):
        return ref[pl.ds(pl.multiple_of(i * blk, blk), blk), :]

    n_prev, n_next = jnp.maximum(n - 1, 0), jnp.minimum(n + 1, nb - 1)
    cos_q, sin_q = table(cos_ref, n), table(sin_ref, n)
    cos_p, sin_p = table(cos_ref, n_prev), table(sin_ref, n_prev)
    cos_n, sin_n = table(cos_ref, n_next), table(sin_ref, n_next)
    n_keys = 3 * blk + kx_ref.shape[0]
    kj = lax.broadcasted_iota(jnp.int32, (1, n_keys), 1)
    lo = jnp.where(n > 0, 0, blk)
    hi = jnp.where(n < nb - 1, 3 * blk, 2 * blk)
    bias = band_ref[...] + jnp.where(((kj >= lo) & (kj < hi)) | (kj >= 3 * blk), 0.0, -jnp.inf)

    for j in range(n_kv):
        ksl = slice(j * dh, (j + 1) * dh)
        q = q_ref[:, j * qw:(j + 1) * qw].astype(F32)
        q4 = jnp.concatenate(
            [_rope(q[:, g * dh:(g + 1) * dh], cos_q, sin_q) * scale for g in range(ATTN_GROUP)],
            axis=0).astype(BF16)
        kcat = jnp.concatenate([
            _rope(kp_ref[:, ksl].astype(F32), cos_p, sin_p).astype(BF16),
            _rope(kc_ref[:, ksl].astype(F32), cos_q, sin_q).astype(BF16),
            _rope(kn_ref[:, ksl].astype(F32), cos_n, sin_n).astype(BF16),
            kx_ref[:, ksl]], axis=0)
        vcat = jnp.concatenate([vp_ref[:, ksl], vc_ref[:, ksl], vn_ref[:, ksl], vx_ref[:, ksl]], axis=0)
        o = _sink_attention(q4, kcat, vcat, _sink_column(sink_ref, j, blk), bias=bias)
        for g in range(ATTN_GROUP):
            c0 = j * qw + g * dh
            o_ref[:, c0:c0 + dh] = o[g * blk:(g + 1) * blk].astype(o_ref.dtype)


def window_attention(p, sink, cos, sin_signed, prev, *, batch, seq, ctx_len, q_col, k_col, v_col):
    blk, dh = ATTN_BLOCK, ATTN_HEAD_DIM
    nb = seq // blk
    n_kv = (k_col - q_col) // (ATTN_GROUP * dh)
    lat0 = batch * ctx_len // blk
    qw = n_kv * ATTN_GROUP * dh
    kw = n_kv * dh

    def kv_spec(col, shift):
        return pl.BlockSpec((blk, kw), lambda b, n: (lat0 + b * nb + jnp.clip(n + shift, 0, nb - 1), col // kw))

    rows, n_keys = ATTN_GROUP * blk, 3 * blk + ctx_len
    qi = jnp.arange(rows)[:, None] % blk
    kj = jnp.arange(n_keys)[None, :]
    band = jnp.where(((kj >= qi) & (kj <= qi + 2 * blk)) | (kj >= 3 * blk), 0.0, -jnp.inf).astype(F32)

    in_specs = [
        pl.BlockSpec(memory_space=pltpu.SMEM),
        pl.BlockSpec((blk, qw), lambda b, n: (lat0 + b * nb + n, q_col // qw)),
        kv_spec(k_col, -1), kv_spec(k_col, 0), kv_spec(k_col, 1),
        kv_spec(v_col, -1), kv_spec(v_col, 0), kv_spec(v_col, 1),
        pl.BlockSpec((ctx_len, kw), lambda b, n: (b, k_col // kw)),
        pl.BlockSpec((ctx_len, kw), lambda b, n: (b, v_col // kw)),
        pl.BlockSpec((seq, dh), lambda b, n: (0, 0)),
        pl.BlockSpec((seq, dh), lambda b, n: (0, 0)),
        pl.BlockSpec((rows, n_keys), lambda b, n: (0, 0)),
        pl.BlockSpec(memory_space=pl.ANY),
    ]
    return pl.pallas_call(
        functools.partial(_win_attn_body, nb=nb, n_kv=n_kv),
        out_shape=jax.ShapeDtypeStruct(prev.shape, prev.dtype),
        grid=(batch, nb),
        in_specs=in_specs,
        out_specs=pl.BlockSpec((blk, qw), lambda b, n: (lat0 + b * nb + n, 0)),
        input_output_aliases={13: 0},
        compiler_params=_cparams(2, 32 * MIB),
        name="window_attention",
    )(sink, p, p, p, p, p, p, p, p, p, cos, sin_signed, band, prev)


def _ctx_attn_body(sink_ref, q_ref, k_ref, v_ref, o_ref, *, n_kv, batch):
    dh = ATTN_HEAD_DIM
    qw = ATTN_GROUP * dh
    rows = q_ref.shape[0]

    @pl.when(pl.program_id(0) < batch)
    def _():
        for j in range(n_kv):
            ksl = slice(j * dh, (j + 1) * dh)
            q4 = jnp.concatenate(
                [q_ref[:, j * qw + g * dh:j * qw + (g + 1) * dh] for g in range(ATTN_GROUP)], axis=0)
            o = _sink_attention(q4, k_ref[:, ksl], v_ref[:, ksl], _sink_column(sink_ref, j, rows),
                                score_scale=dh ** -0.5)
            for g in range(ATTN_GROUP):
                c0 = j * qw + g * dh
                o_ref[:, c0:c0 + dh] = o[g * rows:(g + 1) * rows].astype(o_ref.dtype)

    @pl.when(pl.program_id(0) >= batch)
    def _():
        o_ref[...] = jnp.zeros_like(o_ref)


def context_attention(p, sink, *, batch, ctx_len, q_col, k_col, v_col):
    dh = ATTN_HEAD_DIM
    n_kv = (k_col - q_col) // (ATTN_GROUP * dh)
    qw = n_kv * ATTN_GROUP * dh
    kw = n_kv * dh
    last = batch - 1
    return pl.pallas_call(
        functools.partial(_ctx_attn_body, n_kv=n_kv, batch=batch),
        out_shape=jax.ShapeDtypeStruct((p.shape[0], qw), BF16),
        grid=(p.shape[0] // ctx_len,),
        in_specs=[
            pl.BlockSpec(memory_space=pltpu.SMEM),
            pl.BlockSpec((ctx_len, qw), lambda b: (jnp.minimum(b, last), q_col // qw)),
            pl.BlockSpec((ctx_len, kw), lambda b: (jnp.minimum(b, last), k_col // kw)),
            pl.BlockSpec((ctx_len, kw), lambda b: (jnp.minimum(b, last), v_col // kw)),
        ],
        out_specs=pl.BlockSpec((ctx_len, qw), lambda b: (b, 0)),
        compiler_params=_cparams(1, 32 * MIB),
        name="context_attention",
    )(sink, p, p, p)


def _scan_row_block(c, b, batch, n_ctx_blocks, n_lat_blocks, reverse):
    is_ctx = c < n_ctx_blocks
    i = jnp.where(is_ctx, c, c - n_ctx_blocks)
    if reverse:
        i = jnp.where(is_ctx, n_ctx_blocks - 1, n_lat_blocks - 1) - i
    return jnp.where(is_ctx, b * n_ctx_blocks + i, batch * n_ctx_blocks + b * n_lat_blocks + i)


def _head_finish(h_own, h_other, gate, gain, silu_gate):
    h = h_own + h_other.astype(F32)
    hn = h * lax.rsqrt(jnp.mean(h * h, axis=-1, keepdims=True) + NORM_EPS)
    g = gate.astype(F32)
    return hn * gain * (g * _sigmoid(g) if silu_gate else _sigmoid(g))


def _mlstm_body(q_ref, k_ref, v0_ref, v1_ref, g_ref, gt_ref, gb_row_ref, gb_col_ref, *rest, reverse, finish):
    if finish:
        other_ref, og0_ref, og1_ref, gain_ref, o_ref, cn_ref, m_ref = rest
    else:
        o_ref, cn_ref, m_ref = rest
    c = pl.program_id(1)
    ch, dk, dv, nh = MLSTM_CHUNK, MLSTM_QK, MLSTM_V, MLSTM_HEADS

    @pl.when(c == 0)
    def _():
        cn_ref[...] = jnp.zeros_like(cn_ref)
        m_ref[...] = jnp.zeros_like(m_ref)

    t = lax.broadcasted_iota(jnp.int32, (ch, ch), 0)
    s = lax.broadcasted_iota(jnp.int32, (ch, ch), 1)
    valid = (s >= t) if reverse else (s <= t)
    tri = jnp.where(valid, 1.0, 0.0).astype(BF16)
    last_t = 0 if reverse else ch - 1
    col_i = 2 * nh if reverse else 0
    col_f = 3 * nh if reverse else nh

    g_col = g_ref[...] + gb_row_ref[...]
    g_row = gt_ref[...] + gb_col_ref[...]
    lf_col, lf_row = _log_sigmoid(g_col), _log_sigmoid(g_row)
    bc_col_all = sum(jnp.dot(tri, part, preferred_element_type=F32) for part in _split3(lf_col))
    bc_row_all = sum(lax.dot_general(part, tri, (((1,), (1,)), ((), ())), preferred_element_type=F32)
                     for part in _split3(lf_row))
    ones_col = jnp.where(lax.broadcasted_iota(jnp.int32, (ch, LANE), 1) == 0, 1.0, 0.0)

    for h in range(nh):
        ig_col = g_col[:, col_i + h:col_i + h + 1]
        ig_row = g_row[col_i + h:col_i + h + 1, :]
        bc_col = bc_col_all[:, col_f + h:col_f + h + 1]
        bc_row = bc_row_all[col_f + h:col_f + h + 1, :]
        m_st = m_ref[h:h + 1, 0:1]
        q = q_ref[:, h * dk:(h + 1) * dk]
        k = k_ref[:, h * dk:(h + 1) * dk] * (dk ** -0.5)
        v_ref = v0_ref if h < nh // 2 else v1_ref
        hv = h % (nh // 2)
        v = v_ref[:, hv * dv:(hv + 1) * dv]

        log_d = jnp.where(valid, bc_col - bc_row + ig_row, -jnp.inf)
        inter = bc_col + m_st
        m_t = jnp.maximum(inter, jnp.max(log_d, axis=-1, keepdims=True))
        sc = lax.dot_general(q, k, (((1,), (1,)), ((), ())), preferred_element_type=F32) * jnp.exp(log_d - m_t)
        w_inter = jnp.exp(inter - m_t)
        state = cn_ref[h]
        q_state = jnp.dot(q, state.astype(BF16), preferred_element_type=F32)
        num = jnp.dot(sc.astype(BF16), v, preferred_element_type=F32) + w_inter * q_state[:, :dv]
        den = jnp.sum(sc, axis=-1, keepdims=True) + w_inter * q_state[:, dv:dv + 1]
        h_t = num * (1.0 / jnp.maximum(jnp.abs(den), jnp.exp(-m_t)))
        if finish:
            og_ref = og0_ref if h < nh // 2 else og1_ref
            h_t = _head_finish(h_t, other_ref[:, h * dv:(h + 1) * dv], og_ref[:, hv * dv:(hv + 1) * dv],
                               gain_ref[:, h * dv:(h + 1) * dv], False)
        o_ref[:, h * dv:(h + 1) * dv] = h_t.astype(o_ref.dtype)

        b_end = bc_col[last_t:last_t + 1, :]
        m_new = m_t[last_t:last_t + 1, :]
        wk = jnp.exp(b_end - bc_col + ig_col - m_new)
        decay = jnp.exp(b_end + m_st - m_new)
        v_ext = jnp.concatenate([v.astype(F32), ones_col], axis=1) * wk
        cn_ref[h] = decay * state + lax.dot_general(
            k, v_ext.astype(BF16), (((0,), (0,)), ((), ())), preferred_element_type=F32)
        m_ref[h:h + 1, :] = jnp.broadcast_to(m_new, (1, LANE))


def mlstm(p, gates, gate_b, *, reverse, batch, seq, ctx_len, q_col, k_col, v_col,
          other=None, out_gate_col=None, gain=None):
    ch, nh, dk, dv = MLSTM_CHUNK, MLSTM_HEADS, MLSTM_QK, MLSTM_V
    ncc, nlc = ctx_len // ch, seq // ch
    rows = p.shape[0]
    blk = functools.partial(_scan_row_block, batch=batch, n_ctx_blocks=ncc, n_lat_blocks=nlc, reverse=reverse)
    gb = jnp.zeros((LANE,), F32).at[:4 * nh].set(gate_b.astype(F32))
    qw, vw = nh * dk, nh * dv // 2
    finish = other is not None
    in_specs = [
        pl.BlockSpec((ch, qw), lambda b, c: (blk(c, b), q_col // qw)),
        pl.BlockSpec((ch, qw), lambda b, c: (blk(c, b), k_col // qw)),
        pl.BlockSpec((ch, vw), lambda b, c: (blk(c, b), v_col // vw)),
        pl.BlockSpec((ch, vw), lambda b, c: (blk(c, b), v_col // vw + 1)),
        pl.BlockSpec((ch, LANE), lambda b, c: (blk(c, b), 0)),
        pl.BlockSpec((LANE, ch), lambda b, c: (0, blk(c, b))),
        pl.BlockSpec((1, LANE), lambda b, c: (0, 0)),
        pl.BlockSpec((LANE, 1), lambda b, c: (0, 0)),
    ]
    args = [p, p, p, p, gates, gates.T, gb.reshape(1, LANE), gb.reshape(LANE, 1)]
    if finish:
        in_specs += [
            pl.BlockSpec((ch, nh * dv), lambda b, c: (blk(c, b), 0)),
            pl.BlockSpec((ch, vw), lambda b, c: (blk(c, b), out_gate_col // vw)),
            pl.BlockSpec((ch, vw), lambda b, c: (blk(c, b), out_gate_col // vw + 1)),
            pl.BlockSpec((1, nh * dv), lambda b, c: (0, 0)),
        ]
        args += [other, p, p, gain.reshape(1, nh * dv).astype(F32)]
    return pl.pallas_call(
        functools.partial(_mlstm_body, reverse=reverse, finish=finish),
        out_shape=jax.ShapeDtypeStruct((rows, nh * dv), BF16),
        grid=(batch, ncc + nlc),
        in_specs=in_specs,
        out_specs=pl.BlockSpec((ch, nh * dv), lambda b, c: (blk(c, b), 0)),
        scratch_shapes=[pltpu.VMEM((nh, dk, dv + LANE), F32), pltpu.VMEM((8, LANE), F32)],
        compiler_params=_cparams(2, 32 * MIB),
        name="mlstm",
    )(*args)


GLA_BLOCK = 2 * GLA_CHUNK


def _gla_body(q_ref, k_ref, v_ref, lr_ref, w2_ref, gb_ref, *rest, reverse, finish):
    if finish:
        other_ref, og_ref, gain_ref, o_ref, st_ref = rest
    else:
        o_ref, st_ref = rest
    c = pl.program_id(1)
    ch, rows, dk, dv, nh = GLA_CHUNK, GLA_BLOCK, GLA_QK, GLA_V, GLA_HEADS

    @pl.when(c == 0)
    def _():
        st_ref[...] = jnp.zeros_like(st_ref)

    t = lax.broadcasted_iota(jnp.int32, (rows, rows), 0)
    s = lax.broadcasted_iota(jnp.int32, (rows, rows), 1)
    same_chunk = (t ^ s) < ch
    intra = same_chunk & ((s >= t) if reverse else (s <= t))
    cross = ((t < ch) & (s >= ch)) if reverse else ((t >= ch) & (s < ch))
    tri = jnp.where(intra, 1.0, 0.0).astype(BF16)
    row = lax.broadcasted_iota(jnp.int32, (rows, 1), 0)
    in_a = (row >= ch) if reverse else (row < ch)

    pre = jnp.dot(lr_ref[...].astype(BF16), w2_ref[...].astype(BF16), preferred_element_type=F32)
    lg = _log_sigmoid(pre + gb_ref[...]) / GLA_GATE_TEMP
    gcum = sum(jnp.dot(tri, part, preferred_element_type=F32) for part in _split3(lg))
    end_a = gcum[ch:ch + 1, :] if reverse else gcum[ch - 1:ch, :]
    end_b = gcum[0:1, :] if reverse else gcum[rows - 1:rows, :]
    d_a, d_b = jnp.exp(end_a), jnp.exp(end_b)
    e_pos, e_neg = jnp.exp(gcum), jnp.exp(-gcum)
    e_end = jnp.exp(jnp.where(in_a, end_a, end_b) - gcum)
    e_pos_state = e_pos * jnp.where(in_a, 1.0, d_a)
    e_end_state = e_end * jnp.where(in_a, d_b, 1.0)
    decay = d_a * d_b

    for h in range(nh):
        sl = slice(h * dk, (h + 1) * dk)
        q = q_ref[:, sl].astype(F32) * (dk ** -0.5)
        k = k_ref[:, sl].astype(F32)
        v = v_ref[:, h * dv:(h + 1) * dv]
        q_dec = (q * e_pos[:, sl]).astype(BF16)
        q_state = (q * e_pos_state[:, sl]).astype(BF16)
        keys = jnp.concatenate([(k * e_neg[:, sl]).astype(BF16), (k * e_end[:, sl]).astype(BF16)], axis=0)
        k_state = (k * e_end_state[:, sl]).astype(BF16)
        sc = lax.dot_general(q_dec, keys, (((1,), (1,)), ((), ())), preferred_element_type=F32)
        att = jnp.where(intra, sc[:, :rows], jnp.where(cross, sc[:, rows:], 0.0)).astype(BF16)
        st = st_ref[h]
        o = (jnp.dot(att, v, preferred_element_type=F32)
             + lax.dot_general(q_state, st.astype(BF16), (((1,), (1,)), ((), ())), preferred_element_type=F32))
        hsl = slice(h * dv, (h + 1) * dv)
        if finish:
            o = _head_finish(o, other_ref[:, hsl], og_ref[:, hsl], gain_ref[:, hsl], True)
        o_ref[:, hsl] = o.astype(o_ref.dtype)
        st_ref[h] = decay[:, sl] * st + lax.dot_general(
            v, k_state, (((0,), (0,)), ((), ())), preferred_element_type=F32)


def gla(p, lowrank, gate_w2, gate_b, *, reverse, batch, seq, ctx_len, q_col, k_col, v_col,
        other=None, out_gate_col=None, gain=None):
    rows, nh, dk, dv = GLA_BLOCK, GLA_HEADS, GLA_QK, GLA_V
    ncb, nlb = ctx_len // rows, seq // rows
    rank = GLA_GATE_RANK
    d = 1 if reverse else 0
    blk = functools.partial(_scan_row_block, batch=batch, n_ctx_blocks=ncb, n_lat_blocks=nlb, reverse=reverse)

    def lat_blk(c, b):
        return blk(jnp.maximum(c, ncb), b)

    def out_blk(c, b):
        return lat_blk(c, b) - batch * ncb

    w2 = jnp.zeros((LANE, nh * dk), F32).at[d * rank:(d + 1) * rank].set(gate_w2[d])
    finish = other is not None
    in_specs = [
        pl.BlockSpec((rows, nh * dk), lambda b, c: (blk(c, b), q_col // (nh * dk))),
        pl.BlockSpec((rows, nh * dk), lambda b, c: (blk(c, b), k_col // (nh * dk))),
        pl.BlockSpec((rows, nh * dv), lambda b, c: (blk(c, b), v_col // (nh * dv))),
        pl.BlockSpec((rows, LANE), lambda b, c: (blk(c, b), 0)),
        pl.BlockSpec((LANE, nh * dk), lambda b, c: (0, 0)),
        pl.BlockSpec((1, nh * dk), lambda b, c: (0, 0)),
    ]
    args = [p, p, p, lowrank, w2, gate_b[d].reshape(1, nh * dk)]
    if finish:
        in_specs += [
            pl.BlockSpec((rows, nh * dv), lambda b, c: (out_blk(c, b), 0)),
            pl.BlockSpec((rows, nh * dv), lambda b, c: (lat_blk(c, b), out_gate_col // (nh * dv))),
            pl.BlockSpec((1, nh * dv), lambda b, c: (0, 0)),
        ]
        args += [other, p, gain.reshape(1, nh * dv).astype(F32)]
    return pl.pallas_call(
        functools.partial(_gla_body, reverse=reverse, finish=finish),
        out_shape=jax.ShapeDtypeStruct((batch * seq, nh * dv), BF16),
        grid=(batch, ncb + nlb),
        in_specs=in_specs,
        out_specs=pl.BlockSpec((rows, nh * dv), lambda b, c: (out_blk(c, b), 0)),
        scratch_shapes=[pltpu.VMEM((nh, dv, dk), F32)],
        compiler_params=_cparams(2, 44 * MIB),
        name="gla",
    )(*args)


def _rope_tables(seq):
    rows = seq // GRID_W
    row = jnp.repeat(jnp.arange(rows), GRID_W).astype(F32)
    col = jnp.tile(jnp.arange(GRID_W), rows).astype(F32)
    n_freq = ATTN_HEAD_DIM // 4
    inv = ROPE_THETA ** (-jnp.arange(n_freq, dtype=F32) / n_freq)
    ang_r, ang_c = row[:, None] * inv, col[:, None] * inv
    cos = jnp.concatenate([jnp.cos(ang_r), jnp.cos(ang_r), jnp.cos(ang_c), jnp.cos(ang_c)], axis=-1)
    sin = jnp.concatenate([-jnp.sin(ang_r), jnp.sin(ang_r), -jnp.sin(ang_c), jnp.sin(ang_c)], axis=-1)
    return cos, sin


def _conv_ffn(h, w_up, conv_w, conv_b, w_down, *, layer, n_ctx_rows, ctx_len, seq_len, tm):
    g = ffn_up(h, w_up, conv_w, conv_b, layer=layer, tm=tm, n_ctx_rows=n_ctx_rows, ctx_len=ctx_len,
               seq_len=seq_len)
    m, d_ff = g.shape
    d = w_down.shape[2]
    half = d_ff // 2
    kw = dict(layer=layer, m_blocks=m // 512, n_blocks=d // 512, tm=512, tn=512, tk=half)
    f = matmul(g, w_down, k_block=0, **kw)
    return matmul(g, w_down, k_block=1, acc=f, out_dtype=BF16, **kw)


def kernel(x, c, ctx, c_ctx, ada_w, ada_b, norm_w, ab_w_in, ab_gate_b, ab_sink, ab_head_norm, ab_w_out,
           gla_w_in, gla_gate_w2, gla_gate_b, gla_head_norm, gla_w_out,
           ffn_w_up, ffn_conv_w, ffn_conv_b, ffn_w_down):
    batch, seq, d = x.shape
    ctx_len = ctx.shape[1]
    n_ctx, n_lat = batch * ctx_len, batch * seq
    n_tok = n_ctx + n_lat
    depth = ada_w.shape[0]
    assert depth == 2 and ab_w_in.shape[0] == 1 and gla_w_in.shape[0] == 1
    tr = 256
    tm = math.gcd(math.gcd(n_ctx, seq), 1024)
    assert tm % 512 == 0 and n_ctx % tr == 0 and seq % tr == 0 and batch + 1 <= 8
    assert ctx_len & (ctx_len - 1) == 0 and seq & (seq - 1) == 0

    cvec = jnp.zeros((8, d), F32).at[:batch].set(c).at[batch].set(c_ctx)
    mods = ada_mods(cvec, ada_w, ada_b).reshape(depth, 8, 6, 1, d)

    def mod(i, j):
        return mods[i, :, j]

    ctx_tiles = n_ctx // tr
    lat_tiles_per_seq = seq // tr

    def group_all(t):
        return jnp.where(t < ctx_tiles, batch, (t - ctx_tiles) // lat_tiles_per_seq)

    def group_lat(t):
        return t // lat_tiles_per_seq

    ctx2, x2 = ctx.reshape(n_ctx, d), x.reshape(n_lat, d)

    nw = norm_w[0]
    (h,) = row_update(ctx2, group_all, n_tok, x_tail=x2, nwh=nw[0], shift=mod(0, 0), scale=mod(0, 1), tr=tr)
    n_kv = ab_sink.shape[1] // ATTN_GROUP
    aq, akv = n_kv * ATTN_GROUP * ATTN_HEAD_DIM, n_kv * ATTN_HEAD_DIM
    mqk, mv = MLSTM_HEADS * MLSTM_QK, MLSTM_HEADS * MLSTM_V
    cols = [0, aq, aq + akv, aq + 2 * akv, aq + 2 * akv + mqk, aq + 2 * akv + 2 * mqk,
            aq + 2 * akv + 2 * mqk + mv, aq + 2 * akv + 2 * mqk + 2 * mv]
    wide = cols[-1]
    w_in = jnp.swapaxes(ab_w_in, 1, 2)
    p = matmul(h, w_in, layer=0, m_blocks=n_tok // tm, n_blocks=wide // 512, tm=tm, tn=512,
               out_dtype=BF16, w_is_nk=True)
    gates = matmul(h, w_in, layer=0, m_blocks=n_tok // tm, n_blocks=1, tm=tm, tn=LANE,
                   n_off=wide // LANE, n_valid=ab_w_in.shape[2] - wide, w_is_nk=True)
    cos, sin = _rope_tables(seq)
    sink = ab_sink[0].astype(F32)
    attn = context_attention(p, sink, batch=batch, ctx_len=ctx_len, q_col=cols[0], k_col=cols[1], v_col=cols[2])
    attn = window_attention(p, sink, cos, sin, attn, batch=batch, seq=seq, ctx_len=ctx_len,
                            q_col=cols[0], k_col=cols[1], v_col=cols[2])
    scan = functools.partial(mlstm, p, gates, ab_gate_b[0], batch=batch, seq=seq, ctx_len=ctx_len,
                             q_col=cols[3], k_col=cols[4], v_col=cols[5])
    m_all = scan(reverse=True, other=scan(reverse=False), out_gate_col=cols[6], gain=ab_head_norm[0])
    y = matmul([attn, m_all], ab_w_out, layer=0, m_blocks=n_tok // tm, n_blocks=d // 512, tm=tm, tn=512,
               out_dtype=BF16)
    tok, h = row_update(ctx2, group_all, n_tok, x_tail=x2, y=y, gate=mod(0, 2), nwy=nw[1],
                        nwh=nw[2], shift=mod(0, 3), scale=mod(0, 4), tr=tr)
    f = _conv_ffn(h, ffn_w_up, ffn_conv_w, ffn_conv_b, ffn_w_down, layer=0, n_ctx_rows=n_ctx,
                  ctx_len=ctx_len, seq_len=seq, tm=tm)
    nw1 = norm_w[1]
    tok, h = row_update(tok, group_all, n_tok, y=f, gate=mod(0, 5), nwy=nw[3],
                        nwh=nw1[0], shift=mod(1, 0), scale=mod(1, 1), tr=tr)

    gq = GLA_HEADS * GLA_QK
    gv = GLA_HEADS * GLA_V
    wide = 2 * gq + 2 * gv
    w_in = jnp.swapaxes(gla_w_in, 1, 2)
    p = matmul(h, w_in, layer=0, m_blocks=n_tok // tm, n_blocks=wide // 512, tm=tm, tn=512,
               out_dtype=BF16, w_is_nk=True)
    lowrank = matmul(h, w_in, layer=0, m_blocks=n_tok // tm, n_blocks=1, tm=tm, tn=LANE,
                     n_off=wide // LANE, n_valid=gla_w_in.shape[2] - wide, w_is_nk=True)
    scan = functools.partial(gla, p, lowrank, gla_gate_w2[0], gla_gate_b[0], batch=batch, seq=seq,
                             ctx_len=ctx_len, q_col=0, k_col=gq, v_col=2 * gq)
    o_all = scan(reverse=True, other=scan(reverse=False), out_gate_col=2 * gq + gv, gain=gla_head_norm[0])
    y = matmul(o_all, gla_w_out, layer=0, m_blocks=n_lat // tm, n_blocks=d // 512, tm=tm, tn=512,
               out_dtype=BF16)
    xl, h = row_update(tok, group_lat, n_lat, x_row_off=n_ctx, y=y, gate=mod(1, 2), nwy=nw1[1],
                       nwh=nw1[2], shift=mod(1, 3), scale=mod(1, 4), tr=tr)
    f = _conv_ffn(h, ffn_w_up, ffn_conv_w, ffn_conv_b, ffn_w_down, layer=1, n_ctx_rows=0,
                  ctx_len=ctx_len, seq_len=seq, tm=tm)
    (out,) = row_update(xl, group_lat, n_lat, y=f, gate=mod(1, 5), nwy=nw1[3], tr=tr)
    return out.reshape(batch, seq, d)
```

```python
import functools
import math

import jax
import jax.numpy as jnp
from jax import lax
from jax.experimental import pallas as pl
from jax.experimental.pallas import tpu as pltpu

F32 = jnp.float32
BF16 = jnp.bfloat16
CARRY = jnp.bfloat16

NORM_EPS = 1e-6
ROPE_THETA = 10000.0
GRID_W = 64
ATTN_HEAD_DIM = 128
ATTN_GROUP = 4
ATTN_BLOCK = 128
MLSTM_HEADS = 4
MLSTM_QK = 256
MLSTM_V = 512
MLSTM_CHUNK = 128
GLA_HEADS = 8
GLA_QK = 256
GLA_V = 512
GLA_CHUNK = 64
GLA_GATE_TEMP = 16.0
GLA_GATE_RANK = 16
LANE = 128
MXU_DEPTH = 256
MIB = 1024 * 1024
VMEM_CAP = 60 * MIB


def _cparams(n_axes, vmem_bytes, flags=None):
    return pltpu.CompilerParams(
        dimension_semantics=("arbitrary",) * n_axes,
        vmem_limit_bytes=int(min(VMEM_CAP, vmem_bytes)),
        flags=flags,
    )


def _sigmoid(x):
    return 1.0 / (1.0 + jnp.exp(-x))


def _log_sigmoid(x):
    return jnp.minimum(x, 0.0) - jnp.log(1.0 + jnp.exp(-jnp.abs(x)))


def _split3(x):
    hi = x.astype(BF16)
    r1 = x - hi.astype(F32)
    mid = r1.astype(BF16)
    lo = (r1 - mid.astype(F32)).astype(BF16)
    return hi, mid, lo


def _ada_body(s_ref, w_ref, b_ref, o_ref):
    s = s_ref[...]
    s = (s * _sigmoid(s)).astype(BF16)
    o_ref[...] = jnp.dot(s, w_ref[...].astype(BF16), preferred_element_type=F32) + b_ref[...]


def ada_mods(cvec, ada_w, ada_b, tn=1024):
    depth, d, n = ada_w.shape
    rows = cvec.shape[0]
    return pl.pallas_call(
        _ada_body,
        out_shape=jax.ShapeDtypeStruct((depth, rows, n), F32),
        grid=(depth, n // tn),
        in_specs=[
            pl.BlockSpec((rows, d), lambda i, j: (0, 0)),
            pl.BlockSpec((None, d, tn), lambda i, j: (i, 0, j)),
            pl.BlockSpec((None, 1, tn), lambda i, j: (i, 0, j)),
        ],
        out_specs=pl.BlockSpec((None, rows, tn), lambda i, j: (i, 0, j)),
        compiler_params=_cparams(2, 2 * d * tn * 4 + d * tn * 2 + 8 * MIB),
        name="ada_mods",
    )(cvec, ada_w, ada_b.reshape(depth, 1, n))


def _rms(x, w):
    return x * lax.rsqrt(jnp.mean(x * x, axis=-1, keepdims=True) + NORM_EPS) * w


def _row_body(*refs, has_y, has_h, split):
    it = iter(refs)
    x_refs = [next(it)] if split is None else [next(it), next(it)]
    y_ref, gate_ref, nwy_ref = (next(it), next(it), next(it)) if has_y else (None, None, None)
    nwh_ref, shift_ref, scale_ref = (next(it), next(it), next(it)) if has_h else (None, None, None)
    xo_ref = next(it) if has_y else None
    ho_ref = next(it) if has_h else None

    def run(x_ref):
        x = x_ref[...].astype(F32)
        if has_y:
            x = x + gate_ref[...] * _rms(y_ref[...].astype(F32), nwy_ref[...])
            xo_ref[...] = x.astype(xo_ref.dtype)
        if has_h:
            ho_ref[...] = (_rms(x, nwh_ref[...]) * (1.0 + scale_ref[...]) + shift_ref[...]).astype(BF16)

    if split is None:
        run(x_refs[0])
    else:
        pl.when(pl.program_id(0) < split)(lambda: run(x_refs[0]))
        pl.when(pl.program_id(0) >= split)(lambda: run(x_refs[1]))


def row_update(x, group_of_tile, n_rows, *, x_tail=None, x_row_off=0, y=None, gate=None, nwy=None,
               nwh=None, shift=None, scale=None, tr=256, x_out_dtype=F32):
    d = x.shape[1]
    has_y, has_h = y is not None, nwh is not None
    off = x_row_off // tr
    row = pl.BlockSpec((tr, d), lambda t: (t, 0))
    vec = pl.BlockSpec((1, d), lambda t: (0, 0))
    tab = pl.BlockSpec((None, 1, d), lambda t: (group_of_tile(t), 0, 0))
    split = None
    if x_tail is None:
        args, specs = [x], [pl.BlockSpec((tr, d), lambda t: (t + off, 0))]
    else:
        split = x.shape[0] // tr
        args = [x, x_tail]
        specs = [pl.BlockSpec((tr, d), lambda t: (jnp.minimum(t, split - 1), 0)),
                 pl.BlockSpec((tr, d), lambda t: (jnp.maximum(t - split, 0), 0))]
    if has_y:
        args += [y, gate, nwy.reshape(1, d)]
        specs += [row, tab, vec]
    if has_h:
        args += [nwh.reshape(1, d), shift, scale]
        specs += [vec, tab, tab]
    out_shape, out_specs = [], []
    if has_y:
        out_shape.append(jax.ShapeDtypeStruct((n_rows, d), x_out_dtype))
        out_specs.append(row)
    if has_h:
        out_shape.append(jax.ShapeDtypeStruct((n_rows, d), BF16))
        out_specs.append(row)
    return pl.pallas_call(
        functools.partial(_row_body, has_y=has_y, has_h=has_h, split=split),
        out_shape=out_shape,
        grid=(n_rows // tr,),
        in_specs=specs,
        out_specs=out_specs,
        compiler_params=_cparams(1, 2 * tr * d * 22 + 8 * MIB),
        name="row_update",
    )(*args)


A_SLOTS = 3


def _mm_body(*refs, n_a, n_valid, has_acc, w_is_nk, tm, tk, k_block, row_off, w_copy):
    a_hbms = refs[:n_a]
    w_ref = refs[n_a]
    acc_ref = refs[n_a + 1] if has_acc else None
    o_ref, wbf_ref = refs[n_a + 1 + has_acc], refs[n_a + 2 + has_acc]
    rest = refs[n_a + 3 + has_acc:]
    if w_copy is not None:
        w_stage, w_sem, rest = rest[-2], rest[-1], rest[:-2]
    a_bufs, a_sem = rest[:-1], rest[-1]
    n_blk, m_blocks = pl.program_id(0), pl.num_programs(1)
    s = n_blk * m_blocks + pl.program_id(1)
    last = pl.num_programs(0) * m_blocks - 1

    def a_copies(step):
        row0 = (step % m_blocks) * tm + row_off
        slot = step % A_SLOTS
        return [pltpu.make_async_copy(
            a_hbm.at[pl.ds(row0, tm), pl.ds(k_block * tk, tk) if n_a == 1 else slice(None)],
            a_buf.at[slot], a_sem.at[i, slot]) for i, (a_hbm, a_buf) in enumerate(zip(a_hbms, a_bufs))]

    @pl.when(s == 0)
    def _():
        for c in a_copies(0) + a_copies(1):
            c.start()

    @pl.when(s + 2 <= last)
    def _():
        for c in a_copies(s + 2):
            c.start()

    for c in a_copies(s):
        c.wait()

    if w_copy is not None:
        def w_dma(blk):
            return pltpu.make_async_copy(w_copy(w_ref, blk), w_stage.at[blk % 2], w_sem.at[blk % 2])

        @pl.when(s == 0)
        def _():
            w_dma(0).start()

    @pl.when(pl.program_id(1) == 0)
    def _():
        if w_copy is not None:
            @pl.when(n_blk + 1 < pl.num_programs(0))
            def _():
                w_dma(n_blk + 1).start()

            w_dma(n_blk).wait()
            w = w_stage[n_blk % 2]
        else:
            w = w_ref[...]
        if n_valid is not None:
            col = lax.broadcasted_iota(jnp.int32, w.shape, 0 if w_is_nk else 1)
            w = jnp.where(col < n_valid, w, 0.0)
        wbf_ref[...] = w.astype(BF16)

    r, k0 = None, 0
    for a_buf in a_bufs:
        a = a_buf[s % A_SLOTS]
        k1 = k0 + a.shape[1]
        if w_is_nk:
            part = lax.dot_general(a, wbf_ref[:, k0:k1], (((1,), (1,)), ((), ())), preferred_element_type=F32)
        else:
            part = jnp.dot(a, wbf_ref[k0:k1, :], preferred_element_type=F32)
        r = part if r is None else r + part
        k0 = k1
    if has_acc:
        r = r + acc_ref[...]
    o_ref[...] = r.astype(o_ref.dtype)


def matmul(a, w, *, layer, m_blocks, n_blocks, tm, tn, tk=None, k_block=0, a_row_off=0, n_off=0,
           n_valid=None, acc=None, out_dtype=F32, w_is_nk=False):
    a_list = list(a) if isinstance(a, (list, tuple)) else [a]
    k_total = sum(x.shape[1] for x in a_list)
    tk = k_total if tk is None else tk
    assert len(a_list) == 1 or tk == k_total
    assert n_blocks * m_blocks >= 2
    widths = [tk] if len(a_list) == 1 else [x.shape[1] for x in a_list]
    in_specs = [pl.BlockSpec(memory_space=pl.ANY) for _ in a_list]
    w_block = (tn, tk) if w_is_nk else (tk, tn)
    w_copy, w_scratch = None, []
    if n_valid is not None:
        in_specs.append(pl.BlockSpec((None,) + w_block, (lambda n, m: (layer, n + n_off, k_block)) if w_is_nk
                                     else (lambda n, m: (layer, k_block, n + n_off))))
    else:
        in_specs.append(pl.BlockSpec(memory_space=pl.ANY))
        w_scratch = [pltpu.VMEM((2,) + w_block, w.dtype), pltpu.SemaphoreType.DMA((2,))]

        def w_copy(w_hbm, blk):
            rows, cols = pl.ds(k_block * tk, tk), pl.ds((blk + n_off) * tn, tn)
            return w_hbm.at[layer, cols, rows] if w_is_nk else w_hbm.at[layer, rows, cols]
    args = a_list + [w]
    out_spec = pl.BlockSpec((tm, tn), lambda n, m: (m, n))
    if acc is not None:
        in_specs.append(out_spec)
        args.append(acc)
    osz = jnp.dtype(out_dtype).itemsize
    vmem = (A_SLOTS * tm * tk * 2 + 2 * tk * tn * 4 + tk * tn * 2 + 2 * tm * tn * osz
            + (2 * tm * tn * 4 if acc is not None else 0) + 2 * tm * tn * 4 + 4 * MIB)
    return pl.pallas_call(
        functools.partial(_mm_body, n_a=len(a_list), n_valid=n_valid, has_acc=acc is not None, w_is_nk=w_is_nk,
                          tm=tm, tk=tk, k_block=k_block, row_off=a_row_off, w_copy=w_copy),
        out_shape=jax.ShapeDtypeStruct((m_blocks * tm, n_blocks * tn), out_dtype),
        grid=(n_blocks, m_blocks),
        in_specs=in_specs,
        out_specs=out_spec,
        scratch_shapes=([pltpu.VMEM(w_block, BF16)]
                        + [pltpu.VMEM((A_SLOTS, tm, wd), BF16) for wd in widths]
                        + [pltpu.SemaphoreType.DMA((len(a_list), A_SLOTS))] + w_scratch),
        compiler_params=_cparams(2, vmem),
        name="matmul",
    )(*args)


FFN_LAG = 2
FFN_RING = FFN_LAG + 1


def _ffn_up_body(a_hbm, w_hbm, cwg_ref, cwv_ref, cbg_ref, cbv_ref, o_ref, wg_bf, wv_bf, ug_ref, uv_ref,
                 a_buf, a_sem, w_stage, w_sem, *, layer, nb, tm, tiles, total, n_ctx_rows, ctx_len, seq_len):
    s = pl.program_id(0)
    last = total + FFN_LAG - 1
    m = jnp.minimum(s, total - 1) % tiles
    m_out = jnp.maximum(s - FFN_LAG, 0) % tiles

    def a_copy(step):
        tile = jnp.minimum(step, total - 1) % tiles
        slot = step % A_SLOTS
        return pltpu.make_async_copy(a_hbm.at[pl.ds(tile * tm, tm), :], a_buf.at[slot], a_sem.at[slot])

    @pl.when(s == 0)
    def _():
        a_copy(0).start()
        a_copy(1).start()

    @pl.when(s + 2 <= last)
    def _():
        a_copy(s + 2).start()

    a_copy(s).wait()

    @pl.when(s == 0)
    def _():
        ug_ref[...] = jnp.zeros_like(ug_ref)
        uv_ref[...] = jnp.zeros_like(uv_ref)

    n_blk = jnp.minimum(s, total - 1) // tiles
    tn = wg_bf.shape[1]

    def w_dmas(blk):
        slot = blk % 2
        return [pltpu.make_async_copy(w_hbm.at[layer, :, pl.ds((half * nb + blk) * tn, tn)],
                                      w_stage.at[slot, half], w_sem.at[slot, half]) for half in range(2)]

    @pl.when(s == 0)
    def _():
        for c in w_dmas(0):
            c.start()

    @pl.when((m == 0) & (s < total))
    def _():
        @pl.when(n_blk + 1 < nb)
        def _():
            for c in w_dmas(n_blk + 1):
                c.start()

        for c in w_dmas(n_blk):
            c.wait()
        wg_bf[...] = w_stage[n_blk % 2, 0].astype(BF16)
        wv_bf[...] = w_stage[n_blk % 2, 1].astype(BF16)

    slot_new = s % FFN_RING
    slot_mid = (s + 1) % FFN_RING
    slot_next = (s + 2) % FFN_RING
    k_tiles = a_buf.shape[2] // MXU_DEPTH
    chunk = tm // k_tiles
    unit = min(chunk, ctx_len)
    row8 = lax.broadcasted_iota(jnp.int32, (8, o_ref.shape[1]), 0)

    def starts_sequence(g):
        return (g <= n_ctx_rows) | (((g - n_ctx_rows) & (seq_len - 1)) == 0)

    def conv(u_ref, cw_ref, cb_ref, r0):
        r1 = r0 + unit
        above = u_ref[slot_new, tm - 1:tm, :] if r0 == 0 else u_ref[slot_mid, r0 - 1:r0, :]
        below = u_ref[slot_next, 0:1, :] if r1 == tm else u_ref[slot_mid, r1:r1 + 1, :]
        if r0 % ctx_len == 0:
            above = jnp.where(starts_sequence(m_out * tm + r0), 0.0, above)
        if r1 % ctx_len == 0:
            below = jnp.where(starts_sequence(m_out * tm + r1), 0.0, below)
        u = u_ref[slot_mid, r0:r1, :]
        down, up = pltpu.roll(u, 1, 0), pltpu.roll(u, unit - 1, 0)
        u_prev = jnp.concatenate([jnp.where(row8 == 0, above, down[0:8]), down[8:]], axis=0)
        u_next = jnp.concatenate([up[:unit - 8], jnp.where(row8 == 7, below, up[unit - 8:])], axis=0)
        cw = cw_ref[...]
        return cb_ref[...] + u_prev * cw[0:1] + u * cw[1:2] + u_next * cw[2:3]

    sixteen = jnp.uint32(16)
    zeros = []
    for j in range(k_tiles):
        seen = None
        for r0 in range(j * chunk, (j + 1) * chunk, unit):
            gate = conv(ug_ref, cwg_ref, cbg_ref, r0)
            val = conv(uv_ref, cwv_ref, cbv_ref, r0)
            out = gate * _sigmoid(gate) * val
            o_ref[r0:r0 + unit, :] = out.astype(o_ref.dtype)
            bits = lax.bitcast_convert_type(out, jnp.uint32)
            for r in range(0, unit, 8):
                for c in range(0, bits.shape[1], LANE):
                    blk = bits[r:r + 8, c:c + LANE]
                    seen = blk if seen is None else seen | blk
        zeros.append(pltpu.bitcast(lax.shift_right_logical(lax.shift_right_logical(seen, sixteen), sixteen), BF16))

    def paced(w):
        parts = [w[0:MXU_DEPTH]]
        zr = zeros[0].shape[0]
        for k in range(1, k_tiles):
            blk = w[k * MXU_DEPTH:(k + 1) * MXU_DEPTH]
            head = jnp.concatenate([blk[0:zr, :LANE] + zeros[k - 1], blk[0:zr, LANE:]], axis=1)
            parts.append(jnp.concatenate([head, blk[zr:]], axis=0))
        return jnp.concatenate(parts, axis=0)

    a = a_buf[s % A_SLOTS]
    ug_ref[slot_new] = jnp.dot(a, paced(wg_bf[...]), preferred_element_type=F32)
    uv_ref[slot_new] = jnp.dot(a, paced(wv_bf[...]), preferred_element_type=F32)


def ffn_up(h, w_up, conv_w, conv_b, *, layer, tm, n_ctx_rows, ctx_len, seq_len, tn=256):
    m_rows, k = h.shape
    tiles = m_rows // tm
    d_ff = w_up.shape[2] // 2
    nb = d_ff // tn
    total = nb * tiles
    assert tm % ctx_len == 0 and seq_len % ctx_len == 0 and n_ctx_rows % ctx_len == 0 and tn % LANE == 0
    chunk = tm // (k // MXU_DEPTH)
    assert k % MXU_DEPTH == 0 and chunk % 8 == 0 and (chunk % ctx_len == 0 or ctx_len % chunk == 0)
    depth = w_up.shape[0]
    cb = conv_b.reshape(depth, 1, 2 * d_ff)

    def out(s):
        so = jnp.maximum(s - FFN_LAG, 0)
        return so // tiles, so % tiles

    vmem = (A_SLOTS * tm * k * 2 + 2 * 2 * k * tn * 4 + 2 * k * tn * 2 + 2 * FFN_RING * tm * tn * 4
            + 2 * tm * tn * 2 + 12 * tm * tn * 4 + 4 * MIB)
    return pl.pallas_call(
        functools.partial(_ffn_up_body, layer=layer, nb=nb, tm=tm, tiles=tiles, total=total,
                          n_ctx_rows=n_ctx_rows, ctx_len=ctx_len, seq_len=seq_len),
        out_shape=jax.ShapeDtypeStruct((m_rows, d_ff), BF16),
        grid=(total + FFN_LAG,),
        in_specs=[
            pl.BlockSpec(memory_space=pl.ANY),
            pl.BlockSpec(memory_space=pl.ANY),
            pl.BlockSpec((None, 3, tn), lambda s: (layer, 0, out(s)[0])),
            pl.BlockSpec((None, 3, tn), lambda s: (layer, 0, out(s)[0] + nb)),
            pl.BlockSpec((None, 1, tn), lambda s: (layer, 0, out(s)[0])),
            pl.BlockSpec((None, 1, tn), lambda s: (layer, 0, out(s)[0] + nb)),
        ],
        out_specs=pl.BlockSpec((tm, tn), lambda s: (out(s)[1], out(s)[0])),
        scratch_shapes=[pltpu.VMEM((k, tn), BF16), pltpu.VMEM((k, tn), BF16),
                        pltpu.VMEM((FFN_RING, tm, tn), F32), pltpu.VMEM((FFN_RING, tm, tn), F32),
                        pltpu.VMEM((A_SLOTS, tm, k), BF16), pltpu.SemaphoreType.DMA((A_SLOTS,)),
                        pltpu.VMEM((2, 2, k, tn), w_up.dtype), pltpu.SemaphoreType.DMA((2, 2))],
        compiler_params=_cparams(1, vmem),
        name="ffn_up",
    )(h, w_up, conv_w, conv_w, cb, cb)


def _rope(x, cos, sin_signed):
    lane = lax.broadcasted_iota(jnp.int32, x.shape, 1)
    q = ATTN_HEAD_DIM // 4
    partner = jnp.where((lane & (2 * q - 1)) < q, pltpu.roll(x, ATTN_HEAD_DIM - q, 1), pltpu.roll(x, q, 1))
    return x * cos + partner * sin_signed


def _sink_attention(q4, kcat, vcat, sink_col, bias=None, score_scale=None):
    s = lax.dot_general(q4, kcat, (((1,), (1,)), ((), ())), preferred_element_type=F32)
    if score_scale is not None:
        s = s * score_scale
    if bias is not None:
        s = s + bias
    m = jnp.maximum(jnp.max(s, axis=-1, keepdims=True), sink_col)
    e = jnp.exp(s - m)
    denom = jnp.sum(e, axis=-1, keepdims=True) + jnp.exp(sink_col - m)
    return jnp.dot(e.astype(BF16), vcat, preferred_element_type=F32) * (1.0 / denom)


def _sink_column(sink_ref, j, rows):
    return jnp.concatenate(
        [jnp.full((rows, 1), sink_ref[j * ATTN_GROUP + g], F32) for g in range(ATTN_GROUP)], axis=0)


def _win_attn_body(sink_ref, q_ref, kp_ref, kc_ref, kn_ref, vp_ref, vc_ref, vn_ref, kx_ref, vx_ref,
                   cos_ref, sin_ref, band_ref, prev_ref, o_ref, *, nb, n_kv):
    del prev_ref
    n = pl.program_id(1)
    blk, dh = ATTN_BLOCK, ATTN_HEAD_DIM
    qw = ATTN_GROUP * dh
    scale = dh ** -0.5

    def table(ref, i):
        return ref[pl.ds(pl.multiple_of(i * blk, blk), blk), :]

    n_prev, n_next = jnp.maximum(n - 1, 0), jnp.minimum(n + 1, nb - 1)
    cos_q, sin_q = table(cos_ref, n), table(sin_ref, n)
    cos_p, sin_p = table(cos_ref, n_prev), table(sin_ref, n_prev)
    cos_n, sin_n = table(cos_ref, n_next), table(sin_ref, n_next)
    n_keys = 3 * blk + kx_ref.shape[0]
    kj = lax.broadcasted_iota(jnp.int32, (1, n_keys), 1)
    lo = jnp.where(n > 0, 0, blk)
    hi = jnp.where(n < nb - 1, 3 * blk, 2 * blk)
    bias = band_ref[...] + jnp.where(((kj >= lo) & (kj < hi)) | (kj >= 3 * blk), 0.0, -jnp.inf)

    for j in range(n_kv):
        ksl = slice(j * dh, (j + 1) * dh)
        q = q_ref[:, j * qw:(j + 1) * qw].astype(F32)
        q4 = jnp.concatenate(
            [_rope(q[:, g * dh:(g + 1) * dh], cos_q, sin_q) * scale for g in range(ATTN_GROUP)],
            axis=0).astype(BF16)
        kcat = jnp.concatenate([
            _rope(kp_ref[:, ksl].astype(F32), cos_p, sin_p).astype(BF16),
            _rope(kc_ref[:, ksl].astype(F32), cos_q, sin_q).astype(BF16),
            _rope(kn_ref[:, ksl].astype(F32), cos_n, sin_n).astype(BF16),
            kx_ref[:, ksl]], axis=0)
        vcat = jnp.concatenate([vp_ref[:, ksl], vc_ref[:, ksl], vn_ref[:, ksl], vx_ref[:, ksl]], axis=0)
        o = _sink_attention(q4, kcat, vcat, _sink_column(sink_ref, j, blk), bias=bias)
        for g in range(ATTN_GROUP):
            c0 = j * qw + g * dh
            o_ref[:, c0:c0 + dh] = o[g * blk:(g + 1) * blk].astype(o_ref.dtype)


def window_attention(p, sink, cos, sin_signed, prev, *, batch, seq, ctx_len, q_col, k_col, v_col):
    blk, dh = ATTN_BLOCK, ATTN_HEAD_DIM
    nb = seq // blk
    n_kv = (k_col - q_col) // (ATTN_GROUP * dh)
    lat0 = batch * ctx_len // blk
    qw = n_kv * ATTN_GROUP * dh
    kw = n_kv * dh

    def kv_spec(col, shift):
        return pl.BlockSpec((blk, kw), lambda b, n: (lat0 + b * nb + jnp.clip(n + shift, 0, nb - 1), col // kw))

    rows, n_keys = ATTN_GROUP * blk, 3 * blk + ctx_len
    qi = jnp.arange(rows)[:, None] % blk
    kj = jnp.arange(n_keys)[None, :]
    band = jnp.where(((kj >= qi) & (kj <= qi + 2 * blk)) | (kj >= 3 * blk), 0.0, -jnp.inf).astype(F32)

    in_specs = [
        pl.BlockSpec(memory_space=pltpu.SMEM),
        pl.BlockSpec((blk, qw), lambda b, n: (lat0 + b * nb + n, q_col // qw)),
        kv_spec(k_col, -1), kv_spec(k_col, 0), kv_spec(k_col, 1),
        kv_spec(v_col, -1), kv_spec(v_col, 0), kv_spec(v_col, 1),
        pl.BlockSpec((ctx_len, kw), lambda b, n: (b, k_col // kw)),
        pl.BlockSpec((ctx_len, kw), lambda b, n: (b, v_col // kw)),
        pl.BlockSpec((seq, dh), lambda b, n: (0, 0)),
        pl.BlockSpec((seq, dh), lambda b, n: (0, 0)),
        pl.BlockSpec((rows, n_keys), lambda b, n: (0, 0)),
        pl.BlockSpec(memory_space=pl.ANY),
    ]
    return pl.pallas_call(
        functools.partial(_win_attn_body, nb=nb, n_kv=n_kv),
        out_shape=jax.ShapeDtypeStruct(prev.shape, prev.dtype),
        grid=(batch, nb),
        in_specs=in_specs,
        out_specs=pl.BlockSpec((blk, qw), lambda b, n: (lat0 + b * nb + n, 0)),
        input_output_aliases={13: 0},
        compiler_params=_cparams(2, 32 * MIB),
        name="window_attention",
    )(sink, p, p, p, p, p, p, p, p, p, cos, sin_signed, band, prev)


def _ctx_attn_body(sink_ref, q_ref, k_ref, v_ref, o_ref, *, n_kv, batch):
    dh = ATTN_HEAD_DIM
    qw = ATTN_GROUP * dh
    rows = q_ref.shape[0]

    @pl.when(pl.program_id(0) < batch)
    def _():
        for j in range(n_kv):
            ksl = slice(j * dh, (j + 1) * dh)
            q4 = jnp.concatenate(
                [q_ref[:, j * qw + g * dh:j * qw + (g + 1) * dh] for g in range(ATTN_GROUP)], axis=0)
            o = _sink_attention(q4, k_ref[:, ksl], v_ref[:, ksl], _sink_column(sink_ref, j, rows),
                                score_scale=dh ** -0.5)
            for g in range(ATTN_GROUP):
                c0 = j * qw + g * dh
                o_ref[:, c0:c0 + dh] = o[g * rows:(g + 1) * rows].astype(o_ref.dtype)

    @pl.when(pl.program_id(0) >= batch)
    def _():
        o_ref[...] = jnp.zeros_like(o_ref)


def context_attention(p, sink, *, batch, ctx_len, q_col, k_col, v_col):
    dh = ATTN_HEAD_DIM
    n_kv = (k_col - q_col) // (ATTN_GROUP * dh)
    qw = n_kv * ATTN_GROUP * dh
    kw = n_kv * dh
    last = batch - 1
    return pl.pallas_call(
        functools.partial(_ctx_attn_body, n_kv=n_kv, batch=batch),
        out_shape=jax.ShapeDtypeStruct((p.shape[0], qw), BF16),
        grid=(p.shape[0] // ctx_len,),
        in_specs=[
            pl.BlockSpec(memory_space=pltpu.SMEM),
            pl.BlockSpec((ctx_len, qw), lambda b: (jnp.minimum(b, last), q_col // qw)),
            pl.BlockSpec((ctx_len, kw), lambda b: (jnp.minimum(b, last), k_col // kw)),
            pl.BlockSpec((ctx_len, kw), lambda b: (jnp.minimum(b, last), v_col // kw)),
        ],
        out_specs=pl.BlockSpec((ctx_len, qw), lambda b: (b, 0)),
        compiler_params=_cparams(1, 32 * MIB),
        name="context_attention",
    )(sink, p, p, p)


def _scan_row_block(c, b, batch, n_ctx_blocks, n_lat_blocks, reverse):
    is_ctx = c < n_ctx_blocks
    i = jnp.where(is_ctx, c, c - n_ctx_blocks)
    if reverse:
        i = jnp.where(is_ctx, n_ctx_blocks - 1, n_lat_blocks - 1) - i
    return jnp.where(is_ctx, b * n_ctx_blocks + i, batch * n_ctx_blocks + b * n_lat_blocks + i)


def _head_finish(h_own, h_other, gate, gain, silu_gate):
    h = h_own + h_other.astype(F32)
    hn = h * lax.rsqrt(jnp.mean(h * h, axis=-1, keepdims=True) + NORM_EPS)
    g = gate.astype(F32)
    return hn * gain * (g * _sigmoid(g) if silu_gate else _sigmoid(g))


def _mlstm_body(q_ref, k_ref, v0_ref, v1_ref, g_ref, gt_ref, gb_row_ref, gb_col_ref, *rest, reverse, finish):
    if finish:
        other_ref, og0_ref, og1_ref, gain_ref, o_ref, cn_ref, m_ref = rest
    else:
        o_ref, cn_ref, m_ref = rest
    c = pl.program_id(1)
    ch, dk, dv, nh = MLSTM_CHUNK, MLSTM_QK, MLSTM_V, MLSTM_HEADS

    @pl.when(c == 0)
    def _():
        cn_ref[...] = jnp.zeros_like(cn_ref)
        m_ref[...] = jnp.zeros_like(m_ref)

    t = lax.broadcasted_iota(jnp.int32, (ch, ch), 0)
    s = lax.broadcasted_iota(jnp.int32, (ch, ch), 1)
    valid = (s >= t) if reverse else (s <= t)
    tri = jnp.where(valid, 1.0, 0.0).astype(BF16)
    last_t = 0 if reverse else ch - 1
    col_i = 2 * nh if reverse else 0
    col_f = 3 * nh if reverse else nh

    g_col = g_ref[...] + gb_row_ref[...]
    g_row = gt_ref[...] + gb_col_ref[...]
    lf_col, lf_row = _log_sigmoid(g_col), _log_sigmoid(g_row)
    bc_col_all = sum(jnp.dot(tri, part, preferred_element_type=F32) for part in _split3(lf_col))
    bc_row_all = sum(lax.dot_general(part, tri, (((1,), (1,)), ((), ())), preferred_element_type=F32)
                     for part in _split3(lf_row))
    ones_col = jnp.where(lax.broadcasted_iota(jnp.int32, (ch, LANE), 1) == 0, 1.0, 0.0)

    for h in range(nh):
        ig_col = g_col[:, col_i + h:col_i + h + 1]
        ig_row = g_row[col_i + h:col_i + h + 1, :]
        bc_col = bc_col_all[:, col_f + h:col_f + h + 1]
        bc_row = bc_row_all[col_f + h:col_f + h + 1, :]
        m_st = m_ref[h:h + 1, 0:1]
        q = q_ref[:, h * dk:(h + 1) * dk]
        k = k_ref[:, h * dk:(h + 1) * dk] * (dk ** -0.5)
        v_ref = v0_ref if h < nh // 2 else v1_ref
        hv = h % (nh // 2)
        v = v_ref[:, hv * dv:(hv + 1) * dv]

        log_d = jnp.where(valid, bc_col - bc_row + ig_row, -jnp.inf)
        inter = bc_col + m_st
        m_t = jnp.maximum(inter, jnp.max(log_d, axis=-1, keepdims=True))
        sc = lax.dot_general(q, k, (((1,), (1,)), ((), ())), preferred_element_type=F32) * jnp.exp(log_d - m_t)
        w_inter = jnp.exp(inter - m_t)
        state = cn_ref[h]
        q_state = jnp.dot(q, state.astype(BF16), preferred_element_type=F32)
        num = jnp.dot(sc.astype(BF16), v, preferred_element_type=F32) + w_inter * q_state[:, :dv]
        den = jnp.sum(sc, axis=-1, keepdims=True) + w_inter * q_state[:, dv:dv + 1]
        h_t = num * (1.0 / jnp.maximum(jnp.abs(den), jnp.exp(-m_t)))
        if finish:
            og_ref = og0_ref if h < nh // 2 else og1_ref
            h_t = _head_finish(h_t, other_ref[:, h * dv:(h + 1) * dv], og_ref[:, hv * dv:(hv + 1) * dv],
                               gain_ref[:, h * dv:(h + 1) * dv], False)
        o_ref[:, h * dv:(h + 1) * dv] = h_t.astype(o_ref.dtype)

        b_end = bc_col[last_t:last_t + 1, :]
        m_new = m_t[last_t:last_t + 1, :]
        wk = jnp.exp(b_end - bc_col + ig_col - m_new)
        decay = jnp.exp(b_end + m_st - m_new)
        v_ext = jnp.concatenate([v.astype(F32), ones_col], axis=1) * wk
        cn_ref[h] = decay * state + lax.dot_general(
            k, v_ext.astype(BF16), (((0,), (0,)), ((), ())), preferred_element_type=F32)
        m_ref[h:h + 1, :] = jnp.broadcast_to(m_new, (1, LANE))


def mlstm(p, gates, gate_b, *, reverse, batch, seq, ctx_len, q_col, k_col, v_col,
          other=None, out_gate_col=None, gain=None):
    ch, nh, dk, dv = MLSTM_CHUNK, MLSTM_HEADS, MLSTM_QK, MLSTM_V
    ncc, nlc = ctx_len // ch, seq // ch
    rows = p.shape[0]
    blk = functools.partial(_scan_row_block, batch=batch, n_ctx_blocks=ncc, n_lat_blocks=nlc, reverse=reverse)
    gb = jnp.zeros((LANE,), F32).at[:4 * nh].set(gate_b.astype(F32))
    qw, vw = nh * dk, nh * dv // 2
    finish = other is not None
    in_specs = [
        pl.BlockSpec((ch, qw), lambda b, c: (blk(c, b), q_col // qw)),
        pl.BlockSpec((ch, qw), lambda b, c: (blk(c, b), k_col // qw)),
        pl.BlockSpec((ch, vw), lambda b, c: (blk(c, b), v_col // vw)),
        pl.BlockSpec((ch, vw), lambda b, c: (blk(c, b), v_col // vw + 1)),
        pl.BlockSpec((ch, LANE), lambda b, c: (blk(c, b), 0)),
        pl.BlockSpec((LANE, ch), lambda b, c: (0, blk(c, b))),
        pl.BlockSpec((1, LANE), lambda b, c: (0, 0)),
        pl.BlockSpec((LANE, 1), lambda b, c: (0, 0)),
    ]
    args = [p, p, p, p, gates, gates.T, gb.reshape(1, LANE), gb.reshape(LANE, 1)]
    if finish:
        in_specs += [
            pl.BlockSpec((ch, nh * dv), lambda b, c: (blk(c, b), 0)),
            pl.BlockSpec((ch, vw), lambda b, c: (blk(c, b), out_gate_col // vw)),
            pl.BlockSpec((ch, vw), lambda b, c: (blk(c, b), out_gate_col // vw + 1)),
            pl.BlockSpec((1, nh * dv), lambda b, c: (0, 0)),
        ]
        args += [other, p, p, gain.reshape(1, nh * dv).astype(F32)]
    return pl.pallas_call(
        functools.partial(_mlstm_body, reverse=reverse, finish=finish),
        out_shape=jax.ShapeDtypeStruct((rows, nh * dv), BF16),
        grid=(batch, ncc + nlc),
        in_specs=in_specs,
        out_specs=pl.BlockSpec((ch, nh * dv), lambda b, c: (blk(c, b), 0)),
        scratch_shapes=[pltpu.VMEM((nh, dk, dv + LANE), F32), pltpu.VMEM((8, LANE), F32)],
        compiler_params=_cparams(2, 32 * MIB),
        name="mlstm",
    )(*args)


GLA_BLOCK = 2 * GLA_CHUNK


def _gla_body(q_ref, k_ref, v_ref, lr_ref, w2_ref, gb_ref, *rest, reverse, finish):
    if finish:
        other_ref, og_ref, gain_ref, o_ref, st_ref = rest
    else:
        o_ref, st_ref = rest
    c = pl.program_id(1)
    ch, rows, dk, dv, nh = GLA_CHUNK, GLA_BLOCK, GLA_QK, GLA_V, GLA_HEADS

    @pl.when(c == 0)
    def _():
        st_ref[...] = jnp.zeros_like(st_ref)

    t = lax.broadcasted_iota(jnp.int32, (rows, rows), 0)
    s = lax.broadcasted_iota(jnp.int32, (rows, rows), 1)
    same_chunk = (t ^ s) < ch
    intra = same_chunk & ((s >= t) if reverse else (s <= t))
    cross = ((t < ch) & (s >= ch)) if reverse else ((t >= ch) & (s < ch))
    tri = jnp.where(intra, 1.0, 0.0).astype(BF16)
    row = lax.broadcasted_iota(jnp.int32, (rows, 1), 0)
    in_a = (row >= ch) if reverse else (row < ch)

    pre = jnp.dot(lr_ref[...].astype(BF16), w2_ref[...].astype(BF16), preferred_element_type=F32)
    lg = _log_sigmoid(pre + gb_ref[...]) / GLA_GATE_TEMP
    gcum = sum(jnp.dot(tri, part, preferred_element_type=F32) for part in _split3(lg))
    end_a = gcum[ch:ch + 1, :] if reverse else gcum[ch - 1:ch, :]
    end_b = gcum[0:1, :] if reverse else gcum[rows - 1:rows, :]
    d_a, d_b = jnp.exp(end_a), jnp.exp(end_b)
    e_pos, e_neg = jnp.exp(gcum), jnp.exp(-gcum)
    e_end = jnp.exp(jnp.where(in_a, end_a, end_b) - gcum)
    e_pos_state = e_pos * jnp.where(in_a, 1.0, d_a)
    e_end_state = e_end * jnp.where(in_a, d_b, 1.0)
    decay = d_a * d_b

    for h in range(nh):
        sl = slice(h * dk, (h + 1) * dk)
        q = q_ref[:, sl].astype(F32) * (dk ** -0.5)
        k = k_ref[:, sl].astype(F32)
        v = v_ref[:, h * dv:(h + 1) * dv]
        q_dec = (q * e_pos[:, sl]).astype(BF16)
        q_state = (q * e_pos_state[:, sl]).astype(BF16)
        keys = jnp.concatenate([(k * e_neg[:, sl]).astype(BF16), (k * e_end[:, sl]).astype(BF16)], axis=0)
        k_state = (k * e_end_state[:, sl]).astype(BF16)
        sc = lax.dot_general(q_dec, keys, (((1,), (1,)), ((), ())), preferred_element_type=F32)
        att = jnp.where(intra, sc[:, :rows], jnp.where(cross, sc[:, rows:], 0.0)).astype(BF16)
        st = st_ref[h]
        o = (jnp.dot(att, v, preferred_element_type=F32)
             + lax.dot_general(q_state, st.astype(BF16), (((1,), (1,)), ((), ())), preferred_element_type=F32))
        hsl = slice(h * dv, (h + 1) * dv)
        if finish:
            o = _head_finish(o, other_ref[:, hsl], og_ref[:, hsl], gain_ref[:, hsl], True)
        o_ref[:, hsl] = o.astype(o_ref.dtype)
        st_ref[h] = decay[:, sl] * st + lax.dot_general(
            v, k_state, (((0,), (0,)), ((), ())), preferred_element_type=F32)


def gla(p, lowrank, gate_w2, gate_b, *, reverse, batch, seq, ctx_len, q_col, k_col, v_col,
        other=None, out_gate_col=None, gain=None):
    rows, nh, dk, dv = GLA_BLOCK, GLA_HEADS, GLA_QK, GLA_V
    ncb, nlb = ctx_len // rows, seq // rows
    rank = GLA_GATE_RANK
    d = 1 if reverse else 0
    blk = functools.partial(_scan_row_block, batch=batch, n_ctx_blocks=ncb, n_lat_blocks=nlb, reverse=reverse)

    def lat_blk(c, b):
        return blk(jnp.maximum(c, ncb), b)

    def out_blk(c, b):
        return lat_blk(c, b) - batch * ncb

    w2 = jnp.zeros((LANE, nh * dk), F32).at[d * rank:(d + 1) * rank].set(gate_w2[d])
    finish = other is not None
    in_specs = [
        pl.BlockSpec((rows, nh * dk), lambda b, c: (blk(c, b), q_col // (nh * dk))),
        pl.BlockSpec((rows, nh * dk), lambda b, c: (blk(c, b), k_col // (nh * dk))),
        pl.BlockSpec((rows, nh * dv), lambda b, c: (blk(c, b), v_col // (nh * dv))),
        pl.BlockSpec((rows, LANE), lambda b, c: (blk(c, b), 0)),
        pl.BlockSpec((LANE, nh * dk), lambda b, c: (0, 0)),
        pl.BlockSpec((1, nh * dk), lambda b, c: (0, 0)),
    ]
    args = [p, p, p, lowrank, w2, gate_b[d].reshape(1, nh * dk)]
    if finish:
        in_specs += [
            pl.BlockSpec((rows, nh * dv), lambda b, c: (out_blk(c, b), 0)),
            pl.BlockSpec((rows, nh * dv), lambda b, c: (lat_blk(c, b), out_gate_col // (nh * dv))),
            pl.BlockSpec((1, nh * dv), lambda b, c: (0, 0)),
        ]
        args += [other, p, gain.reshape(1, nh * dv).astype(F32)]
    return pl.pallas_call(
        functools.partial(_gla_body, reverse=reverse, finish=finish),
        out_shape=jax.ShapeDtypeStruct((batch * seq, nh * dv), BF16),
        grid=(batch, ncb + nlb),
        in_specs=in_specs,
        out_specs=pl.BlockSpec((rows, nh * dv), lambda b, c: (out_blk(c, b), 0)),
        scratch_shapes=[pltpu.VMEM((nh, dv, dk), F32)],
        compiler_params=_cparams(2, 44 * MIB),
        name="gla",
    )(*args)


def _rope_tables(seq):
    rows = seq // GRID_W
    row = jnp.repeat(jnp.arange(rows), GRID_W).astype(F32)
    col = jnp.tile(jnp.arange(GRID_W), rows).astype(F32)
    n_freq = ATTN_HEAD_DIM // 4
    inv = ROPE_THETA ** (-jnp.arange(n_freq, dtype=F32) / n_freq)
    ang_r, ang_c = row[:, None] * inv, col[:, None] * inv
    cos = jnp.concatenate([jnp.cos(ang_r), jnp.cos(ang_r), jnp.cos(ang_c), jnp.cos(ang_c)], axis=-1)
    sin = jnp.concatenate([-jnp.sin(ang_r), jnp.sin(ang_r), -jnp.sin(ang_c), jnp.sin(ang_c)], axis=-1)
    return cos, sin


def _conv_ffn(h, w_up, conv_w, conv_b, w_down, *, layer, n_ctx_rows, ctx_len, seq_len, tm):
    g = ffn_up(h, w_up, conv_w, conv_b, layer=layer, tm=tm, n_ctx_rows=n_ctx_rows, ctx_len=ctx_len,
               seq_len=seq_len)
    m, d_ff = g.shape
    d = w_down.shape[2]
    half = d_ff // 2
    kw = dict(layer=layer, m_blocks=m // 512, n_blocks=d // 512, tm=512, tn=512, tk=half)
    f = matmul(g, w_down, k_block=0, **kw)
    return matmul(g, w_down, k_block=1, acc=f, out_dtype=BF16, **kw)


def kernel(x, c, ctx, c_ctx, ada_w, ada_b, norm_w, ab_w_in, ab_gate_b, ab_sink, ab_head_norm, ab_w_out,
           gla_w_in, gla_gate_w2, gla_gate_b, gla_head_norm, gla_w_out,
           ffn_w_up, ffn_conv_w, ffn_conv_b, ffn_w_down):
    batch, seq, d = x.shape
    ctx_len = ctx.shape[1]
    n_ctx, n_lat = batch * ctx_len, batch * seq
    n_tok = n_ctx + n_lat
    depth = ada_w.shape[0]
    assert depth == 2 and ab_w_in.shape[0] == 1 and gla_w_in.shape[0] == 1
    tr = 256
    tm = math.gcd(math.gcd(n_ctx, seq), 1024)
    assert tm % 512 == 0 and n_ctx % tr == 0 and seq % tr == 0 and batch + 1 <= 8
    assert ctx_len & (ctx_len - 1) == 0 and seq & (seq - 1) == 0

    cvec = jnp.zeros((8, d), F32).at[:batch].set(c).at[batch].set(c_ctx)
    mods = ada_mods(cvec, ada_w, ada_b).reshape(depth, 8, 6, 1, d)

    def mod(i, j):
        return mods[i, :, j]

    ctx_tiles = n_ctx // tr
    lat_tiles_per_seq = seq // tr

    def group_all(t):
        return jnp.where(t < ctx_tiles, batch, (t - ctx_tiles) // lat_tiles_per_seq)

    def group_lat(t):
        return t // lat_tiles_per_seq

    ctx2, x2 = ctx.reshape(n_ctx, d), x.reshape(n_lat, d)

    nw = norm_w[0]
    (h,) = row_update(ctx2, group_all, n_tok, x_tail=x2, nwh=nw[0], shift=mod(0, 0), scale=mod(0, 1), tr=tr)
    n_kv = ab_sink.shape[1] // ATTN_GROUP
    aq, akv = n_kv * ATTN_GROUP * ATTN_HEAD_DIM, n_kv * ATTN_HEAD_DIM
    mqk, mv = MLSTM_HEADS * MLSTM_QK, MLSTM_HEADS * MLSTM_V
    cols = [0, aq, aq + akv, aq + 2 * akv, aq + 2 * akv + mqk, aq + 2 * akv + 2 * mqk,
            aq + 2 * akv + 2 * mqk + mv, aq + 2 * akv + 2 * mqk + 2 * mv]
    wide = cols[-1]
    w_in = jnp.swapaxes(ab_w_in, 1, 2)
    p = matmul(h, w_in, layer=0, m_blocks=n_tok // tm, n_blocks=wide // 512, tm=tm, tn=512,
               out_dtype=BF16, w_is_nk=True)
    gates = matmul(h, w_in, layer=0, m_blocks=n_tok // tm, n_blocks=1, tm=tm, tn=LANE,
                   n_off=wide // LANE, n_valid=ab_w_in.shape[2] - wide, w_is_nk=True)
    cos, sin = _rope_tables(seq)
    sink = ab_sink[0].astype(F32)
    attn = context_attention(p, sink, batch=batch, ctx_len=ctx_len, q_col=cols[0], k_col=cols[1], v_col=cols[2])
    attn = window_attention(p, sink, cos, sin, attn, batch=batch, seq=seq, ctx_len=ctx_len,
                            q_col=cols[0], k_col=cols[1], v_col=cols[2])
    scan = functools.partial(mlstm, p, gates, ab_gate_b[0], batch=batch, seq=seq, ctx_len=ctx_len,
                             q_col=cols[3], k_col=cols[4], v_col=cols[5])
    m_all = scan(reverse=True, other=scan(reverse=False), out_gate_col=cols[6], gain=ab_head_norm[0])
    y = matmul([attn, m_all], ab_w_out, layer=0, m_blocks=n_tok // tm, n_blocks=d // 512, tm=tm, tn=512,
               out_dtype=BF16)
    tok, h = row_update(ctx2, group_all, n_tok, x_tail=x2, y=y, gate=mod(0, 2), nwy=nw[1],
                        nwh=nw[2], shift=mod(0, 3), scale=mod(0, 4), tr=tr, x_out_dtype=CARRY)
    f = _conv_ffn(h, ffn_w_up, ffn_conv_w, ffn_conv_b, ffn_w_down, layer=0, n_ctx_rows=n_ctx,
                  ctx_len=ctx_len, seq_len=seq, tm=tm)
    nw1 = norm_w[1]
    tok, h = row_update(tok, group_all, n_tok, y=f, gate=mod(0, 5), nwy=nw[3],
                        nwh=nw1[0], shift=mod(1, 0), scale=mod(1, 1), tr=tr, x_out_dtype=CARRY)

    gq = GLA_HEADS * GLA_QK
    gv = GLA_HEADS * GLA_V
    wide = 2 * gq + 2 * gv
    w_in = jnp.swapaxes(gla_w_in, 1, 2)
    p = matmul(h, w_in, layer=0, m_blocks=n_tok // tm, n_blocks=wide // 512, tm=tm, tn=512,
               out_dtype=BF16, w_is_nk=True)
    lowrank = matmul(h, w_in, layer=0, m_blocks=n_tok // tm, n_blocks=1, tm=tm, tn=LANE,
                     n_off=wide // LANE, n_valid=gla_w_in.shape[2] - wide, w_is_nk=True)
    scan = functools.partial(gla, p, lowrank, gla_gate_w2[0], gla_gate_b[0], batch=batch, seq=seq,
                             ctx_len=ctx_len, q_col=0, k_col=gq, v_col=2 * gq)
    o_all = scan(reverse=True, other=scan(reverse=False), out_gate_col=2 * gq + gv, gain=gla_head_norm[0])
    y = matmul(o_all, gla_w_out, layer=0, m_blocks=n_lat // tm, n_blocks=d // 512, tm=tm, tn=512,
               out_dtype=BF16)
    xl, h = row_update(tok, group_lat, n_lat, x_row_off=n_ctx, y=y, gate=mod(1, 2), nwy=nw1[1],
                       nwh=nw1[2], shift=mod(1, 3), scale=mod(1, 4), tr=tr, x_out_dtype=CARRY)
    f = _conv_ffn(h, ffn_w_up, ffn_conv_w, ffn_conv_b, ffn_w_down, layer=1, n_ctx_rows=0,
                  ctx_len=ctx_len, seq_len=seq, tm=tm)
    (out,) = row_update(xl, group_lat, n_lat, y=f, gate=mod(1, 5), nwy=nw1[3], tr=tr)
    return out.reshape(batch, seq, d)
```
